```python
import math
import jax
import jax.numpy as jnp
from jax import lax
import numpy as np

D_MODEL = 1024
BATCH = 4
SEQ = 4096
DEPTH = 2
DEC_BATCH = 16
DEC_SEQ = 32
PAST_LEN = 2048

CHUNK = 64
N_META = 16
N_EVEN = (DEPTH + 1) // 2
N_ODD = DEPTH // 2
EPS = 1e-6

SSD_HEADDIM = 64
SSD_INNER = D_MODEL
SSD_HEADS = SSD_INNER // SSD_HEADDIM
SSD_GROUPS = 2
SSD_STATE = 128
SSD_CONV = 4
SSD_CONV_DIM = SSD_INNER + 2 * SSD_GROUPS * SSD_STATE
RET_HEADS = 8
RET_DK = D_MODEL // RET_HEADS
RET_DV = D_MODEL // RET_HEADS
RET_QK = RET_HEADS * RET_DK
RET_V = RET_HEADS * RET_DV
ROPE_BASE = 10000.0
AB_SPLITS = [SSD_INNER,
             SSD_INNER + SSD_CONV_DIM,
             SSD_INNER + SSD_CONV_DIM + SSD_HEADS,
             SSD_INNER + SSD_CONV_DIM + SSD_HEADS + RET_QK,
             SSD_INNER + SSD_CONV_DIM + SSD_HEADS + 2 * RET_QK,
             SSD_INNER + SSD_CONV_DIM + SSD_HEADS + 2 * RET_QK + RET_V]
AB_IN = AB_SPLITS[-1] + RET_V
AB_OUT = SSD_INNER + RET_V
HG_DK = 128
HG_HEADS = D_MODEL // HG_DK
HG_DV = D_MODEL // HG_HEADS
HG_IN = 4 * D_MODEL
D_FF = 2816
N_EXPERTS = 8
TOP_K = 2
D_FF_EXPERT = 3584

kernel_name = "hybrid_ssd_retention_hgrn2_stream_step"


def rms_norm(x, w=None):
    xf = x.astype(jnp.float32)
    y = xf * lax.rsqrt(jnp.mean(jnp.square(xf), axis=-1, keepdims=True) + EPS)
    if w is not None:
        y = y * w.astype(jnp.float32)
    return y.astype(x.dtype)


def causal_dwconv(x_hist, w, b):
    c = x_hist.shape[-1]
    y = lax.conv_general_dilated(x_hist, w.astype(x_hist.dtype)[:, None, :], window_strides=(1,),
                                 padding='VALID', dimension_numbers=('NWC', 'WIO', 'NWC'),
                                 feature_group_count=c)
    return y + b.astype(x_hist.dtype)


def rotate_every_two(x):
    x1 = x[..., 0::2]
    x2 = x[..., 1::2]
    return jnp.stack((-x2, x1), axis=-1).reshape(x.shape)


def retention_rotation(pos):
    inv = 1.0 / (ROPE_BASE ** jnp.linspace(0.0, 1.0, RET_DK // 2, dtype=jnp.float32))
    ang = pos.astype(jnp.float32)[:, None] * jnp.repeat(inv, 2)[None, :]
    return jnp.sin(ang), jnp.cos(ang)


def apply_rotation(x, sin, cos):
    return x * cos[:, None, :].astype(x.dtype) + rotate_every_two(x) * sin[:, None, :].astype(x.dtype)


def scan_chunk_states(s0, a, ds):
    def step(s, inp):
        a_c, d_c = inp
        return a_c * s + d_c, s
    s_fin, s_in = lax.scan(step, s0, (jnp.swapaxes(a, 0, 1), jnp.swapaxes(ds, 0, 1)))
    return s_fin, jnp.swapaxes(s_in, 0, 1)


def scalar_decay_chunks(q, k, v, log_a, s0):
    f32 = jnp.float32
    qf, kf, vf = q.astype(f32), k.astype(f32), v.astype(f32)
    cum = jnp.cumsum(log_a.astype(f32), axis=2)
    cum_h = jnp.moveaxis(cum, 3, 2)
    c = q.shape[2]
    causal = jnp.tril(jnp.ones((c, c), dtype=bool))
    seg = cum_h[..., :, None] - cum_h[..., None, :]
    decay = jnp.exp(jnp.where(causal, seg, -jnp.inf))
    scores = jnp.einsum('bcihk,bcjhk->bchij', qf, kf) * decay
    y = jnp.einsum('bchij,bcjhv->bcihv', scores, vf)
    last = cum_h[..., -1]
    w_end = jnp.exp(last[..., None] - cum_h)
    ds = jnp.einsum('bchj,bcjhk,bcjhv->bchkv', w_end, kf, vf)
    s_fin, s_in = scan_chunk_states(s0.astype(f32), jnp.exp(last)[..., None, None], ds)
    y = y + jnp.einsum('bcihk,bchkv->bcihv', qf * jnp.exp(cum)[..., None], s_in)
    return y.astype(q.dtype), s_fin.astype(s0.dtype)


def vector_decay_chunks(q, k, v, log_f, s0):
    f32 = jnp.float32
    qf, kf, vf = q.astype(f32), k.astype(f32), v.astype(f32)
    g = jnp.cumsum(log_f.astype(f32), axis=2)
    qg = qf * jnp.exp(g)
    kg = kf * jnp.exp(-g)
    c = q.shape[2]
    causal = jnp.tril(jnp.ones((c, c), dtype=bool))
    scores = jnp.where(causal, jnp.einsum('bcihk,bcjhk->bchij', qg, kg), 0.0)
    y = jnp.einsum('bchij,bcjhv->bcihv', scores, vf)
    last = g[:, :, -1]
    ds = jnp.einsum('bcjhk,bcjhv->bchkv', kf * jnp.exp(last[:, :, None] - g), vf)
    s_fin, s_in = scan_chunk_states(s0.astype(f32), jnp.exp(last)[..., None], ds)
    y = y + jnp.einsum('bcihk,bchkv->bcihv', qg, s_in)
    return y.astype(q.dtype), s_fin.astype(s0.dtype)


def run_segments(chunk_fn, plan, s0, *xs):
    outs = []
    s = s0
    start = 0
    for length, csize in plan:
        parts = [x[:, start:start + length].reshape(x.shape[0], length // csize, csize, *x.shape[2:])
                 for x in xs]
        y, s = chunk_fn(*parts, s)
        outs.append(y.reshape(y.shape[0], length, *y.shape[3:]))
        start += length
    return jnp.concatenate(outs, axis=1), s


def mixer_ssd_retention(h, w_in, conv_w, conv_b, dt_bias, a_log, d_skip, ssd_norm, w_out,
                        conv_hist, s_ssd, s_ret, sin, cos, plan):
    b, l, _ = h.shape
    proj = h @ w_in
    z, xbc, dt_raw, q, k, v, g = jnp.split(proj, AB_SPLITS, axis=-1)
    xbc_hist = jnp.concatenate([conv_hist.astype(xbc.dtype), xbc], axis=1)
    new_conv = xbc_hist[:, xbc_hist.shape[1] - (SSD_CONV - 1):]
    xbc = jax.nn.silu(causal_dwconv(xbc_hist, conv_w, conv_b))
    xs, bm, cm = jnp.split(xbc, [SSD_INNER, SSD_INNER + SSD_GROUPS * SSD_STATE], axis=-1)
    xs = xs.reshape(b, l, SSD_HEADS, SSD_HEADDIM)
    rep = SSD_HEADS // SSD_GROUPS
    bh = jnp.repeat(bm.reshape(b, l, SSD_GROUPS, SSD_STATE), rep, axis=2)
    ch = jnp.repeat(cm.reshape(b, l, SSD_GROUPS, SSD_STATE), rep, axis=2)
    dt = jax.nn.softplus(dt_raw.astype(jnp.float32) + dt_bias.astype(jnp.float32))
    a = -jnp.exp(a_log.astype(jnp.float32))
    y, s_ssd_new = run_segments(scalar_decay_chunks, plan, s_ssd, ch, bh,
                                xs * dt[..., None].astype(xs.dtype), dt * a)
    y = y + xs * d_skip[:, None].astype(xs.dtype)
    y = (y.reshape(b, l, SSD_INNER) * jax.nn.silu(z)).reshape(b, l, SSD_GROUPS, SSD_INNER // SSD_GROUPS)
    y_ssd = rms_norm(y).reshape(b, l, SSD_INNER) * ssd_norm.astype(y.dtype)
    q = apply_rotation(q.reshape(b, l, RET_HEADS, RET_DK), sin, cos)
    k = apply_rotation(k.reshape(b, l, RET_HEADS, RET_DK), sin, cos) * (RET_DK ** -0.5)
    v = v.reshape(b, l, RET_HEADS, RET_DV)
    log_gamma = jnp.log1p(-(2.0 ** (-5.0 - jnp.arange(RET_HEADS, dtype=jnp.float32))))
    o, s_ret_new = run_segments(scalar_decay_chunks, plan, s_ret, q, k, v,
                                jnp.broadcast_to(log_gamma, (b, l, RET_HEADS)))
    y_ret = rms_norm(o).reshape(b, l, RET_V) * jax.nn.silu(g)
    out = jnp.concatenate([y_ssd, y_ret], axis=-1) @ w_out
    return out, new_conv, s_ssd_new, s_ret_new


def mixer_hgrn2(h, w_in, lb, norm_w, w_out, s_hg, plan):
    b, l, _ = h.shape
    q, f, i, g = jnp.split(h @ w_in, 4, axis=-1)
    q = jax.nn.silu(q).reshape(b, l, HG_HEADS, HG_DK)
    forget = lb + (1.0 - lb) * jax.nn.sigmoid(f.astype(jnp.float32))
    k = (1.0 - forget).reshape(b, l, HG_HEADS, HG_DK)
    log_f = jnp.log(forget).reshape(b, l, HG_HEADS, HG_DK)
    o, s_new = run_segments(vector_decay_chunks, plan, s_hg, q, k,
                            i.reshape(b, l, HG_HEADS, HG_DV), log_f)
    o = (rms_norm(o) * norm_w.astype(o.dtype)).reshape(b, l, D_MODEL) * jax.nn.silu(g)
    return o @ w_out, s_new


def swiglu(x, wg, wu, wd):
    return (jax.nn.silu(x @ wg) * (x @ wu)) @ wd


def moe_swiglu(x, w_router, wg, wu, wd):
    logits = (x @ w_router).astype(jnp.float32)
    top_v, top_i = lax.top_k(logits, TOP_K)
    gates = jax.nn.softmax(top_v, axis=-1)
    combine = jnp.sum(jax.nn.one_hot(top_i, N_EXPERTS, dtype=jnp.float32) * gates[..., None],
                      axis=-2).astype(x.dtype)
    out = jnp.zeros_like(x)
    for e in range(N_EXPERTS):
        out = out + combine[..., e:e + 1] * swiglu(x, wg[e], wu[e], wd[e])
    return out


def run_trunk(x, pos, plan, conv_st, ssd_st, ret_st, hg_st, p):
    sin, cos = retention_rotation(pos)
    lb_soft = jax.nn.softmax(p['hgrn_lb'].astype(jnp.float32), axis=0)
    lb_all = jnp.cumsum(lb_soft, axis=0) - lb_soft[0]
    conv_new, ssd_new, ret_new, hg_new = [], [], [], []
    for layer in range(DEPTH):
        j = layer // 2
        h = rms_norm(x, p['norm_mix'][layer])
        if layer % 2 == 0:
            out, c_n, s_n, r_n = mixer_ssd_retention(
                h, p['w_in_ab'][j], p['conv_w'][j], p['conv_b'][j], p['dt_bias'][j], p['a_log'][j],
                p['d_skip'][j], p['ssd_norm'][j], p['w_out_ab'][j],
                conv_st[j], ssd_st[j], ret_st[j], sin, cos, plan)
            conv_new.append(c_n)
            ssd_new.append(s_n)
            ret_new.append(r_n)
            x = x + out
            x = x + swiglu(rms_norm(x, p['norm_ffn'][layer]), p['w_ffn_gate'][j], p['w_ffn_up'][j],
                           p['w_ffn_down'][j])
        else:
            out, h_n = mixer_hgrn2(h, p['w_in_c'][j], lb_all[layer], p['hgrn_norm'][j], p['w_out_c'][j],
                                   hg_st[j], plan)
            hg_new.append(h_n)
            x = x + out
            x = x + moe_swiglu(rms_norm(x, p['norm_ffn'][layer]), p['w_router'][j], p['w_exp_gate'][j],
                               p['w_exp_up'][j], p['w_exp_down'][j])
    y = rms_norm(x, p['norm_final'])
    return y, jnp.stack(ssd_new), jnp.stack(conv_new), jnp.stack(ret_new), jnp.stack(hg_new)


def setup_inputs(seed: int = 0) -> dict:
    key = jax.random.key(seed)
    ks = jax.random.split(key, 32)
    f32 = jnp.float32

    def nrm(k, shape, s):
        return jax.random.normal(k, shape, f32) * s

    dt0 = jnp.exp(jax.random.uniform(ks[13], (N_EVEN, SSD_HEADS), f32, math.log(1e-3), math.log(1e-1)))
    return {
        'x_prompt': nrm(ks[0], (BATCH, SEQ, D_MODEL), 1.0),
        'x_sample': nrm(ks[1], (DEC_BATCH, DEC_SEQ, D_MODEL), 1.0),
        'state_ssd': nrm(ks[2], (N_EVEN, DEC_BATCH, SSD_HEADS, SSD_STATE, SSD_HEADDIM), 0.5),
        'state_ssd_conv': nrm(ks[3], (N_EVEN, DEC_BATCH, SSD_CONV - 1, SSD_CONV_DIM), 1.0),
        'state_ret': nrm(ks[4], (N_EVEN, DEC_BATCH, RET_HEADS, RET_DK, RET_DV), 1.0),
        'state_hgrn': nrm(ks[5], (N_ODD, DEC_BATCH, HG_HEADS, HG_DK, HG_DV), 0.5),
        'meta_tokens': nrm(ks[6], (N_META, D_MODEL), 1.0),
        'norm_mix': 1.0 + nrm(ks[7], (DEPTH, D_MODEL), 0.02),
        'norm_ffn': 1.0 + nrm(ks[8], (DEPTH, D_MODEL), 0.02),
        'norm_final': 1.0 + nrm(ks[9], (D_MODEL,), 0.02),
        'w_in_ab': nrm(ks[10], (N_EVEN, D_MODEL, AB_IN), D_MODEL ** -0.5),
        'conv_w': nrm(ks[11], (N_EVEN, SSD_CONV, SSD_CONV_DIM), SSD_CONV ** -0.5),
        'conv_b': nrm(ks[12], (N_EVEN, SSD_CONV_DIM), 0.02),
        'dt_bias': dt0 + jnp.log(-jnp.expm1(-dt0)),
        'a_log': jnp.log(jax.random.uniform(ks[14], (N_EVEN, SSD_HEADS), f32, 1.0, 16.0)),
        'd_skip': 1.0 + nrm(ks[15], (N_EVEN, SSD_HEADS), 0.1),
        'ssd_norm': 1.0 + nrm(ks[16], (N_EVEN, SSD_INNER), 0.02),
        'w_out_ab': nrm(ks[17], (N_EVEN, AB_OUT, D_MODEL), AB_OUT ** -0.5),
        'w_ffn_gate': nrm(ks[18], (N_EVEN, D_MODEL, D_FF), D_MODEL ** -0.5),
        'w_ffn_up': nrm(ks[19], (N_EVEN, D_MODEL, D_FF), D_MODEL ** -0.5),
        'w_ffn_down': nrm(ks[20], (N_EVEN, D_FF, D_MODEL), D_FF ** -0.5),
        'w_in_c': nrm(ks[21], (N_ODD, D_MODEL, HG_IN), D_MODEL ** -0.5),
        'hgrn_lb': nrm(ks[22], (DEPTH, HG_HEADS * HG_DK), 0.5),
        'hgrn_norm': 1.0 + nrm(ks[23], (N_ODD, HG_DV), 0.02),
        'w_out_c': nrm(ks[24], (N_ODD, D_MODEL, D_MODEL), D_MODEL ** -0.5),
        'w_router': nrm(ks[25], (N_ODD, D_MODEL, N_EXPERTS), D_MODEL ** -0.5),
        'w_exp_gate': nrm(ks[26], (N_ODD, N_EXPERTS, D_MODEL, D_FF_EXPERT), D_MODEL ** -0.5),
        'w_exp_up': nrm(ks[27], (N_ODD, N_EXPERTS, D_MODEL, D_FF_EXPERT), D_MODEL ** -0.5),
        'w_exp_down': nrm(ks[28], (N_ODD, N_EXPERTS, D_FF_EXPERT, D_MODEL), D_FF_EXPERT ** -0.5),
    }


def reference(x_prompt, x_sample, state_ssd, state_ssd_conv, state_ret, state_hgrn,
              meta_tokens, norm_mix, norm_ffn, norm_final, w_in_ab, conv_w, conv_b, dt_bias, a_log,
              d_skip, ssd_norm, w_out_ab, w_ffn_gate, w_ffn_up, w_ffn_down, w_in_c, hgrn_lb,
              hgrn_norm, w_out_c, w_router, w_exp_gate, w_exp_up, w_exp_down):
    p = {'meta_tokens': meta_tokens, 'norm_mix': norm_mix, 'norm_ffn': norm_ffn,
         'norm_final': norm_final, 'w_in_ab': w_in_ab, 'conv_w': conv_w, 'conv_b': conv_b,
         'dt_bias': dt_bias, 'a_log': a_log, 'd_skip': d_skip, 'ssd_norm': ssd_norm,
         'w_out_ab': w_out_ab, 'w_ffn_gate': w_ffn_gate, 'w_ffn_up': w_ffn_up,
         'w_ffn_down': w_ffn_down, 'w_in_c': w_in_c, 'hgrn_lb': hgrn_lb, 'hgrn_norm': hgrn_norm,
         'w_out_c': w_out_c, 'w_router': w_router, 'w_exp_gate': w_exp_gate,
         'w_exp_up': w_exp_up, 'w_exp_down': w_exp_down}
    bp, seq = x_prompt.shape[0], x_prompt.shape[1]
    meta = jnp.broadcast_to(meta_tokens[None].astype(x_prompt.dtype), (bp, N_META, D_MODEL))
    xp = jnp.concatenate([meta, x_prompt], axis=1)
    pos_p = jnp.arange(N_META + seq)
    plan_p = ((N_META, N_META), (seq, CHUNK))
    sdt = state_ssd.dtype
    conv0 = jnp.zeros((N_EVEN, bp, SSD_CONV - 1, SSD_CONV_DIM), state_ssd_conv.dtype)
    ssd0 = jnp.zeros((N_EVEN, bp, SSD_HEADS, SSD_STATE, SSD_HEADDIM), sdt)
    ret0 = jnp.zeros((N_EVEN, bp, RET_HEADS, RET_DK, RET_DV), state_ret.dtype)
    hg0 = jnp.zeros((N_ODD, bp, HG_HEADS, HG_DK, HG_DV), state_hgrn.dtype)
    yp, ssd_p, conv_p, ret_p, hg_p = run_trunk(xp, pos_p, plan_p, conv0, ssd0, ret0, hg0, p)
    y_prompt = yp[:, N_META:]
    dec_seq = x_sample.shape[1]
    pos_s = N_META + PAST_LEN + jnp.arange(dec_seq)
    plan_s = ((dec_seq, dec_seq),)
    y_sample, ssd_s, conv_s, ret_s, hg_s = run_trunk(x_sample, pos_s, plan_s, state_ssd_conv, state_ssd,
                                                     state_ret, state_hgrn, p)
    return (y_prompt, y_sample, ssd_p, conv_p, ret_p, hg_p, ssd_s, conv_s, ret_s, hg_s)
```

```python
import functools

import jax
import jax.numpy as jnp
from jax import lax
from jax.experimental import pallas as pl
from jax.experimental.pallas import tpu as pltpu

F32 = jnp.float32
BF16 = jnp.bfloat16

CHUNK = 64
N_META = 16
PAST_LEN = 2048
EPS = 1e-6
SSD_HEADDIM = 64
SSD_GROUPS = 2
SSD_STATE = 128
SSD_CONV = 4
RET_HEADS = 8
ROPE_BASE = 10000.0
HG_DK = 128

LANES = 128
HIST_ROWS = 8
VMEM_LIMIT = 56 * 1024 * 1024
NEG_BIG = -1e30


def _dot(a, b):
    return jnp.dot(a, b, preferred_element_type=F32)


def _dot_nt(a, b):
    return lax.dot_general(a, b, (((1,), (1,)), ((), ())), preferred_element_type=F32)


def _dot_tn(a, b):
    return lax.dot_general(a, b, (((0,), (0,)), ((), ())), preferred_element_type=F32)


def _split3(x):
    hi = x.astype(BF16)
    r = x - hi.astype(F32)
    mid = r.astype(BF16)
    lo = (r - mid.astype(F32)).astype(BF16)
    return hi, mid, lo


def _exact_ldot(x, m01):
    hi, mid, lo = _split3(x)
    return _dot(hi, m01) + _dot(mid, m01) + _dot(lo, m01)


def _exact_rdot(m01, x):
    hi, mid, lo = _split3(x)
    return _dot(m01, hi) + _dot(m01, mid) + _dot(m01, lo)


def _causal(c):
    row = lax.broadcasted_iota(jnp.int32, (c, c), 0)
    col = lax.broadcasted_iota(jnp.int32, (c, c), 1)
    return row >= col


def _cumsum_time(x):
    c = x.shape[0]
    return _exact_rdot(jnp.where(_causal(c), 1.0, 0.0).astype(BF16), x)


def _sigmoid(x):
    return 1.0 / (1.0 + jnp.exp(-x))


def _silu(x):
    return x * _sigmoid(x)


def _rms(x):
    return x * lax.rsqrt(jnp.mean(x * x, axis=-1, keepdims=True) + EPS)


def _normmm_kernel(x_ref, nw_ref, w_ref, o_ref, xn_scr):
    @pl.when(pl.program_id(1) == 0)
    def _():
        xn_scr[...] = (_rms(x_ref[...]) * nw_ref[...]).astype(BF16)

    o_ref[...] = _dot(xn_scr[...], w_ref[...])


def _normmm(x, nw, w, tm, tn):
    t, k = x.shape
    n = w.shape[1]
    return pl.pallas_call(
        _normmm_kernel,
        grid=(t // tm, n // tn),
        in_specs=[pl.BlockSpec((tm, k), lambda i, j: (i, 0)),
                  pl.BlockSpec((1, k), lambda i, j: (0, 0)),
                  pl.BlockSpec((k, tn), lambda i, j: (0, j))],
        out_specs=pl.BlockSpec((tm, tn), lambda i, j: (i, j)),
        out_shape=jax.ShapeDtypeStruct((t, n), F32),
        scratch_shapes=[pltpu.VMEM((tm, k), BF16)],
        compiler_params=pltpu.CompilerParams(dimension_semantics=("parallel", "arbitrary"),
                                             vmem_limit_bytes=VMEM_LIMIT),
    )(x, nw.reshape(1, k), w)


def _outproj_kernel(*refs, n_in):
    x_ref = refs[0]
    ys = refs[1:1 + n_in]
    ws = refs[1 + n_in:1 + 2 * n_in]
    o_ref = refs[1 + 2 * n_in]
    acc = x_ref[...]
    for y, w in zip(ys, ws):
        acc = acc + _dot(y[...].astype(BF16), w[...])
    o_ref[...] = acc


def _outproj(x, ys, ws, tm):
    t, d = x.shape
    n_in = len(ys)
    in_specs = [pl.BlockSpec((tm, d), lambda i: (i, 0))]
    in_specs += [pl.BlockSpec((tm, y.shape[1]), lambda i: (i, 0)) for y in ys]
    in_specs += [pl.BlockSpec(w.shape, lambda i: (0, 0)) for w in ws]
    return pl.pallas_call(
        functools.partial(_outproj_kernel, n_in=n_in),
        grid=(t // tm,),
        in_specs=in_specs,
        out_specs=pl.BlockSpec((tm, d), lambda i: (i, 0)),
        out_shape=jax.ShapeDtypeStruct((t, d), F32),
        compiler_params=pltpu.CompilerParams(dimension_semantics=("parallel",),
                                             vmem_limit_bytes=VMEM_LIMIT),
    )(x, *ys, *ws)


def _ffn_kernel(x_ref, nw_ref, wg_ref, wu_ref, wd_ref, o_ref, xn_scr, acc_scr):
    f = pl.program_id(1)

    @pl.when(f == 0)
    def _():
        x = x_ref[...]
        xn_scr[...] = (_rms(x) * nw_ref[...]).astype(BF16)
        acc_scr[...] = x

    xn = xn_scr[...]
    h = _silu(_dot(xn, wg_ref[...])) * _dot(xn, wu_ref[...])
    acc_scr[...] += _dot(h.astype(BF16), wd_ref[...])

    @pl.when(f == pl.num_programs(1) - 1)
    def _():
        o_ref[...] = acc_scr[...]


def _ffn(x, nw, wg, wu, wd, tm, tf):
    t, d = x.shape
    ff = wg.shape[1]
    return pl.pallas_call(
        _ffn_kernel,
        grid=(t // tm, ff // tf),
        in_specs=[pl.BlockSpec((tm, d), lambda i, f: (i, 0)),
                  pl.BlockSpec((1, d), lambda i, f: (0, 0)),
                  pl.BlockSpec((d, tf), lambda i, f: (0, f)),
                  pl.BlockSpec((d, tf), lambda i, f: (0, f)),
                  pl.BlockSpec((tf, d), lambda i, f: (f, 0))],
        out_specs=pl.BlockSpec((tm, d), lambda i, f: (i, 0)),
        out_shape=jax.ShapeDtypeStruct((t, d), F32),
        scratch_shapes=[pltpu.VMEM((tm, d), BF16), pltpu.VMEM((tm, d), F32)],
        compiler_params=pltpu.CompilerParams(dimension_semantics=("parallel", "arbitrary"),
                                             vmem_limit_bytes=VMEM_LIMIT),
    )(x, nw.reshape(1, d), wg, wu, wd)


def _router_kernel(x_ref, nw_ref, whi_ref, wlo_ref, comb_ref, *, n_experts):
    xn = _rms(x_ref[...]) * nw_ref[...]
    hi = xn.astype(BF16)
    lo = (xn - hi.astype(F32)).astype(BF16)
    logits = _dot(hi, whi_ref[...]) + _dot(lo, whi_ref[...]) + _dot(hi, wlo_ref[...])
    lane = lax.broadcasted_iota(jnp.int32, logits.shape, 1)
    logits = jnp.where(lane < n_experts, logits, NEG_BIG)
    m1 = jnp.max(logits, axis=-1, keepdims=True)
    i1 = jnp.min(jnp.where(logits == m1, lane, LANES), axis=-1, keepdims=True)
    rest = jnp.where(lane == i1, NEG_BIG, logits)
    m2 = jnp.max(rest, axis=-1, keepdims=True)
    i2 = jnp.min(jnp.where(rest == m2, lane, LANES), axis=-1, keepdims=True)
    e2 = jnp.exp(m2 - m1)
    g1 = 1.0 / (1.0 + e2)
    g2 = e2 / (1.0 + e2)
    comb_ref[...] = jnp.where(lane == i1, g1, 0.0) + jnp.where(lane == i2, g2, 0.0)


def _router(x, nw, w_router, tm):
    t, d = x.shape
    e = w_router.shape[1]
    wpad = jnp.zeros((d, LANES), F32).at[:, :e].set(w_router)
    whi = wpad.astype(BF16)
    wlo = (wpad - whi.astype(F32)).astype(BF16)
    return pl.pallas_call(
        functools.partial(_router_kernel, n_experts=e),
        grid=(t // tm,),
        in_specs=[pl.BlockSpec((tm, d), lambda i: (i, 0)),
                  pl.BlockSpec((1, d), lambda i: (0, 0)),
                  pl.BlockSpec((d, LANES), lambda i: (0, 0)),
                  pl.BlockSpec((d, LANES), lambda i: (0, 0))],
        out_specs=pl.BlockSpec((tm, LANES), lambda i: (i, 0)),
        out_shape=jax.ShapeDtypeStruct((t, LANES), F32),
        compiler_params=pltpu.CompilerParams(dimension_semantics=("parallel",),
                                             vmem_limit_bytes=VMEM_LIMIT),
    )(x, nw.reshape(1, d), whi, wlo)


def _moe_kernel(x_ref, nw_ref, comb_ref, wg_ref, wu_ref, wd_ref, o_ref, xn_scr, acc_scr):
    e = pl.program_id(1)
    f = pl.program_id(2)

    @pl.when((e == 0) & (f == 0))
    def _():
        x = x_ref[...]
        xn_scr[...] = (_rms(x) * nw_ref[...]).astype(BF16)
        acc_scr[...] = x

    comb = comb_ref[...]
    lane = lax.broadcasted_iota(jnp.int32, comb.shape, 1)
    ce = jnp.sum(jnp.where(lane == e, comb, 0.0), axis=-1, keepdims=True)
    xn = xn_scr[...]
    h = _silu(_dot(xn, wg_ref[...])) * _dot(xn, wu_ref[...])
    acc_scr[...] += _dot((h * ce).astype(BF16), wd_ref[...])

    @pl.when((e == pl.num_programs(1) - 1) & (f == pl.num_programs(2) - 1))
    def _():
        o_ref[...] = acc_scr[...]


def _moe(x, nw, comb, wg, wu, wd, tm, tf):
    t, d = x.shape
    ne, _, ff = wg.shape
    return pl.pallas_call(
        _moe_kernel,
        grid=(t // tm, ne, ff // tf),
        in_specs=[pl.BlockSpec((tm, d), lambda i, e, f: (i, 0)),
                  pl.BlockSpec((1, d), lambda i, e, f: (0, 0)),
                  pl.BlockSpec((tm, LANES), lambda i, e, f: (i, 0)),
                  pl.BlockSpec((None, d, tf), lambda i, e, f: (e, 0, f)),
                  pl.BlockSpec((None, d, tf), lambda i, e, f: (e, 0, f)),
                  pl.BlockSpec((None, tf, d), lambda i, e, f: (e, f, 0))],
        out_specs=pl.BlockSpec((tm, d), lambda i, e, f: (i, 0)),
        out_shape=jax.ShapeDtypeStruct((t, d), F32),
        scratch_shapes=[pltpu.VMEM((tm, d), BF16), pltpu.VMEM((tm, d), F32)],
        compiler_params=pltpu.CompilerParams(dimension_semantics=("parallel", "arbitrary", "arbitrary"),
                                             vmem_limit_bytes=VMEM_LIMIT),
    )(x, nw.reshape(1, d), comb, wg, wu, wd)


def _finalnorm_kernel(x_ref, nw_ref, o_ref):
    o_ref[...] = _rms(x_ref[...]) * nw_ref[...]


def _finalnorm(x, nw, tm):
    t, d = x.shape
    return pl.pallas_call(
        _finalnorm_kernel,
        grid=(t // tm,),
        in_specs=[pl.BlockSpec((tm, d), lambda i: (i, 0)), pl.BlockSpec((1, d), lambda i: (0, 0))],
        out_specs=pl.BlockSpec((tm, d), lambda i: (i, 0)),
        out_shape=jax.ShapeDtypeStruct((t, d), F32),
        compiler_params=pltpu.CompilerParams(dimension_semantics=("parallel",)),
    )(x, nw.reshape(1, d))


class _Group:
    def __init__(self, row0, nb, nc, c, nc_grid=None):
        self.row0, self.nb, self.nc, self.c = row0, nb, nc, c
        self.nc_grid = nc if nc_grid is None else nc_grid

    def rows(self, width, col_block):
        base, nc, c = self.row0 // self.c, self.nc, self.c
        return pl.BlockSpec((c, width), lambda b, i: (base + b * nc + i, col_block))

    def chunk_rows(self, width):
        nc, c = self.nc, self.c
        return pl.BlockSpec((c, width), lambda b, i: (jnp.minimum(i, nc - 1), 0))


def _full(shape):
    nd = len(shape)
    return pl.BlockSpec(tuple(shape), lambda b, i: (0,) * nd)


def _per_stream(shape):
    nd = len(shape)
    return pl.BlockSpec((None,) + tuple(shape[1:]), lambda b, i: (b,) + (0,) * (nd - 1))


def _scan_call(kernel, group, row_inputs, const_inputs, stream_inputs, y_prev, t_pad, width, extra_out_shapes,
               extra_out_specs, scratch_shapes):
    arrays, specs = [], []
    for arr, w, cb in row_inputs:
        arrays.append(arr)
        specs.append(group.rows(w, cb))
    for item in const_inputs:
        if isinstance(item, tuple):
            arr, spec = item
        else:
            arr, spec = item, _full(item.shape)
        arrays.append(arr)
        specs.append(spec)
    for arr in stream_inputs:
        arrays.append(arr)
        specs.append(_per_stream(arr.shape))
    aliases = {}
    if y_prev is not None:
        aliases = {len(arrays): 0}
        arrays.append(y_prev)
        specs.append(pl.BlockSpec(memory_space=pl.ANY))
    out_shape = [jax.ShapeDtypeStruct((t_pad, width), F32)] + list(extra_out_shapes)
    out_specs = [group.rows(width, 0)] + list(extra_out_specs)
    return pl.pallas_call(
        functools.partial(kernel, c=group.c, nc=group.nc, has_prev=y_prev is not None),
        grid=(group.nb, group.nc_grid),
        in_specs=specs,
        out_specs=out_specs,
        out_shape=out_shape,
        scratch_shapes=scratch_shapes,
        input_output_aliases=aliases,
        compiler_params=pltpu.CompilerParams(dimension_semantics=("parallel", "arbitrary"),
                                             vmem_limit_bytes=VMEM_LIMIT),
    )(*arrays)


def _ssd_kernel(*refs, c, nc, has_prev):
    (z_ref, x_ref, bc_ref, dt_ref, convw_ref, convb_ref, dtb_ref, alog_ref, dskip_ref, nw_ref, expand_ref,
     hist0_ref, s0_ref) = refs[:13]
    y_ref, s_ref, hout_ref, hist_scr = refs[13 + int(has_prev):]
    i = pl.program_id(1)
    d_inner = x_ref.shape[1]
    n_state = SSD_STATE
    n_pairs = d_inner // LANES
    pairs_per_group = n_pairs // SSD_GROUPS
    n_hist = SSD_CONV - 1

    @pl.when(i == 0)
    def _():
        s_ref[...] = s0_ref[...]
        hist_scr[HIST_ROWS - n_hist:HIST_ROWS, :] = hist0_ref[...]

    @pl.when(i >= nc)
    def _():
        y_ref[...] = jnp.zeros_like(y_ref)

    @pl.when(i < nc)
    def _():
        hist_scr[HIST_ROWS:HIST_ROWS + c, 0:d_inner] = x_ref[...]
        hist_scr[HIST_ROWS:HIST_ROWS + c, d_inner:] = bc_ref[...]
        conv = convb_ref[...]
        for j in range(SSD_CONV):
            conv = conv + hist_scr[HIST_ROWS - n_hist + j:HIST_ROWS - n_hist + j + c, :] * convw_ref[j:j + 1, :]
        tail = hist_scr[HIST_ROWS + c - n_hist:HIST_ROWS + c, :]
        hist_scr[HIST_ROWS - n_hist:HIST_ROWS, :] = tail
        hout_ref[...] = tail
        xbc = _silu(conv)
        xs = xbc[:, 0:d_inner]
        bm = xbc[:, d_inner:d_inner + SSD_GROUPS * n_state]
        cm = xbc[:, d_inner + SSD_GROUPS * n_state:]

        x_dt = dt_ref[...] + dtb_ref[...]
        dt = jnp.maximum(x_dt, 0.0) + jnp.log1p(jnp.exp(-jnp.abs(x_dt)))
        log_a = dt * (-jnp.exp(alog_ref[...]))
        cum = _cumsum_time(log_a)
        expand = expand_ref[...]
        dt_e = _exact_ldot(dt, expand)
        cum_e = _exact_ldot(cum, expand)
        last_e = cum_e[c - 1:c, :]
        xdt = xs * dt_e
        wx = jnp.exp(last_e - cum_e) * xdt
        ecum = jnp.exp(cum_e)
        sdecay = jnp.exp(last_e)

        row = lax.broadcasted_iota(jnp.int32, (c, 2 * c), 0)
        col = lax.broadcasted_iota(jnp.int32, (c, 2 * c), 1)
        second = col >= c
        tcol = jnp.where(second, col - c, col)
        causal2 = tcol <= row
        diag2 = tcol == row
        r2 = lax.broadcasted_iota(jnp.int32, (2 * c, LANES), 0)
        l2 = lax.broadcasted_iota(jnp.int32, (2 * c, LANES), 1)
        half2 = (r2 >= c) == (l2 >= SSD_HEADDIM)

        ys = []
        for g in range(SSD_GROUPS):
            cg = cm[:, g * n_state:(g + 1) * n_state].astype(BF16)
            bg = bm[:, g * n_state:(g + 1) * n_state].astype(BF16)
            scores2 = _dot_nt(cg, jnp.concatenate([bg, bg], axis=0))
            for p in range(g * pairs_per_group, (g + 1) * pairs_per_group):
                sl = slice(p * LANES, (p + 1) * LANES)
                cum_p = cum_e[:, sl]
                cum_col = jnp.where(second, cum_p[:, SSD_HEADDIM:SSD_HEADDIM + 1], cum_p[:, 0:1])
                cum_row = jnp.sum(jnp.where(diag2, cum_col, 0.0), axis=0, keepdims=True)
                seg = cum_col - cum_row
                m2 = (scores2 * jnp.exp(jnp.where(causal2, seg, NEG_BIG))).astype(BF16)
                xp = xdt[:, sl]
                x2 = jnp.where(half2, jnp.concatenate([xp, xp], axis=0), 0.0).astype(BF16)
                y = _dot(m2, x2)
                y = y + _dot(cg, s_ref[p].astype(BF16)) * ecum[:, sl]
                s_ref[p] = s_ref[p] * sdecay[:, sl] + _dot_tn(bg, wx[:, sl].astype(BF16))
                ys.append(y)
        y = jnp.concatenate(ys, axis=1)
        y = y + xs * dskip_ref[...]
        y = y * _silu(z_ref[...])
        gw = d_inner // SSD_GROUPS
        y = jnp.concatenate([_rms(y[:, g * gw:(g + 1) * gw]) for g in range(SSD_GROUPS)], axis=1)
        y_ref[...] = y * nw_ref[...]


def _ret_kernel(*refs, c, nc, has_prev):
    (q_ref, k_ref, v_ref, g_ref, sin_ref, cos_ref, dmat_ref, ecum_ref, wend_ref, sdec_ref, s0_ref) = refs[:11]
    y_ref, s_ref = refs[11 + int(has_prev):]
    i = pl.program_id(1)
    n_heads = q_ref.shape[1] // LANES

    @pl.when(i == 0)
    def _():
        s_ref[...] = s0_ref[...]

    @pl.when(i >= nc)
    def _():
        y_ref[...] = jnp.zeros_like(y_ref)

    @pl.when(i < nc)
    def _():
        sin = sin_ref[...]
        cos = cos_ref[...]
        even = (lax.broadcasted_iota(jnp.int32, (c, LANES), 1) % 2) == 0

        def rotate(x):
            nxt = pltpu.roll(x, LANES - 1, 1)
            prv = pltpu.roll(x, 1, 1)
            return x * cos + jnp.where(even, -nxt, prv) * sin

        scale = LANES ** -0.5
        ys = []
        for h in range(n_heads):
            sl = slice(h * LANES, (h + 1) * LANES)
            qh = rotate(q_ref[:, sl])
            kh = rotate(k_ref[:, sl]) * scale
            vh = v_ref[:, sl].astype(BF16)
            scores = _dot_nt(qh.astype(BF16), kh.astype(BF16)) * dmat_ref[h]
            y = _dot(scores.astype(BF16), vh) + _dot((qh * ecum_ref[:, sl]).astype(BF16), s_ref[h].astype(BF16))
            s_ref[h] = s_ref[h] * sdec_ref[:, sl] + _dot_tn((kh * wend_ref[:, sl]).astype(BF16), vh)
            ys.append(_rms(y) * _silu(g_ref[:, sl]))
        y_ref[...] = jnp.concatenate(ys, axis=1)


def _hgrn_kernel(*refs, c, nc, has_prev):
    (q_ref, f_ref, v_ref, g_ref, lb_ref, nw_ref, s0_ref) = refs[:7]
    y_ref, s_ref = refs[7 + int(has_prev):]
    i = pl.program_id(1)
    n_heads = q_ref.shape[1] // LANES

    @pl.when(i == 0)
    def _():
        s_ref[...] = s0_ref[...]

    @pl.when(i >= nc)
    def _():
        y_ref[...] = jnp.zeros_like(y_ref)

    @pl.when(i < nc)
    def _():
        lb = lb_ref[...]
        forget = lb + (1.0 - lb) * _sigmoid(f_ref[...])
        kk = 1.0 - forget
        gc = _cumsum_time(jnp.log(forget))
        last = gc[c - 1:c, :]
        qg = (_silu(q_ref[...]) * jnp.exp(gc)).astype(BF16)
        kg = (kk * jnp.exp(-gc)).astype(BF16)
        kw = (kk * jnp.exp(last - gc)).astype(BF16)
        sdec = jnp.exp(last)
        causal = _causal(c)
        ys = []
        for h in range(n_heads):
            sl = slice(h * LANES, (h + 1) * LANES)
            vh = v_ref[:, sl].astype(BF16)
            scores = jnp.where(causal, _dot_nt(qg[:, sl], kg[:, sl]), 0.0)
            y = _dot(scores.astype(BF16), vh) + _dot_nt(qg[:, sl], s_ref[h].astype(BF16))
            s_ref[h] = s_ref[h] * sdec[:, sl] + _dot_tn(vh, kw[:, sl])
            ys.append(_rms(y) * nw_ref[...] * _silu(g_ref[:, sl]))
        y_ref[...] = jnp.concatenate(ys, axis=1)


def _rotation_tables(pos, dk):
    inv = 1.0 / (ROPE_BASE ** jnp.linspace(0.0, 1.0, dk // 2, dtype=F32))
    ang = pos.astype(F32)[:, None] * jnp.repeat(inv, 2)[None, :]
    return jnp.sin(ang), jnp.cos(ang)


def _retention_decay(c, n_heads, dv):
    log_gamma = jnp.log1p(-(2.0 ** (-5.0 - jnp.arange(n_heads, dtype=F32))))
    cum = jnp.cumsum(jnp.broadcast_to(log_gamma, (c, n_heads)), axis=0)
    cum_h = cum.T
    causal = jnp.tril(jnp.ones((c, c), dtype=bool))
    dmat = jnp.exp(jnp.where(causal, cum_h[:, :, None] - cum_h[:, None, :], -jnp.inf))
    ecum = jnp.repeat(jnp.exp(cum), dv, axis=1)
    wend = jnp.repeat(jnp.exp(cum[-1][None, :] - cum), dv, axis=1)
    sdec = jnp.repeat(jnp.exp(cum[-1])[None, :], dv, axis=1)
    return dmat, ecum, wend, sdec


def _pair_heads(s):
    nb, h, n, p = s.shape
    return s.reshape(nb, h // 2, 2, n, p).transpose(0, 1, 3, 2, 4).reshape(nb, h // 2, n, 2 * p)


def _unpair_heads(s, p):
    nb, hp, n, _ = s.shape
    return s.reshape(nb, hp, n, 2, p).transpose(0, 1, 3, 2, 4).reshape(nb, 2 * hp, n, p)


def kernel(x_prompt, x_sample, state_ssd, state_ssd_conv, state_ret, state_hgrn, meta_tokens, norm_mix, norm_ffn,
           norm_final, w_in_ab, conv_w, conv_b, dt_bias, a_log, d_skip, ssd_norm, w_out_ab, w_ffn_gate, w_ffn_up,
           w_ffn_down, w_in_c, hgrn_lb, hgrn_norm, w_out_c, w_router, w_exp_gate, w_exp_up, w_exp_down):
    bp, seq, d = x_prompt.shape
    bs, dec_seq, _ = x_sample.shape
    n_meta = meta_tokens.shape[0]
    depth = norm_mix.shape[0]
    assert depth == 2 and seq % CHUNK == 0 and n_meta == N_META and d % LANES == 0
    n_ssd_heads = d_skip.shape[1]
    d_inner = n_ssd_heads * SSD_HEADDIM
    bc_w = 2 * SSD_GROUPS * SSD_STATE
    conv_dim = d_inner + bc_w
    ret_w = RET_HEADS * LANES
    assert d_inner == d and ret_w == d and conv_dim == conv_w.shape[2]

    tp, ts = bp * seq, bs * dec_seq
    t_real = tp + ts + n_meta
    tm = 512 if t_real >= 4096 else 128
    t_pad = -(-t_real // tm) * tm
    g_prompt = _Group(0, bp, seq // CHUNK, CHUNK)
    g_sample = _Group(tp, bs, 1, dec_seq)
    g_meta = _Group(tp + ts, 1, 1, n_meta, nc_grid=(t_pad - tp - ts) // n_meta)
    assert tp % dec_seq == 0 and (tp + ts) % n_meta == 0 and (t_pad - tp - ts) % n_meta == 0

    x0 = jnp.concatenate([x_prompt.reshape(tp, d), x_sample.reshape(ts, d), meta_tokens,
                          jnp.zeros((t_pad - t_real, d), F32)], axis=0)

    w_in = w_in_ab[0]
    o_z, o_xbc, o_dt, o_q = 0, d_inner, d_inner + conv_dim, d_inner + conv_dim + n_ssd_heads
    n_cols = 6 * d + bc_w + LANES
    tn = 768
    n_cols_pad = -(-n_cols // tn) * tn
    w_perm = jnp.concatenate([
        w_in[:, o_z:o_z + d_inner], w_in[:, o_q:o_q + 4 * ret_w], w_in[:, o_xbc:o_xbc + conv_dim],
        w_in[:, o_dt:o_dt + n_ssd_heads], jnp.zeros((d, n_cols_pad - n_cols + LANES - n_ssd_heads), F32)],
        axis=1).astype(BF16)
    proj = _normmm(x0, norm_mix[0], w_perm, tm, tn)
    cb_x, cb_bc, cb_dt = 5, (6 * d) // bc_w, (6 * d + bc_w) // LANES
    assert (6 * d) % bc_w == 0

    expand = (jnp.arange(LANES)[:, None] == (jnp.arange(d_inner) // SSD_HEADDIM)[None, :]).astype(BF16)
    pad_h = LANES - n_ssd_heads
    ssd_consts = [conv_w[0], conv_b[0].reshape(1, conv_dim), jnp.pad(dt_bias[0], (0, pad_h)).reshape(1, LANES),
                  jnp.pad(a_log[0], (0, pad_h)).reshape(1, LANES),
                  jnp.repeat(d_skip[0], SSD_HEADDIM).reshape(1, d_inner), ssd_norm[0].reshape(1, d_inner), expand]
    n_pairs = n_ssd_heads // 2

    def ssd(group, hist0, s0, y_prev):
        nb = group.nb
        return _scan_call(
            _ssd_kernel, group,
            [(proj, d_inner, 0), (proj, d_inner, cb_x), (proj, bc_w, cb_bc), (proj, LANES, cb_dt)],
            ssd_consts, [hist0, _pair_heads(s0)], y_prev, t_pad, d_inner,
            [jax.ShapeDtypeStruct((nb, n_pairs, SSD_STATE, LANES), F32),
             jax.ShapeDtypeStruct((nb, SSD_CONV - 1, conv_dim), F32)],
            [_per_stream((nb, n_pairs, SSD_STATE, LANES)), _per_stream((nb, SSD_CONV - 1, conv_dim))],
            [pltpu.VMEM((HIST_ROWS + group.c, conv_dim), F32)])

    def bcast(s, n):
        return jnp.broadcast_to(s, (n,) + s.shape[1:])

    zeros = jnp.zeros
    y_ssd, s_m, h_m = ssd(g_meta, zeros((1, SSD_CONV - 1, conv_dim), F32),
                          zeros((1, n_ssd_heads, SSD_STATE, SSD_HEADDIM), F32), None)
    y_ssd, ssd_p, conv_p = ssd(g_prompt, bcast(h_m, bp), bcast(_unpair_heads(s_m, SSD_HEADDIM), bp), y_ssd)
    y_ssd, ssd_s, conv_s = ssd(g_sample, state_ssd_conv[0], state_ssd[0], y_ssd)
    ssd_p = _unpair_heads(ssd_p, SSD_HEADDIM)
    ssd_s = _unpair_heads(ssd_s, SSD_HEADDIM)

    def ret(group, pos, s0, y_prev):
        nb = group.nb
        sin, cos = _rotation_tables(pos, LANES)
        dmat, ecum, wend, sdec = _retention_decay(group.c, RET_HEADS, LANES)
        return _scan_call(
            _ret_kernel, group,
            [(proj, ret_w, 1), (proj, ret_w, 2), (proj, ret_w, 3), (proj, ret_w, 4)],
            [(sin, group.chunk_rows(LANES)), (cos, group.chunk_rows(LANES)), dmat, ecum, wend, sdec],
            [s0], y_prev, t_pad, ret_w,
            [jax.ShapeDtypeStruct((nb, RET_HEADS, LANES, LANES), F32)],
            [_per_stream((nb, RET_HEADS, LANES, LANES))], [])

    y_ret, r_m = ret(g_meta, jnp.arange(n_meta), zeros((1, RET_HEADS, LANES, LANES), F32), None)
    y_ret, ret_p = ret(g_prompt, n_meta + jnp.arange(seq), bcast(r_m, bp), y_ret)
    y_ret, ret_s = ret(g_sample, n_meta + PAST_LEN + jnp.arange(dec_seq), state_ret[0], y_ret)

    w_out = w_out_ab[0].astype(BF16)
    x1 = _outproj(x0, [y_ssd, y_ret], [w_out[:d_inner], w_out[d_inner:]], tm)
    ff = w_ffn_gate.shape[2]
    tf = 256 if ff % 256 == 0 else ff
    x2 = _ffn(x1, norm_ffn[0], w_ffn_gate[0].astype(BF16), w_ffn_up[0].astype(BF16), w_ffn_down[0].astype(BF16),
              tm, tf)

    proj_c = _normmm(x2, norm_mix[1], w_in_c[0].astype(BF16), tm, d)
    lb_soft = jax.nn.softmax(hgrn_lb.astype(F32), axis=0)
    lb = (jnp.cumsum(lb_soft, axis=0) - lb_soft[0])[1].reshape(1, d)
    n_hg = d // HG_DK
    hg_consts = [lb, hgrn_norm[0].reshape(1, LANES)]

    def hgrn(group, s0, y_prev):
        nb = group.nb
        return _scan_call(
            _hgrn_kernel, group,
            [(proj_c, d, 0), (proj_c, d, 1), (proj_c, d, 2), (proj_c, d, 3)],
            hg_consts, [jnp.swapaxes(s0, 2, 3)], y_prev, t_pad, d,
            [jax.ShapeDtypeStruct((nb, n_hg, LANES, HG_DK), F32)],
            [_per_stream((nb, n_hg, LANES, HG_DK))], [])

    y_hg, g_m = hgrn(g_meta, zeros((1, n_hg, HG_DK, LANES), F32), None)
    y_hg, hg_p = hgrn(g_prompt, bcast(jnp.swapaxes(g_m, 2, 3), bp), y_hg)
    y_hg, hg_s = hgrn(g_sample, state_hgrn[0], y_hg)
    hg_p = jnp.swapaxes(hg_p, 2, 3)
    hg_s = jnp.swapaxes(hg_s, 2, 3)

    x3 = _outproj(x2, [y_hg], [w_out_c[0].astype(BF16)], tm)
    comb = _router(x3, norm_ffn[1], w_router[0], tm)
    ffe = w_exp_gate.shape[3]
    tfe = 512 if ffe % 512 == 0 else ffe
    x4 = _moe(x3, norm_ffn[1], comb, w_exp_gate[0].astype(BF16), w_exp_up[0].astype(BF16),
              w_exp_down[0].astype(BF16), tm, tfe)
    y = _finalnorm(x4, norm_final, tm)

    y_prompt = y[:tp].reshape(bp, seq, d)
    y_sample = y[tp:tp + ts].reshape(bs, dec_seq, d)
    return (y_prompt, y_sample, ssd_p[None], conv_p[None], ret_p[None], hg_p[None],
            ssd_s[None], conv_s[None], ret_s[None], hg_s[None])
```

```python
import functools

import jax
import jax.numpy as jnp
from jax import lax
from jax.experimental import pallas as pl
from jax.experimental.pallas import tpu as pltpu

F32 = jnp.float32
BF16 = jnp.bfloat16

CHUNK = 64
N_META = 16
PAST_LEN = 2048
EPS = 1e-6
SSD_HEADDIM = 64
SSD_GROUPS = 2
SSD_STATE = 128
SSD_CONV = 4
RET_HEADS = 8
ROPE_BASE = 10000.0
HG_DK = 128
TOP_K = 2

LANES = 128
HIST_ROWS = 8
VMEM_LIMIT = 56 * 1024 * 1024
NEG_BIG = -1e30


def _dot(a, b):
    return jnp.dot(a, b, preferred_element_type=F32)


def _dot_nt(a, b):
    return lax.dot_general(a, b, (((1,), (1,)), ((), ())), preferred_element_type=F32)


def _dot_tn(a, b):
    return lax.dot_general(a, b, (((0,), (0,)), ((), ())), preferred_element_type=F32)


def _split3(x):
    hi = x.astype(BF16)
    r = x - hi.astype(F32)
    mid = r.astype(BF16)
    lo = (r - mid.astype(F32)).astype(BF16)
    return hi, mid, lo


def _exact_ldot(x, m01):
    hi, mid, lo = _split3(x)
    return _dot(hi, m01) + _dot(mid, m01) + _dot(lo, m01)


def _exact_rdot(m01, x):
    hi, mid, lo = _split3(x)
    return _dot(m01, hi) + _dot(m01, mid) + _dot(m01, lo)


def _causal(c):
    row = lax.broadcasted_iota(jnp.int32, (c, c), 0)
    col = lax.broadcasted_iota(jnp.int32, (c, c), 1)
    return row >= col


def _cumsum_time(x):
    c = x.shape[0]
    return _exact_rdot(jnp.where(_causal(c), 1.0, 0.0).astype(BF16), x)


def _sigmoid(x):
    return 1.0 / (1.0 + jnp.exp(-x))


def _silu(x):
    return x * _sigmoid(x)


def _rms(x):
    return x * lax.rsqrt(jnp.mean(x * x, axis=-1, keepdims=True) + EPS)


def _normmm_kernel(x_ref, nw_ref, w_ref, o_ref, xn_scr):
    @pl.when(pl.program_id(1) == 0)
    def _():
        xn_scr[...] = (_rms(x_ref[...]) * nw_ref[...]).astype(BF16)

    o_ref[...] = _dot(xn_scr[...], w_ref[...])


def _normmm(x, nw, w, tm, tn):
    t, k = x.shape
    n = w.shape[1]
    return pl.pallas_call(
        _normmm_kernel,
        grid=(t // tm, n // tn),
        in_specs=[pl.BlockSpec((tm, k), lambda i, j: (i, 0)),
                  pl.BlockSpec((1, k), lambda i, j: (0, 0)),
                  pl.BlockSpec((k, tn), lambda i, j: (0, j))],
        out_specs=pl.BlockSpec((tm, tn), lambda i, j: (i, j)),
        out_shape=jax.ShapeDtypeStruct((t, n), F32),
        scratch_shapes=[pltpu.VMEM((tm, k), BF16)],
        compiler_params=pltpu.CompilerParams(dimension_semantics=("parallel", "arbitrary"),
                                             vmem_limit_bytes=VMEM_LIMIT),
    )(x, nw.reshape(1, k), w)


def _outproj_kernel(*refs, n_in):
    x_ref = refs[0]
    ys = refs[1:1 + n_in]
    ws = refs[1 + n_in:1 + 2 * n_in]
    o_ref = refs[1 + 2 * n_in]
    acc = x_ref[...]
    for y, w in zip(ys, ws):
        acc = acc + _dot(y[...].astype(BF16), w[...])
    o_ref[...] = acc


def _outproj(x, ys, ws, tm):
    t, d = x.shape
    n_in = len(ys)
    in_specs = [pl.BlockSpec((tm, d), lambda i: (i, 0))]
    in_specs += [pl.BlockSpec((tm, y.shape[1]), lambda i: (i, 0)) for y in ys]
    in_specs += [pl.BlockSpec(w.shape, lambda i: (0, 0)) for w in ws]
    return pl.pallas_call(
        functools.partial(_outproj_kernel, n_in=n_in),
        grid=(t // tm,),
        in_specs=in_specs,
        out_specs=pl.BlockSpec((tm, d), lambda i: (i, 0)),
        out_shape=jax.ShapeDtypeStruct((t, d), F32),
        compiler_params=pltpu.CompilerParams(dimension_semantics=("parallel",),
                                             vmem_limit_bytes=VMEM_LIMIT),
    )(x, *ys, *ws)


def _ffn_kernel(x_ref, nw_ref, wg_ref, wu_ref, wd_ref, o_ref, xn_scr, acc_scr):
    f = pl.program_id(1)

    @pl.when(f == 0)
    def _():
        x = x_ref[...]
        xn_scr[...] = (_rms(x) * nw_ref[...]).astype(BF16)
        acc_scr[...] = x

    xn = xn_scr[...]
    h = _silu(_dot(xn, wg_ref[...])) * _dot(xn, wu_ref[...])
    acc_scr[...] += _dot(h.astype(BF16), wd_ref[...])

    @pl.when(f == pl.num_programs(1) - 1)
    def _():
        o_ref[...] = acc_scr[...]


def _ffn(x, nw, wg, wu, wd, tm, tf):
    t, d = x.shape
    ff = wg.shape[1]
    return pl.pallas_call(
        _ffn_kernel,
        grid=(t // tm, ff // tf),
        in_specs=[pl.BlockSpec((tm, d), lambda i, f: (i, 0)),
                  pl.BlockSpec((1, d), lambda i, f: (0, 0)),
                  pl.BlockSpec((d, tf), lambda i, f: (0, f)),
                  pl.BlockSpec((d, tf), lambda i, f: (0, f)),
                  pl.BlockSpec((tf, d), lambda i, f: (f, 0))],
        out_specs=pl.BlockSpec((tm, d), lambda i, f: (i, 0)),
        out_shape=jax.ShapeDtypeStruct((t, d), F32),
        scratch_shapes=[pltpu.VMEM((tm, d), BF16), pltpu.VMEM((tm, d), F32)],
        compiler_params=pltpu.CompilerParams(dimension_semantics=("parallel", "arbitrary"),
                                             vmem_limit_bytes=VMEM_LIMIT),
    )(x, nw.reshape(1, d), wg, wu, wd)


ROUTE_E1, ROUTE_E2, ROUTE_R1, ROUTE_R2, ROUTE_G1, ROUTE_G2 = range(6)


def _router_kernel(x_ref, nw_ref, whi_ref, wlo_ref, route_ref, count_ref, *, n_experts, t_real):
    i = pl.program_id(0)
    tm = x_ref.shape[0]

    @pl.when(i == 0)
    def _():
        count_ref[...] = jnp.zeros_like(count_ref)

    xn = _rms(x_ref[...]) * nw_ref[...]
    hi = xn.astype(BF16)
    lo = (xn - hi.astype(F32)).astype(BF16)
    logits = _dot(hi, whi_ref[...]) + _dot(lo, whi_ref[...]) + _dot(hi, wlo_ref[...])
    lane = lax.broadcasted_iota(jnp.int32, logits.shape, 1)
    lane_f = lane.astype(F32)
    logits = jnp.where(lane < n_experts, logits, NEG_BIG)
    m1 = jnp.max(logits, axis=-1, keepdims=True)
    i1 = jnp.min(jnp.where(logits == m1, lane_f, float(LANES)), axis=-1, keepdims=True)
    rest = jnp.where(lane_f == i1, NEG_BIG, logits)
    m2 = jnp.max(rest, axis=-1, keepdims=True)
    i2 = jnp.min(jnp.where(rest == m2, lane_f, float(LANES)), axis=-1, keepdims=True)
    e2 = jnp.exp(m2 - m1)
    g1 = 1.0 / (1.0 + e2)
    g2 = e2 / (1.0 + e2)

    row = lax.broadcasted_iota(jnp.int32, (tm, 1), 0) + i * tm
    valid = row < t_real
    sel = jnp.where(valid & ((lane_f == i1) | (lane_f == i2)), 1.0, 0.0)
    r = lax.broadcasted_iota(jnp.int32, (tm, tm), 0)
    q = lax.broadcasted_iota(jnp.int32, (tm, tm), 1)
    before = jnp.where(q < r, 1.0, 0.0).astype(BF16)
    rank = count_ref[...] + _dot(before, sel.astype(BF16))
    r1 = jnp.sum(jnp.where(lane_f == i1, rank, 0.0), axis=-1, keepdims=True)
    r2 = jnp.sum(jnp.where(lane_f == i2, rank, 0.0), axis=-1, keepdims=True)
    count_ref[...] += jnp.sum(sel, axis=0, keepdims=True)

    rec = jnp.zeros_like(logits)
    for k, v in ((ROUTE_E1, i1), (ROUTE_E2, i2), (ROUTE_R1, r1), (ROUTE_R2, r2), (ROUTE_G1, g1), (ROUTE_G2, g2)):
        rec = jnp.where(lane == k, v, rec)
    route_ref[...] = jnp.where(valid, rec, 0.0)


def _router(x, nw, w_router, tm, t_real):
    t, d = x.shape
    e = w_router.shape[1]
    wpad = jnp.zeros((d, LANES), F32).at[:, :e].set(w_router)
    whi = wpad.astype(BF16)
    wlo = (wpad - whi.astype(F32)).astype(BF16)
    return pl.pallas_call(
        functools.partial(_router_kernel, n_experts=e, t_real=t_real),
        grid=(t // tm,),
        in_specs=[pl.BlockSpec((tm, d), lambda i: (i, 0)),
                  pl.BlockSpec((1, d), lambda i: (0, 0)),
                  pl.BlockSpec((d, LANES), lambda i: (0, 0)),
                  pl.BlockSpec((d, LANES), lambda i: (0, 0))],
        out_specs=[pl.BlockSpec((tm, LANES), lambda i: (i, 0)), pl.BlockSpec((1, LANES), lambda i: (0, 0))],
        out_shape=[jax.ShapeDtypeStruct((t, LANES), F32), jax.ShapeDtypeStruct((1, LANES), F32)],
        compiler_params=pltpu.CompilerParams(dimension_semantics=("arbitrary",),
                                             vmem_limit_bytes=VMEM_LIMIT),
    )(x, nw.reshape(1, d), whi, wlo)


def _row_copy(src, src_row, dst, dst_row, sem):
    return pltpu.make_async_copy(src.at[pl.ds(src_row, 1)], dst.at[pl.ds(dst_row, 1)], sem)


def _dispatch_kernel(pos_ref, x_hbm, xs_hbm, sem, *, tm, t_real, top_k):
    i = pl.program_id(0)
    n_rows = jnp.clip(t_real - i * tm, 0, tm)

    def start(r, carry):
        for k in range(top_k):
            _row_copy(x_hbm, i * tm + r, xs_hbm, pos_ref[(i * tm + r) * top_k + k], sem).start()
        return carry

    def wait(r, carry):
        for k in range(top_k):
            _row_copy(x_hbm, 0, xs_hbm, 0, sem).wait()
        return carry

    lax.fori_loop(0, n_rows, start, 0)
    lax.fori_loop(0, n_rows, wait, 0)


def _dispatch(pos, x, n_slots, tm, t_real, top_k):
    t, d = x.shape
    return pl.pallas_call(
        functools.partial(_dispatch_kernel, tm=tm, t_real=t_real, top_k=top_k),
        grid_spec=pltpu.PrefetchScalarGridSpec(
            num_scalar_prefetch=1, grid=(t // tm,),
            in_specs=[pl.BlockSpec(memory_space=pl.ANY)],
            out_specs=pl.BlockSpec(memory_space=pl.ANY),
            scratch_shapes=[pltpu.SemaphoreType.DMA(())]),
        out_shape=jax.ShapeDtypeStruct((n_slots, d), F32),
        compiler_params=pltpu.CompilerParams(dimension_semantics=("arbitrary",)),
    )(pos, x)


def _expert_kernel(blk_e_ref, blk_n_ref, xs_ref, nw_ref, wg_ref, wu_ref, wd_ref, o_ref, xn_scr, acc_scr):
    del blk_e_ref
    j = pl.program_id(0)
    f = pl.program_id(1)
    n_valid = blk_n_ref[j]
    last = f == pl.num_programs(1) - 1

    @pl.when((n_valid > 0) & (f == 0))
    def _():
        row = lax.broadcasted_iota(jnp.int32, (xs_ref.shape[0], 1), 0)
        x = jnp.where(row < n_valid, xs_ref[...], 0.0)
        xn_scr[...] = (_rms(x) * nw_ref[...]).astype(BF16)
        acc_scr[...] = jnp.zeros_like(acc_scr)

    @pl.when(n_valid > 0)
    def _():
        xn = xn_scr[...]
        h = _silu(_dot(xn, wg_ref[...])) * _dot(xn, wu_ref[...])
        acc_scr[...] += _dot(h.astype(BF16), wd_ref[...])

    @pl.when((n_valid > 0) & last)
    def _():
        o_ref[...] = acc_scr[...]

    @pl.when((n_valid == 0) & last)
    def _():
        o_ref[...] = jnp.zeros_like(o_ref)


def _experts(blk_e, blk_n, xs, nw, wg, wu, wd, tmb, tf):
    s, d = xs.shape
    ff = wg.shape[2]
    nf = ff // tf

    def w_up(j, f, be, bn):
        return (be[j], 0, jnp.where(bn[j] > 0, f, nf - 1))

    def w_down(j, f, be, bn):
        return (be[j], jnp.where(bn[j] > 0, f, nf - 1), 0)

    return pl.pallas_call(
        _expert_kernel,
        grid_spec=pltpu.PrefetchScalarGridSpec(
            num_scalar_prefetch=2, grid=(s // tmb, nf),
            in_specs=[pl.BlockSpec((tmb, d), lambda j, f, be, bn: (j, 0)),
                      pl.BlockSpec((1, d), lambda j, f, be, bn: (0, 0)),
                      pl.BlockSpec((None, d, tf), w_up),
                      pl.BlockSpec((None, d, tf), w_up),
                      pl.BlockSpec((None, tf, d), w_down)],
            out_specs=pl.BlockSpec((tmb, d), lambda j, f, be, bn: (j, 0)),
            scratch_shapes=[pltpu.VMEM((tmb, d), BF16), pltpu.VMEM((tmb, d), F32)]),
        out_shape=jax.ShapeDtypeStruct((s, d), F32),
        compiler_params=pltpu.CompilerParams(dimension_semantics=("parallel", "arbitrary"),
                                             vmem_limit_bytes=VMEM_LIMIT),
    )(blk_e, blk_n, xs, nw.reshape(1, d), wg, wu, wd)


def _combine_kernel(pos_ref, x_ref, route_ref, nw_ref, ys_hbm, o_ref, ybuf, sem, *, tm, t_real, top_k):
    i = pl.program_id(0)
    n_rows = jnp.clip(t_real - i * tm, 0, tm)

    def start(r, carry):
        for k in range(top_k):
            _row_copy(ys_hbm, pos_ref[(i * tm + r) * top_k + k], ybuf.at[k], r, sem).start()
        return carry

    def wait(r, carry):
        for k in range(top_k):
            _row_copy(ys_hbm, 0, ybuf.at[k], 0, sem).wait()
        return carry

    lax.fori_loop(0, n_rows, start, 0)
    lax.fori_loop(0, n_rows, wait, 0)
    route = route_ref[...]
    row = lax.broadcasted_iota(jnp.int32, (tm, 1), 0)
    mix = route[:, ROUTE_G1:ROUTE_G1 + 1] * ybuf[0] + route[:, ROUTE_G2:ROUTE_G2 + 1] * ybuf[1]
    out = x_ref[...] + jnp.where(row < n_rows, mix, 0.0)
    o_ref[...] = _rms(out) * nw_ref[...]


def _combine(pos, x, route, nw, ys, tm, t_real, top_k):
    t, d = x.shape
    assert top_k == 2
    return pl.pallas_call(
        functools.partial(_combine_kernel, tm=tm, t_real=t_real, top_k=top_k),
        grid_spec=pltpu.PrefetchScalarGridSpec(
            num_scalar_prefetch=1, grid=(t // tm,),
            in_specs=[pl.BlockSpec((tm, d), lambda i, p: (i, 0)),
                      pl.BlockSpec((tm, LANES), lambda i, p: (i, 0)),
                      pl.BlockSpec((1, d), lambda i, p: (0, 0)),
                      pl.BlockSpec(memory_space=pl.ANY)],
            out_specs=pl.BlockSpec((tm, d), lambda i, p: (i, 0)),
            scratch_shapes=[pltpu.VMEM((top_k, tm, d), F32), pltpu.SemaphoreType.DMA(())]),
        out_shape=jax.ShapeDtypeStruct((t, d), F32),
        compiler_params=pltpu.CompilerParams(dimension_semantics=("arbitrary",),
                                             vmem_limit_bytes=VMEM_LIMIT),
    )(pos, x, route, nw.reshape(1, d), ys)


def _moe_plan(route, counts, n_experts, tmb, n_blocks, top_k):
    cnt = counts[0, :n_experts].astype(jnp.int32)
    nblk = (cnt + tmb - 1) // tmb
    blk_end = jnp.cumsum(nblk)
    blk_start = blk_end - nblk
    slot0 = blk_start * tmb
    e_idx = route[:, ROUTE_E1:ROUTE_E1 + top_k].astype(jnp.int32)
    rank = route[:, ROUTE_R1:ROUTE_R1 + top_k].astype(jnp.int32)
    onehot = e_idx[:, :, None] == jnp.arange(n_experts)[None, None, :]
    pos = jnp.sum(jnp.where(onehot, slot0[None, None, :], 0), axis=-1) + rank
    j = jnp.arange(n_blocks)
    used = j < blk_end[-1]
    blk_e = jnp.minimum(jnp.sum(j[:, None] >= blk_end[None, :], axis=1), n_experts - 1)
    last_e = jnp.max(jnp.where(nblk > 0, jnp.arange(n_experts), 0))
    blk_e = jnp.where(used, blk_e, last_e).astype(jnp.int32)
    blk_n = jnp.where(used, jnp.clip(cnt[blk_e] - (j - blk_start[blk_e]) * tmb, 0, tmb), 0).astype(jnp.int32)
    return pos.reshape(-1).astype(jnp.int32), blk_e, blk_n


class _Group:
    def __init__(self, row0, nb, nc, c, nc_grid=None):
        self.row0, self.nb, self.nc, self.c = row0, nb, nc, c
        self.nc_grid = nc if nc_grid is None else nc_grid

    def rows(self, width, col_block):
        base, nc, c = self.row0 // self.c, self.nc, self.c
        return pl.BlockSpec((c, width), lambda b, i: (base + b * nc + i, col_block))

    def chunk_rows(self, width):
        nc, c = self.nc, self.c
        return pl.BlockSpec((c, width), lambda b, i: (jnp.minimum(i, nc - 1), 0))


def _full(shape):
    nd = len(shape)
    return pl.BlockSpec(tuple(shape), lambda b, i: (0,) * nd)


def _per_stream(shape):
    nd = len(shape)
    return pl.BlockSpec((None,) + tuple(shape[1:]), lambda b, i: (b,) + (0,) * (nd - 1))


def _scan_call(kernel, group, row_inputs, const_inputs, stream_inputs, y_prev, t_pad, width, extra_out_shapes,
               extra_out_specs, scratch_shapes):
    arrays, specs = [], []
    for arr, w, cb in row_inputs:
        arrays.append(arr)
        specs.append(group.rows(w, cb))
    for item in const_inputs:
        if isinstance(item, tuple):
            arr, spec = item
        else:
            arr, spec = item, _full(item.shape)
        arrays.append(arr)
        specs.append(spec)
    for arr in stream_inputs:
        arrays.append(arr)
        specs.append(_per_stream(arr.shape))
    aliases = {}
    if y_prev is not None:
        aliases = {len(arrays): 0}
        arrays.append(y_prev)
        specs.append(pl.BlockSpec(memory_space=pl.ANY))
    out_shape = [jax.ShapeDtypeStruct((t_pad, width), F32)] + list(extra_out_shapes)
    out_specs = [group.rows(width, 0)] + list(extra_out_specs)
    return pl.pallas_call(
        functools.partial(kernel, c=group.c, nc=group.nc, has_prev=y_prev is not None),
        grid=(group.nb, group.nc_grid),
        in_specs=specs,
        out_specs=out_specs,
        out_shape=out_shape,
        scratch_shapes=scratch_shapes,
        input_output_aliases=aliases,
        compiler_params=pltpu.CompilerParams(dimension_semantics=("parallel", "arbitrary"),
                                             vmem_limit_bytes=VMEM_LIMIT),
    )(*arrays)


def _ssd_kernel(*refs, c, nc, has_prev):
    (z_ref, x_ref, bc_ref, dt_ref, convw_ref, convb_ref, dtb_ref, alog_ref, dskip_ref, nw_ref, expand_ref,
     hist0_ref, s0_ref) = refs[:13]
    y_ref, s_ref, hout_ref, hist_scr = refs[13 + int(has_prev):]
    i = pl.program_id(1)
    d_inner = x_ref.shape[1]
    n_state = SSD_STATE
    n_pairs = d_inner // LANES
    pairs_per_group = n_pairs // SSD_GROUPS
    n_hist = SSD_CONV - 1

    @pl.when(i == 0)
    def _():
        s_ref[...] = s0_ref[...]
        hist_scr[HIST_ROWS - n_hist:HIST_ROWS, :] = hist0_ref[...]

    @pl.when(i >= nc)
    def _():
        y_ref[...] = jnp.zeros_like(y_ref)

    @pl.when(i < nc)
    def _():
        hist_scr[HIST_ROWS:HIST_ROWS + c, 0:d_inner] = x_ref[...]
        hist_scr[HIST_ROWS:HIST_ROWS + c, d_inner:] = bc_ref[...]
        conv = convb_ref[...]
        for j in range(SSD_CONV):
            conv = conv + hist_scr[HIST_ROWS - n_hist + j:HIST_ROWS - n_hist + j + c, :] * convw_ref[j:j + 1, :]
        tail = hist_scr[HIST_ROWS + c - n_hist:HIST_ROWS + c, :]
        hist_scr[HIST_ROWS - n_hist:HIST_ROWS, :] = tail
        hout_ref[...] = tail
        xbc = _silu(conv)
        xs = xbc[:, 0:d_inner]
        bm = xbc[:, d_inner:d_inner + SSD_GROUPS * n_state]
        cm = xbc[:, d_inner + SSD_GROUPS * n_state:]

        x_dt = dt_ref[...] + dtb_ref[...]
        dt = jnp.maximum(x_dt, 0.0) + jnp.log1p(jnp.exp(-jnp.abs(x_dt)))
        log_a = dt * (-jnp.exp(alog_ref[...]))
        cum = _cumsum_time(log_a)
        expand = expand_ref[...]
        dt_e = _exact_ldot(dt, expand)
        cum_e = _exact_ldot(cum, expand)
        last_e = cum_e[c - 1:c, :]
        xdt = xs * dt_e
        wx = jnp.exp(last_e - cum_e) * xdt
        ecum = jnp.exp(cum_e)
        sdecay = jnp.exp(last_e)

        row = lax.broadcasted_iota(jnp.int32, (c, 2 * c), 0)
        col = lax.broadcasted_iota(jnp.int32, (c, 2 * c), 1)
        second = col >= c
        tcol = jnp.where(second, col - c, col)
        causal2 = tcol <= row
        diag2 = tcol == row
        r2 = lax.broadcasted_iota(jnp.int32, (2 * c, LANES), 0)
        l2 = lax.broadcasted_iota(jnp.int32, (2 * c, LANES), 1)
        half2 = (r2 >= c) == (l2 >= SSD_HEADDIM)

        ys = []
        for g in range(SSD_GROUPS):
            cg = cm[:, g * n_state:(g + 1) * n_state].astype(BF16)
            bg = bm[:, g * n_state:(g + 1) * n_state].astype(BF16)
            scores2 = _dot_nt(cg, jnp.concatenate([bg, bg], axis=0))
            for p in range(g * pairs_per_group, (g + 1) * pairs_per_group):
                sl = slice(p * LANES, (p + 1) * LANES)
                cum_p = cum_e[:, sl]
                cum_col = jnp.where(second, cum_p[:, SSD_HEADDIM:SSD_HEADDIM + 1], cum_p[:, 0:1])
                cum_row = jnp.sum(jnp.where(diag2, cum_col, 0.0), axis=0, keepdims=True)
                seg = cum_col - cum_row
                m2 = (scores2 * jnp.exp(jnp.where(causal2, seg, NEG_BIG))).astype(BF16)
                xp = xdt[:, sl]
                x2 = jnp.where(half2, jnp.concatenate([xp, xp], axis=0), 0.0).astype(BF16)
                y = _dot(m2, x2)
                y = y + _dot(cg, s_ref[p].astype(BF16)) * ecum[:, sl]
                s_ref[p] = s_ref[p] * sdecay[:, sl] + _dot_tn(bg, wx[:, sl].astype(BF16))
                ys.append(y)
        y = jnp.concatenate(ys, axis=1)
        y = y + xs * dskip_ref[...]
        y = y * _silu(z_ref[...])
        gw = d_inner // SSD_GROUPS
        y = jnp.concatenate([_rms(y[:, g * gw:(g + 1) * gw]) for g in range(SSD_GROUPS)], axis=1)
        y_ref[...] = y * nw_ref[...]


def _ret_kernel(*refs, c, nc, has_prev):
    (q_ref, k_ref, v_ref, g_ref, sin_ref, cos_ref, dmat_ref, ecum_ref, wend_ref, sdec_ref, s0_ref) = refs[:11]
    y_ref, s_ref = refs[11 + int(has_prev):]
    i = pl.program_id(1)
    n_heads = q_ref.shape[1] // LANES

    @pl.when(i == 0)
    def _():
        s_ref[...] = s0_ref[...]

    @pl.when(i >= nc)
    def _():
        y_ref[...] = jnp.zeros_like(y_ref)

    @pl.when(i < nc)
    def _():
        sin = sin_ref[...]
        cos = cos_ref[...]
        even = (lax.broadcasted_iota(jnp.int32, (c, LANES), 1) % 2) == 0

        def rotate(x):
            nxt = pltpu.roll(x, LANES - 1, 1)
            prv = pltpu.roll(x, 1, 1)
            return x * cos + jnp.where(even, -nxt, prv) * sin

        scale = LANES ** -0.5
        ys = []
        for h in range(n_heads):
            sl = slice(h * LANES, (h + 1) * LANES)
            qh = rotate(q_ref[:, sl])
            kh = rotate(k_ref[:, sl]) * scale
            vh = v_ref[:, sl].astype(BF16)
            scores = _dot_nt(qh.astype(BF16), kh.astype(BF16)) * dmat_ref[h]
            y = _dot(scores.astype(BF16), vh) + _dot((qh * ecum_ref[:, sl]).astype(BF16), s_ref[h].astype(BF16))
            s_ref[h] = s_ref[h] * sdec_ref[:, sl] + _dot_tn((kh * wend_ref[:, sl]).astype(BF16), vh)
            ys.append(_rms(y) * _silu(g_ref[:, sl]))
        y_ref[...] = jnp.concatenate(ys, axis=1)


def _hgrn_kernel(*refs, c, nc, has_prev):
    (q_ref, f_ref, v_ref, g_ref, lb_ref, nw_ref, s0_ref) = refs[:7]
    y_ref, s_ref = refs[7 + int(has_prev):]
    i = pl.program_id(1)
    n_heads = q_ref.shape[1] // LANES

    @pl.when(i == 0)
    def _():
        s_ref[...] = s0_ref[...]

    @pl.when(i >= nc)
    def _():
        y_ref[...] = jnp.zeros_like(y_ref)

    @pl.when(i < nc)
    def _():
        lb = lb_ref[...]
        forget = lb + (1.0 - lb) * _sigmoid(f_ref[...])
        kk = 1.0 - forget
        gc = _cumsum_time(jnp.log(forget))
        last = gc[c - 1:c, :]
        qg = (_silu(q_ref[...]) * jnp.exp(gc)).astype(BF16)
        kg = (kk * jnp.exp(-gc)).astype(BF16)
        kw = (kk * jnp.exp(last - gc)).astype(BF16)
        sdec = jnp.exp(last)
        causal = _causal(c)
        ys = []
        for h in range(n_heads):
            sl = slice(h * LANES, (h + 1) * LANES)
            vh = v_ref[:, sl].astype(BF16)
            scores = jnp.where(causal, _dot_nt(qg[:, sl], kg[:, sl]), 0.0)
            y = _dot(scores.astype(BF16), vh) + _dot_nt(qg[:, sl], s_ref[h].astype(BF16))
            s_ref[h] = s_ref[h] * sdec[:, sl] + _dot_tn(vh, kw[:, sl])
            ys.append(_rms(y) * nw_ref[...] * _silu(g_ref[:, sl]))
        y_ref[...] = jnp.concatenate(ys, axis=1)


def _rotation_tables(pos, dk):
    inv = 1.0 / (ROPE_BASE ** jnp.linspace(0.0, 1.0, dk // 2, dtype=F32))
    ang = pos.astype(F32)[:, None] * jnp.repeat(inv, 2)[None, :]
    return jnp.sin(ang), jnp.cos(ang)


def _retention_decay(c, n_heads, dv):
    log_gamma = jnp.log1p(-(2.0 ** (-5.0 - jnp.arange(n_heads, dtype=F32))))
    cum = jnp.cumsum(jnp.broadcast_to(log_gamma, (c, n_heads)), axis=0)
    cum_h = cum.T
    causal = jnp.tril(jnp.ones((c, c), dtype=bool))
    dmat = jnp.exp(jnp.where(causal, cum_h[:, :, None] - cum_h[:, None, :], -jnp.inf))
    ecum = jnp.repeat(jnp.exp(cum), dv, axis=1)
    wend = jnp.repeat(jnp.exp(cum[-1][None, :] - cum), dv, axis=1)
    sdec = jnp.repeat(jnp.exp(cum[-1])[None, :], dv, axis=1)
    return dmat, ecum, wend, sdec


def _pair_heads(s):
    nb, h, n, p = s.shape
    return s.reshape(nb, h // 2, 2, n, p).transpose(0, 1, 3, 2, 4).reshape(nb, h // 2, n, 2 * p)


def _unpair_heads(s, p):
    nb, hp, n, _ = s.shape
    return s.reshape(nb, hp, n, 2, p).transpose(0, 1, 3, 2, 4).reshape(nb, 2 * hp, n, p)


def kernel(x_prompt, x_sample, state_ssd, state_ssd_conv, state_ret, state_hgrn, meta_tokens, norm_mix, norm_ffn,
           norm_final, w_in_ab, conv_w, conv_b, dt_bias, a_log, d_skip, ssd_norm, w_out_ab, w_ffn_gate, w_ffn_up,
           w_ffn_down, w_in_c, hgrn_lb, hgrn_norm, w_out_c, w_router, w_exp_gate, w_exp_up, w_exp_down):
    bp, seq, d = x_prompt.shape
    bs, dec_seq, _ = x_sample.shape
    n_meta = meta_tokens.shape[0]
    depth = norm_mix.shape[0]
    assert depth == 2 and seq % CHUNK == 0 and n_meta == N_META and d % LANES == 0
    n_ssd_heads = d_skip.shape[1]
    d_inner = n_ssd_heads * SSD_HEADDIM
    bc_w = 2 * SSD_GROUPS * SSD_STATE
    conv_dim = d_inner + bc_w
    ret_w = RET_HEADS * LANES
    assert d_inner == d and ret_w == d and conv_dim == conv_w.shape[2]

    tp, ts = bp * seq, bs * dec_seq
    t_real = tp + ts + n_meta
    tm = 512 if t_real >= 4096 else 128
    t_pad = -(-t_real // tm) * tm
    g_prompt = _Group(0, bp, seq // CHUNK, CHUNK)
    g_sample = _Group(tp, bs, 1, dec_seq)
    g_meta = _Group(tp + ts, 1, 1, n_meta, nc_grid=(t_pad - tp - ts) // n_meta)
    assert tp % dec_seq == 0 and (tp + ts) % n_meta == 0 and (t_pad - tp - ts) % n_meta == 0

    x0 = jnp.concatenate([x_prompt.reshape(tp, d), x_sample.reshape(ts, d), meta_tokens,
                          jnp.zeros((t_pad - t_real, d), F32)], axis=0)

    w_in = w_in_ab[0]
    o_z, o_xbc, o_dt, o_q = 0, d_inner, d_inner + conv_dim, d_inner + conv_dim + n_ssd_heads
    n_cols = 6 * d + bc_w + LANES
    tn = 768
    n_cols_pad = -(-n_cols // tn) * tn
    w_perm = jnp.concatenate([
        w_in[:, o_z:o_z + d_inner], w_in[:, o_q:o_q + 4 * ret_w], w_in[:, o_xbc:o_xbc + conv_dim],
        w_in[:, o_dt:o_dt + n_ssd_heads], jnp.zeros((d, n_cols_pad - n_cols + LANES - n_ssd_heads), F32)],
        axis=1).astype(BF16)
    proj = _normmm(x0, norm_mix[0], w_perm, tm, tn)
    cb_x, cb_bc, cb_dt = 5, (6 * d) // bc_w, (6 * d + bc_w) // LANES
    assert (6 * d) % bc_w == 0

    expand = (jnp.arange(LANES)[:, None] == (jnp.arange(d_inner) // SSD_HEADDIM)[None, :]).astype(BF16)
    pad_h = LANES - n_ssd_heads
    ssd_consts = [conv_w[0], conv_b[0].reshape(1, conv_dim), jnp.pad(dt_bias[0], (0, pad_h)).reshape(1, LANES),
                  jnp.pad(a_log[0], (0, pad_h)).reshape(1, LANES),
                  jnp.repeat(d_skip[0], SSD_HEADDIM).reshape(1, d_inner), ssd_norm[0].reshape(1, d_inner), expand]
    n_pairs = n_ssd_heads // 2

    def ssd(group, hist0, s0, y_prev):
        nb = group.nb
        return _scan_call(
            _ssd_kernel, group,
            [(proj, d_inner, 0), (proj, d_inner, cb_x), (proj, bc_w, cb_bc), (proj, LANES, cb_dt)],
            ssd_consts, [hist0, _pair_heads(s0)], y_prev, t_pad, d_inner,
            [jax.ShapeDtypeStruct((nb, n_pairs, SSD_STATE, LANES), F32),
             jax.ShapeDtypeStruct((nb, SSD_CONV - 1, conv_dim), F32)],
            [_per_stream((nb, n_pairs, SSD_STATE, LANES)), _per_stream((nb, SSD_CONV - 1, conv_dim))],
            [pltpu.VMEM((HIST_ROWS + group.c, conv_dim), F32)])

    def bcast(s, n):
        return jnp.broadcast_to(s, (n,) + s.shape[1:])

    zeros = jnp.zeros
    y_ssd, s_m, h_m = ssd(g_meta, zeros((1, SSD_CONV - 1, conv_dim), F32),
                          zeros((1, n_ssd_heads, SSD_STATE, SSD_HEADDIM), F32), None)
    y_ssd, ssd_p, conv_p = ssd(g_prompt, bcast(h_m, bp), bcast(_unpair_heads(s_m, SSD_HEADDIM), bp), y_ssd)
    y_ssd, ssd_s, conv_s = ssd(g_sample, state_ssd_conv[0], state_ssd[0], y_ssd)
    ssd_p = _unpair_heads(ssd_p, SSD_HEADDIM)
    ssd_s = _unpair_heads(ssd_s, SSD_HEADDIM)

    def ret(group, pos, s0, y_prev):
        nb = group.nb
        sin, cos = _rotation_tables(pos, LANES)
        dmat, ecum, wend, sdec = _retention_decay(group.c, RET_HEADS, LANES)
        return _scan_call(
            _ret_kernel, group,
            [(proj, ret_w, 1), (proj, ret_w, 2), (proj, ret_w, 3), (proj, ret_w, 4)],
            [(sin, group.chunk_rows(LANES)), (cos, group.chunk_rows(LANES)), dmat, ecum, wend, sdec],
            [s0], y_prev, t_pad, ret_w,
            [jax.ShapeDtypeStruct((nb, RET_HEADS, LANES, LANES), F32)],
            [_per_stream((nb, RET_HEADS, LANES, LANES))], [])

    y_ret, r_m = ret(g_meta, jnp.arange(n_meta), zeros((1, RET_HEADS, LANES, LANES), F32), None)
    y_ret, ret_p = ret(g_prompt, n_meta + jnp.arange(seq), bcast(r_m, bp), y_ret)
    y_ret, ret_s = ret(g_sample, n_meta + PAST_LEN + jnp.arange(dec_seq), state_ret[0], y_ret)

    w_out = w_out_ab[0].astype(BF16)
    x1 = _outproj(x0, [y_ssd, y_ret], [w_out[:d_inner], w_out[d_inner:]], tm)
    ff = w_ffn_gate.shape[2]
    tf = 256 if ff % 256 == 0 else ff
    x2 = _ffn(x1, norm_ffn[0], w_ffn_gate[0].astype(BF16), w_ffn_up[0].astype(BF16), w_ffn_down[0].astype(BF16),
              tm, tf)

    proj_c = _normmm(x2, norm_mix[1], w_in_c[0].astype(BF16), tm, d)
    lb_soft = jax.nn.softmax(hgrn_lb.astype(F32), axis=0)
    lb = (jnp.cumsum(lb_soft, axis=0) - lb_soft[0])[1].reshape(1, d)
    n_hg = d // HG_DK
    hg_consts = [lb, hgrn_norm[0].reshape(1, LANES)]

    def hgrn(group, s0, y_prev):
        nb = group.nb
        return _scan_call(
            _hgrn_kernel, group,
            [(proj_c, d, 0), (proj_c, d, 1), (proj_c, d, 2), (proj_c, d, 3)],
            hg_consts, [jnp.swapaxes(s0, 2, 3)], y_prev, t_pad, d,
            [jax.ShapeDtypeStruct((nb, n_hg, LANES, HG_DK), F32)],
            [_per_stream((nb, n_hg, LANES, HG_DK))], [])

    y_hg, g_m = hgrn(g_meta, zeros((1, n_hg, HG_DK, LANES), F32), None)
    y_hg, hg_p = hgrn(g_prompt, bcast(jnp.swapaxes(g_m, 2, 3), bp), y_hg)
    y_hg, hg_s = hgrn(g_sample, state_hgrn[0], y_hg)
    hg_p = jnp.swapaxes(hg_p, 2, 3)
    hg_s = jnp.swapaxes(hg_s, 2, 3)

    x3 = _outproj(x2, [y_hg], [w_out_c[0].astype(BF16)], tm)
    n_exp = w_router.shape[2]
    route, counts = _router(x3, norm_ffn[1], w_router[0], tm, t_real)
    tmb = tm
    n_blocks = (TOP_K * t_real + n_exp * (tmb - 1)) // tmb
    pos, blk_e, blk_n = _moe_plan(route, counts, n_exp, tmb, n_blocks, TOP_K)
    xs = _dispatch(pos, x3, n_blocks * tmb, tm, t_real, TOP_K)
    ffe = w_exp_gate.shape[3]
    tfe = 512 if ffe % 512 == 0 else ffe
    ys = _experts(blk_e, blk_n, xs, norm_ffn[1], w_exp_gate[0].astype(BF16), w_exp_up[0].astype(BF16),
                  w_exp_down[0].astype(BF16), tmb, tfe)
    y = _combine(pos, x3, route, norm_final, ys, tm, t_real, TOP_K)

    y_prompt = y[:tp].reshape(bp, seq, d)
    y_sample = y[tp:tp + ts].reshape(bs, dec_seq, d)
    return (y_prompt, y_sample, ssd_p[None], conv_p[None], ret_p[None], hg_p[None],
            ssd_s[None], conv_s[None], ret_s[None], hg_s[None])
```

```python
import functools

import jax
import jax.numpy as jnp
from jax import lax
from jax.experimental import pallas as pl
from jax.experimental.pallas import tpu as pltpu

F32 = jnp.float32
BF16 = jnp.bfloat16

CHUNK = 64
N_META = 16
PAST_LEN = 2048
EPS = 1e-6
SSD_HEADDIM = 64
SSD_GROUPS = 2
SSD_STATE = 128
SSD_CONV = 4
RET_HEADS = 8
ROPE_BASE = 10000.0
HG_DK = 128
TOP_K = 2

LANES = 128
HIST_ROWS = 8
VMEM_LIMIT = 56 * 1024 * 1024
NEG_BIG = -1e30


def _dot(a, b):
    return jnp.dot(a, b, preferred_element_type=F32)


def _dot_nt(a, b):
    return lax.dot_general(a, b, (((1,), (1,)), ((), ())), preferred_element_type=F32)


def _dot_tn(a, b):
    return lax.dot_general(a, b, (((0,), (0,)), ((), ())), preferred_element_type=F32)


def _split3(x):
    hi = x.astype(BF16)
    r = x - hi.astype(F32)
    mid = r.astype(BF16)
    lo = (r - mid.astype(F32)).astype(BF16)
    return hi, mid, lo


def _exact_ldot(x, m01):
    hi, mid, lo = _split3(x)
    return _dot(hi, m01) + _dot(mid, m01) + _dot(lo, m01)


def _exact_rdot(m01, x):
    hi, mid, lo = _split3(x)
    return _dot(m01, hi) + _dot(m01, mid) + _dot(m01, lo)


def _causal(c):
    row = lax.broadcasted_iota(jnp.int32, (c, c), 0)
    col = lax.broadcasted_iota(jnp.int32, (c, c), 1)
    return row >= col


def _cumsum_time(x):
    c = x.shape[0]
    return _exact_rdot(jnp.where(_causal(c), 1.0, 0.0).astype(BF16), x)


def _sigmoid(x):
    return 1.0 / (1.0 + jnp.exp(-x))


def _silu(x):
    return x * _sigmoid(x)


def _rms(x):
    return x * lax.rsqrt(jnp.mean(x * x, axis=-1, keepdims=True) + EPS)


def _normmm_kernel(x_ref, nw_ref, w_ref, ws_ref, o_ref, os_ref, xn_scr):
    @pl.when(pl.program_id(1) == 0)
    def _():
        xn = (_rms(x_ref[...]) * nw_ref[...]).astype(BF16)
        xn_scr[...] = xn
        os_ref[...] = _dot(xn, ws_ref[...])

    o_ref[...] = _dot(xn_scr[...], w_ref[...]).astype(BF16)


def _normmm(x, nw, w, w_side, tm, tn):
    t, k = x.shape
    n, ns = w.shape[1], w_side.shape[1]
    return pl.pallas_call(
        _normmm_kernel,
        grid=(t // tm, n // tn),
        in_specs=[pl.BlockSpec((tm, k), lambda i, j: (i, 0)),
                  pl.BlockSpec((1, k), lambda i, j: (0, 0)),
                  pl.BlockSpec((k, tn), lambda i, j: (0, j)),
                  pl.BlockSpec((k, ns), lambda i, j: (0, 0))],
        out_specs=[pl.BlockSpec((tm, tn), lambda i, j: (i, j)), pl.BlockSpec((tm, ns), lambda i, j: (i, 0))],
        out_shape=[jax.ShapeDtypeStruct((t, n), BF16), jax.ShapeDtypeStruct((t, ns), F32)],
        scratch_shapes=[pltpu.VMEM((tm, k), BF16)],
        compiler_params=pltpu.CompilerParams(dimension_semantics=("parallel", "arbitrary"),
                                             vmem_limit_bytes=VMEM_LIMIT),
    )(x, nw.reshape(1, k), w, w_side)


def _outproj_kernel(*refs, n_in):
    x_ref = refs[0]
    ys = refs[1:1 + n_in]
    ws = refs[1 + n_in:1 + 2 * n_in]
    o_ref = refs[1 + 2 * n_in]
    acc = x_ref[...]
    for y, w in zip(ys, ws):
        acc = acc + _dot(y[...].astype(BF16), w[...])
    o_ref[...] = acc


def _outproj(x, ys, ws, tm):
    t, d = x.shape
    n_in = len(ys)
    in_specs = [pl.BlockSpec((tm, d), lambda i: (i, 0))]
    in_specs += [pl.BlockSpec((tm, y.shape[1]), lambda i: (i, 0)) for y in ys]
    in_specs += [pl.BlockSpec(w.shape, lambda i: (0, 0)) for w in ws]
    return pl.pallas_call(
        functools.partial(_outproj_kernel, n_in=n_in),
        grid=(t // tm,),
        in_specs=in_specs,
        out_specs=pl.BlockSpec((tm, d), lambda i: (i, 0)),
        out_shape=jax.ShapeDtypeStruct((t, d), F32),
        compiler_params=pltpu.CompilerParams(dimension_semantics=("parallel",),
                                             vmem_limit_bytes=VMEM_LIMIT),
    )(x, *ys, *ws)


def _ffn_kernel(x_ref, nw_ref, wg_ref, wu_ref, wd_ref, o_ref, xn_scr, acc_scr):
    f = pl.program_id(1)

    @pl.when(f == 0)
    def _():
        x = x_ref[...]
        xn_scr[...] = (_rms(x) * nw_ref[...]).astype(BF16)
        acc_scr[...] = x

    xn = xn_scr[...]
    h = _silu(_dot(xn, wg_ref[...])) * _dot(xn, wu_ref[...])
    acc_scr[...] += _dot(h.astype(BF16), wd_ref[...])

    @pl.when(f == pl.num_programs(1) - 1)
    def _():
        o_ref[...] = acc_scr[...]


def _ffn(x, nw, wg, wu, wd, tm, tf):
    t, d = x.shape
    ff = wg.shape[1]
    return pl.pallas_call(
        _ffn_kernel,
        grid=(t // tm, ff // tf),
        in_specs=[pl.BlockSpec((tm, d), lambda i, f: (i, 0)),
                  pl.BlockSpec((1, d), lambda i, f: (0, 0)),
                  pl.BlockSpec((d, tf), lambda i, f: (0, f)),
                  pl.BlockSpec((d, tf), lambda i, f: (0, f)),
                  pl.BlockSpec((tf, d), lambda i, f: (f, 0))],
        out_specs=pl.BlockSpec((tm, d), lambda i, f: (i, 0)),
        out_shape=jax.ShapeDtypeStruct((t, d), F32),
        scratch_shapes=[pltpu.VMEM((tm, d), BF16), pltpu.VMEM((tm, d), F32)],
        compiler_params=pltpu.CompilerParams(dimension_semantics=("parallel", "arbitrary"),
                                             vmem_limit_bytes=VMEM_LIMIT),
    )(x, nw.reshape(1, d), wg, wu, wd)


ROUTE_E1, ROUTE_E2, ROUTE_R1, ROUTE_R2, ROUTE_G1, ROUTE_G2 = range(6)


def _router_kernel(x_ref, nw_ref, whi_ref, wlo_ref, route_ref, count_ref, *, n_experts, t_real):
    i = pl.program_id(0)
    tm = x_ref.shape[0]

    @pl.when(i == 0)
    def _():
        count_ref[...] = jnp.zeros_like(count_ref)

    xn = _rms(x_ref[...]) * nw_ref[...]
    hi = xn.astype(BF16)
    lo = (xn - hi.astype(F32)).astype(BF16)
    logits = _dot(hi, whi_ref[...]) + _dot(lo, whi_ref[...]) + _dot(hi, wlo_ref[...])
    lane = lax.broadcasted_iota(jnp.int32, logits.shape, 1)
    lane_f = lane.astype(F32)
    logits = jnp.where(lane < n_experts, logits, NEG_BIG)
    m1 = jnp.max(logits, axis=-1, keepdims=True)
    i1 = jnp.min(jnp.where(logits == m1, lane_f, float(LANES)), axis=-1, keepdims=True)
    rest = jnp.where(lane_f == i1, NEG_BIG, logits)
    m2 = jnp.max(rest, axis=-1, keepdims=True)
    i2 = jnp.min(jnp.where(rest == m2, lane_f, float(LANES)), axis=-1, keepdims=True)
    e2 = jnp.exp(m2 - m1)
    g1 = 1.0 / (1.0 + e2)
    g2 = e2 / (1.0 + e2)

    row = lax.broadcasted_iota(jnp.int32, (tm, 1), 0) + i * tm
    valid = row < t_real
    sel = jnp.where(valid & ((lane_f == i1) | (lane_f == i2)), 1.0, 0.0)
    r = lax.broadcasted_iota(jnp.int32, (tm, tm), 0)
    q = lax.broadcasted_iota(jnp.int32, (tm, tm), 1)
    before = jnp.where(q < r, 1.0, 0.0).astype(BF16)
    rank = count_ref[...] + _dot(before, sel.astype(BF16))
    r1 = jnp.sum(jnp.where(lane_f == i1, rank, 0.0), axis=-1, keepdims=True)
    r2 = jnp.sum(jnp.where(lane_f == i2, rank, 0.0), axis=-1, keepdims=True)
    count_ref[...] += jnp.sum(sel, axis=0, keepdims=True)

    rec = jnp.zeros_like(logits)
    for k, v in ((ROUTE_E1, i1), (ROUTE_E2, i2), (ROUTE_R1, r1), (ROUTE_R2, r2), (ROUTE_G1, g1), (ROUTE_G2, g2)):
        rec = jnp.where(lane == k, v, rec)
    route_ref[...] = jnp.where(valid, rec, 0.0)


def _router(x, nw, w_router, tm, t_real):
    t, d = x.shape
    e = w_router.shape[1]
    wpad = jnp.zeros((d, LANES), F32).at[:, :e].set(w_router)
    whi = wpad.astype(BF16)
    wlo = (wpad - whi.astype(F32)).astype(BF16)
    return pl.pallas_call(
        functools.partial(_router_kernel, n_experts=e, t_real=t_real),
        grid=(t // tm,),
        in_specs=[pl.BlockSpec((tm, d), lambda i: (i, 0)),
                  pl.BlockSpec((1, d), lambda i: (0, 0)),
                  pl.BlockSpec((d, LANES), lambda i: (0, 0)),
                  pl.BlockSpec((d, LANES), lambda i: (0, 0))],
        out_specs=[pl.BlockSpec((tm, LANES), lambda i: (i, 0)), pl.BlockSpec((1, LANES), lambda i: (0, 0))],
        out_shape=[jax.ShapeDtypeStruct((t, LANES), F32), jax.ShapeDtypeStruct((1, LANES), F32)],
        compiler_params=pltpu.CompilerParams(dimension_semantics=("arbitrary",),
                                             vmem_limit_bytes=VMEM_LIMIT),
    )(x, nw.reshape(1, d), whi, wlo)


def _row_copy(src, src_row, dst, dst_row, sem):
    return pltpu.make_async_copy(src.at[pl.ds(src_row, 1)], dst.at[pl.ds(dst_row, 1)], sem)


ROW_LOOP_UNROLL = 8


def _dispatch_kernel(pos_ref, x_ref, xs_hbm, sem, *, tm, top_k):
    i = pl.program_id(0)

    def start(r, carry):
        for k in range(top_k):
            _row_copy(x_ref, r, xs_hbm, pos_ref[(i * tm + r) * top_k + k], sem).start()
        return carry

    def wait(r, carry):
        for k in range(top_k):
            _row_copy(x_ref, 0, xs_hbm, 0, sem).wait()
        return carry

    lax.fori_loop(0, tm, start, 0, unroll=ROW_LOOP_UNROLL)
    lax.fori_loop(0, tm, wait, 0, unroll=ROW_LOOP_UNROLL)


def _dispatch(pos, x, n_rows_out, tm, top_k):
    t, d = x.shape
    return pl.pallas_call(
        functools.partial(_dispatch_kernel, tm=tm, top_k=top_k),
        grid_spec=pltpu.PrefetchScalarGridSpec(
            num_scalar_prefetch=1, grid=(t // tm,),
            in_specs=[pl.BlockSpec((tm, d), lambda i, p: (i, 0))],
            out_specs=pl.BlockSpec(memory_space=pl.ANY),
            scratch_shapes=[pltpu.SemaphoreType.DMA(())]),
        out_shape=jax.ShapeDtypeStruct((n_rows_out, d), F32),
        compiler_params=pltpu.CompilerParams(dimension_semantics=("arbitrary",),
                                             vmem_limit_bytes=VMEM_LIMIT),
    )(pos, x)


def _expert_kernel(blk_e_ref, blk_n_ref, xs_ref, nw_ref, wg_ref, wu_ref, wd_ref, o_ref, xn_scr, acc_scr):
    del blk_e_ref
    j = pl.program_id(0)
    f = pl.program_id(1)
    n_valid = blk_n_ref[j]
    last = f == pl.num_programs(1) - 1

    @pl.when((n_valid > 0) & (f == 0))
    def _():
        row = lax.broadcasted_iota(jnp.int32, (xs_ref.shape[0], 1), 0)
        x = jnp.where(row < n_valid, xs_ref[...], 0.0)
        xn_scr[...] = (_rms(x) * nw_ref[...]).astype(BF16)
        acc_scr[...] = jnp.zeros_like(acc_scr)

    @pl.when(n_valid > 0)
    def _():
        xn = xn_scr[...]
        h = _silu(_dot(xn, wg_ref[...])) * _dot(xn, wu_ref[...])
        acc_scr[...] += _dot(h.astype(BF16), wd_ref[...])

    @pl.when((n_valid > 0) & last)
    def _():
        o_ref[...] = acc_scr[...]

    @pl.when((n_valid == 0) & last)
    def _():
        o_ref[...] = jnp.zeros_like(o_ref)


def _experts(blk_e, blk_n, xs, nw, wg, wu, wd, tmb, tf):
    d = xs.shape[1]
    s = blk_e.shape[0] * tmb
    ff = wg.shape[2]
    nf = ff // tf

    def w_up(j, f, be, bn):
        return (be[j], 0, jnp.where(bn[j] > 0, f, nf - 1))

    def w_down(j, f, be, bn):
        return (be[j], jnp.where(bn[j] > 0, f, nf - 1), 0)

    return pl.pallas_call(
        _expert_kernel,
        grid_spec=pltpu.PrefetchScalarGridSpec(
            num_scalar_prefetch=2, grid=(s // tmb, nf),
            in_specs=[pl.BlockSpec((tmb, d), lambda j, f, be, bn: (j, 0)),
                      pl.BlockSpec((1, d), lambda j, f, be, bn: (0, 0)),
                      pl.BlockSpec((None, d, tf), w_up),
                      pl.BlockSpec((None, d, tf), w_up),
                      pl.BlockSpec((None, tf, d), w_down)],
            out_specs=pl.BlockSpec((tmb, d), lambda j, f, be, bn: (j, 0)),
            scratch_shapes=[pltpu.VMEM((tmb, d), BF16), pltpu.VMEM((tmb, d), F32)]),
        out_shape=jax.ShapeDtypeStruct((s, d), F32),
        compiler_params=pltpu.CompilerParams(dimension_semantics=("parallel", "arbitrary"),
                                             vmem_limit_bytes=VMEM_LIMIT),
    )(blk_e, blk_n, xs, nw.reshape(1, d), wg, wu, wd)


def _combine_kernel(pos_ref, x_ref, route_ref, nw_ref, ys_hbm, o_ref, ybuf, sem, *, tm, top_k):
    i = pl.program_id(0)

    def start(r, carry):
        for k in range(top_k):
            _row_copy(ys_hbm, pos_ref[(i * tm + r) * top_k + k], ybuf.at[k], r, sem).start()
        return carry

    def wait(r, carry):
        for k in range(top_k):
            _row_copy(ys_hbm, 0, ybuf.at[k], 0, sem).wait()
        return carry

    lax.fori_loop(0, tm, start, 0, unroll=ROW_LOOP_UNROLL)
    lax.fori_loop(0, tm, wait, 0, unroll=ROW_LOOP_UNROLL)
    route = route_ref[...]
    out = x_ref[...] + route[:, ROUTE_G1:ROUTE_G1 + 1] * ybuf[0] + route[:, ROUTE_G2:ROUTE_G2 + 1] * ybuf[1]
    o_ref[...] = _rms(out) * nw_ref[...]


def _combine(pos, x, route, nw, ys, tm, top_k):
    t, d = x.shape
    assert top_k == 2
    return pl.pallas_call(
        functools.partial(_combine_kernel, tm=tm, top_k=top_k),
        grid_spec=pltpu.PrefetchScalarGridSpec(
            num_scalar_prefetch=1, grid=(t // tm,),
            in_specs=[pl.BlockSpec((tm, d), lambda i, p: (i, 0)),
                      pl.BlockSpec((tm, LANES), lambda i, p: (i, 0)),
                      pl.BlockSpec((1, d), lambda i, p: (0, 0)),
                      pl.BlockSpec(memory_space=pl.ANY)],
            out_specs=pl.BlockSpec((tm, d), lambda i, p: (i, 0)),
            scratch_shapes=[pltpu.VMEM((top_k, tm, d), F32), pltpu.SemaphoreType.DMA(())]),
        out_shape=jax.ShapeDtypeStruct((t, d), F32),
        compiler_params=pltpu.CompilerParams(dimension_semantics=("arbitrary",),
                                             vmem_limit_bytes=VMEM_LIMIT),
    )(pos, x, route, nw.reshape(1, d), ys)


def _moe_plan(route, counts, n_experts, tmb, n_blocks, top_k, t_real):
    cnt = counts[0, :n_experts].astype(jnp.int32)
    nblk = (cnt + tmb - 1) // tmb
    blk_end = jnp.cumsum(nblk)
    blk_start = blk_end - nblk
    slot0 = blk_start * tmb
    e_idx = route[:, ROUTE_E1:ROUTE_E1 + top_k].astype(jnp.int32)
    rank = route[:, ROUTE_R1:ROUTE_R1 + top_k].astype(jnp.int32)
    onehot = e_idx[:, :, None] == jnp.arange(n_experts)[None, None, :]
    pos = jnp.sum(jnp.where(onehot, slot0[None, None, :], 0), axis=-1) + rank
    j = jnp.arange(n_blocks)
    used = j < blk_end[-1]
    blk_e = jnp.minimum(jnp.sum(j[:, None] >= blk_end[None, :], axis=1), n_experts - 1)
    last_e = jnp.max(jnp.where(nblk > 0, jnp.arange(n_experts), 0))
    blk_e = jnp.where(used, blk_e, last_e).astype(jnp.int32)
    blk_n = jnp.where(used, jnp.clip(cnt[blk_e] - (j - blk_start[blk_e]) * tmb, 0, tmb), 0).astype(jnp.int32)
    t = route.shape[0]
    tok = jnp.arange(t)[:, None]
    is_pad = tok >= t_real
    scratch = n_blocks * tmb + (tok - t_real) * top_k + jnp.arange(top_k)[None, :]
    pos_out = jnp.where(is_pad, scratch, pos).reshape(-1).astype(jnp.int32)
    pos_in = jnp.where(is_pad, 0, pos).reshape(-1).astype(jnp.int32)
    return pos_out, pos_in, blk_e, blk_n


class _Group:
    def __init__(self, row0, nb, nc, c, nc_grid=None):
        self.row0, self.nb, self.nc, self.c = row0, nb, nc, c
        self.nc_grid = nc if nc_grid is None else nc_grid

    def rows(self, width, col_block):
        base, nc, c = self.row0 // self.c, self.nc, self.c
        return pl.BlockSpec((c, width), lambda b, i: (base + b * nc + i, col_block))

    def chunk_rows(self, width):
        nc, c = self.nc, self.c
        return pl.BlockSpec((c, width), lambda b, i: (jnp.minimum(i, nc - 1), 0))


def _full(shape):
    nd = len(shape)
    return pl.BlockSpec(tuple(shape), lambda b, i: (0,) * nd)


def _per_stream(shape):
    nd = len(shape)
    return pl.BlockSpec((None,) + tuple(shape[1:]), lambda b, i: (b,) + (0,) * (nd - 1))


def _scan_call(kernel, group, row_inputs, const_inputs, stream_inputs, y_prev, t_pad, width, extra_out_shapes,
               extra_out_specs, scratch_shapes):
    arrays, specs = [], []
    for arr, w, cb in row_inputs:
        arrays.append(arr)
        specs.append(group.rows(w, cb))
    for item in const_inputs:
        if isinstance(item, tuple):
            arr, spec = item
        else:
            arr, spec = item, _full(item.shape)
        arrays.append(arr)
        specs.append(spec)
    for arr in stream_inputs:
        arrays.append(arr)
        specs.append(_per_stream(arr.shape))
    aliases = {}
    if y_prev is not None:
        aliases = {len(arrays): 0}
        arrays.append(y_prev)
        specs.append(pl.BlockSpec(memory_space=pl.ANY))
    out_shape = [jax.ShapeDtypeStruct((t_pad, width), F32)] + list(extra_out_shapes)
    out_specs = [group.rows(width, 0)] + list(extra_out_specs)
    return pl.pallas_call(
        functools.partial(kernel, c=group.c, nc=group.nc, has_prev=y_prev is not None),
        grid=(group.nb, group.nc_grid),
        in_specs=specs,
        out_specs=out_specs,
        out_shape=out_shape,
        scratch_shapes=scratch_shapes,
        input_output_aliases=aliases,
        compiler_params=pltpu.CompilerParams(dimension_semantics=("parallel", "arbitrary"),
                                             vmem_limit_bytes=VMEM_LIMIT),
    )(*arrays)


def _ssd_kernel(*refs, c, nc, has_prev):
    (z_ref, x_ref, bc_ref, dt_ref, convw_ref, convb_ref, dtb_ref, alog_ref, dskip_ref, nw_ref, expand_ref,
     hist0_ref, s0_ref) = refs[:13]
    y_ref, s_ref, hout_ref, hist_scr = refs[13 + int(has_prev):]
    i = pl.program_id(1)
    d_inner = x_ref.shape[1]
    n_state = SSD_STATE
    n_pairs = d_inner // LANES
    pairs_per_group = n_pairs // SSD_GROUPS
    n_hist = SSD_CONV - 1

    @pl.when(i == 0)
    def _():
        s_ref[...] = s0_ref[...]
        hist_scr[HIST_ROWS - n_hist:HIST_ROWS, :] = hist0_ref[...]

    @pl.when(i >= nc)
    def _():
        y_ref[...] = jnp.zeros_like(y_ref)

    @pl.when(i < nc)
    def _():
        hist_scr[HIST_ROWS:HIST_ROWS + c, 0:d_inner] = x_ref[...].astype(F32)
        hist_scr[HIST_ROWS:HIST_ROWS + c, d_inner:] = bc_ref[...].astype(F32)
        conv = convb_ref[...]
        for j in range(SSD_CONV):
            conv = conv + hist_scr[HIST_ROWS - n_hist + j:HIST_ROWS - n_hist + j + c, :] * convw_ref[j:j + 1, :]
        tail = hist_scr[HIST_ROWS + c - n_hist:HIST_ROWS + c, :]
        hist_scr[HIST_ROWS - n_hist:HIST_ROWS, :] = tail
        hout_ref[...] = tail
        xbc = _silu(conv)
        xs = xbc[:, 0:d_inner]
        bm = xbc[:, d_inner:d_inner + SSD_GROUPS * n_state]
        cm = xbc[:, d_inner + SSD_GROUPS * n_state:]

        x_dt = dt_ref[...] + dtb_ref[...]
        dt = jnp.maximum(x_dt, 0.0) + jnp.log1p(jnp.exp(-jnp.abs(x_dt)))
        log_a = dt * (-jnp.exp(alog_ref[...]))
        cum = _cumsum_time(log_a)
        expand = expand_ref[...]
        dt_e = _exact_ldot(dt, expand)
        cum_e = _exact_ldot(cum, expand)
        last_e = cum_e[c - 1:c, :]
        xdt = xs * dt_e
        wx = jnp.exp(last_e - cum_e) * xdt
        ecum = jnp.exp(cum_e)
        sdecay = jnp.exp(last_e)

        row = lax.broadcasted_iota(jnp.int32, (c, 2 * c), 0)
        col = lax.broadcasted_iota(jnp.int32, (c, 2 * c), 1)
        second = col >= c
        tcol = jnp.where(second, col - c, col)
        causal2 = tcol <= row
        diag2 = tcol == row
        r2 = lax.broadcasted_iota(jnp.int32, (2 * c, LANES), 0)
        l2 = lax.broadcasted_iota(jnp.int32, (2 * c, LANES), 1)
        half2 = (r2 >= c) == (l2 >= SSD_HEADDIM)

        ys = []
        for g in range(SSD_GROUPS):
            cg = cm[:, g * n_state:(g + 1) * n_state].astype(BF16)
            bg = bm[:, g * n_state:(g + 1) * n_state].astype(BF16)
            scores2 = _dot_nt(cg, jnp.concatenate([bg, bg], axis=0))
            for p in range(g * pairs_per_group, (g + 1) * pairs_per_group):
                sl = slice(p * LANES, (p + 1) * LANES)
                cum_p = cum_e[:, sl]
                cum_col = jnp.where(second, cum_p[:, SSD_HEADDIM:SSD_HEADDIM + 1], cum_p[:, 0:1])
                cum_row = jnp.sum(jnp.where(diag2, cum_col, 0.0), axis=0, keepdims=True)
                seg = cum_col - cum_row
                m2 = (scores2 * jnp.exp(jnp.where(causal2, seg, NEG_BIG))).astype(BF16)
                xp = xdt[:, sl]
                x2 = jnp.where(half2, jnp.concatenate([xp, xp], axis=0), 0.0).astype(BF16)
                y = _dot(m2, x2)
                y = y + _dot(cg, s_ref[p].astype(BF16)) * ecum[:, sl]
                s_ref[p] = s_ref[p] * sdecay[:, sl] + _dot_tn(bg, wx[:, sl].astype(BF16))
                ys.append(y)
        y = jnp.concatenate(ys, axis=1)
        y = y + xs * dskip_ref[...]
        y = y * _silu(z_ref[...].astype(F32))
        gw = d_inner // SSD_GROUPS
        y = jnp.concatenate([_rms(y[:, g * gw:(g + 1) * gw]) for g in range(SSD_GROUPS)], axis=1)
        y_ref[...] = y * nw_ref[...]


def _ret_kernel(*refs, c, nc, has_prev):
    (q_ref, k_ref, v_ref, g_ref, sin_ref, cos_ref, dmat_ref, ecum_ref, wend_ref, sdec_ref, s0_ref) = refs[:11]
    y_ref, s_ref = refs[11 + int(has_prev):]
    i = pl.program_id(1)
    n_heads = q_ref.shape[1] // LANES

    @pl.when(i == 0)
    def _():
        s_ref[...] = s0_ref[...]

    @pl.when(i >= nc)
    def _():
        y_ref[...] = jnp.zeros_like(y_ref)

    @pl.when(i < nc)
    def _():
        sin = sin_ref[...]
        cos = cos_ref[...]
        even = (lax.broadcasted_iota(jnp.int32, (c, LANES), 1) % 2) == 0

        def rotate(x):
            nxt = pltpu.roll(x, LANES - 1, 1)
            prv = pltpu.roll(x, 1, 1)
            return x * cos + jnp.where(even, -nxt, prv) * sin

        scale = LANES ** -0.5
        ys = []
        for h in range(n_heads):
            sl = slice(h * LANES, (h + 1) * LANES)
            qh = rotate(q_ref[:, sl].astype(F32))
            kh = rotate(k_ref[:, sl].astype(F32)) * scale
            vh = v_ref[:, sl]
            scores = _dot_nt(qh.astype(BF16), kh.astype(BF16)) * dmat_ref[h]
            y = _dot(scores.astype(BF16), vh) + _dot((qh * ecum_ref[:, sl]).astype(BF16), s_ref[h].astype(BF16))
            s_ref[h] = s_ref[h] * sdec_ref[:, sl] + _dot_tn((kh * wend_ref[:, sl]).astype(BF16), vh)
            ys.append(_rms(y) * _silu(g_ref[:, sl].astype(F32)))
        y_ref[...] = jnp.concatenate(ys, axis=1)


def _hgrn_kernel(*refs, c, nc, has_prev):
    (q_ref, f_ref, v_ref, g_ref, lb_ref, nw_ref, s0_ref) = refs[:7]
    y_ref, s_ref = refs[7 + int(has_prev):]
    i = pl.program_id(1)
    n_heads = q_ref.shape[1] // LANES

    @pl.when(i == 0)
    def _():
        s_ref[...] = s0_ref[...]

    @pl.when(i >= nc)
    def _():
        y_ref[...] = jnp.zeros_like(y_ref)

    @pl.when(i < nc)
    def _():
        lb = lb_ref[...]
        forget = lb + (1.0 - lb) * _sigmoid(f_ref[...])
        kk = 1.0 - forget
        gc = _cumsum_time(jnp.log(forget))
        last = gc[c - 1:c, :]
        qg = (_silu(q_ref[...].astype(F32)) * jnp.exp(gc)).astype(BF16)
        kg = (kk * jnp.exp(-gc)).astype(BF16)
        kw = (kk * jnp.exp(last - gc)).astype(BF16)
        sdec = jnp.exp(last)
        causal = _causal(c)
        ys = []
        for h in range(n_heads):
            sl = slice(h * LANES, (h + 1) * LANES)
            vh = v_ref[:, sl]
            scores = jnp.where(causal, _dot_nt(qg[:, sl], kg[:, sl]), 0.0)
            y = _dot(scores.astype(BF16), vh) + _dot_nt(qg[:, sl], s_ref[h].astype(BF16))
            s_ref[h] = s_ref[h] * sdec[:, sl] + _dot_tn(vh, kw[:, sl])
            ys.append(_rms(y) * nw_ref[...] * _silu(g_ref[:, sl].astype(F32)))
        y_ref[...] = jnp.concatenate(ys, axis=1)


def _rotation_tables(pos, dk):
    inv = 1.0 / (ROPE_BASE ** jnp.linspace(0.0, 1.0, dk // 2, dtype=F32))
    ang = pos.astype(F32)[:, None] * jnp.repeat(inv, 2)[None, :]
    return jnp.sin(ang), jnp.cos(ang)


def _retention_decay(c, n_heads, dv):
    log_gamma = jnp.log1p(-(2.0 ** (-5.0 - jnp.arange(n_heads, dtype=F32))))
    cum = jnp.cumsum(jnp.broadcast_to(log_gamma, (c, n_heads)), axis=0)
    cum_h = cum.T
    causal = jnp.tril(jnp.ones((c, c), dtype=bool))
    dmat = jnp.exp(jnp.where(causal, cum_h[:, :, None] - cum_h[:, None, :], -jnp.inf))
    ecum = jnp.repeat(jnp.exp(cum), dv, axis=1)
    wend = jnp.repeat(jnp.exp(cum[-1][None, :] - cum), dv, axis=1)
    sdec = jnp.repeat(jnp.exp(cum[-1])[None, :], dv, axis=1)
    return dmat, ecum, wend, sdec


def _pair_heads(s):
    nb, h, n, p = s.shape
    return s.reshape(nb, h // 2, 2, n, p).transpose(0, 1, 3, 2, 4).reshape(nb, h // 2, n, 2 * p)


def _unpair_heads(s, p):
    nb, hp, n, _ = s.shape
    return s.reshape(nb, hp, n, 2, p).transpose(0, 1, 3, 2, 4).reshape(nb, 2 * hp, n, p)


def kernel(x_prompt, x_sample, state_ssd, state_ssd_conv, state_ret, state_hgrn, meta_tokens, norm_mix, norm_ffn,
           norm_final, w_in_ab, conv_w, conv_b, dt_bias, a_log, d_skip, ssd_norm, w_out_ab, w_ffn_gate, w_ffn_up,
           w_ffn_down, w_in_c, hgrn_lb, hgrn_norm, w_out_c, w_router, w_exp_gate, w_exp_up, w_exp_down):
    bp, seq, d = x_prompt.shape
    bs, dec_seq, _ = x_sample.shape
    n_meta = meta_tokens.shape[0]
    depth = norm_mix.shape[0]
    assert depth == 2 and seq % CHUNK == 0 and n_meta == N_META and d % LANES == 0
    n_ssd_heads = d_skip.shape[1]
    d_inner = n_ssd_heads * SSD_HEADDIM
    bc_w = 2 * SSD_GROUPS * SSD_STATE
    conv_dim = d_inner + bc_w
    ret_w = RET_HEADS * LANES
    assert d_inner == d and ret_w == d and conv_dim == conv_w.shape[2]

    tp, ts = bp * seq, bs * dec_seq
    t_real = tp + ts + n_meta
    tm = 512 if t_real >= 4096 else 128
    t_pad = -(-t_real // tm) * tm
    tm_big = 2 * tm if t_pad % (2 * tm) == 0 else tm
    g_prompt = _Group(0, bp, seq // CHUNK, CHUNK)
    g_sample = _Group(tp, bs, 1, dec_seq)
    g_meta = _Group(tp + ts, 1, 1, n_meta, nc_grid=(t_pad - tp - ts) // n_meta)
    assert tp % dec_seq == 0 and (tp + ts) % n_meta == 0 and (t_pad - tp - ts) % n_meta == 0

    x0 = jnp.concatenate([x_prompt.reshape(tp, d), x_sample.reshape(ts, d), meta_tokens,
                          jnp.zeros((t_pad - t_real, d), F32)], axis=0)

    w_in = w_in_ab[0]
    o_z, o_xbc, o_dt, o_q = 0, d_inner, d_inner + conv_dim, d_inner + conv_dim + n_ssd_heads
    n_cols = 6 * d + bc_w
    tn = n_cols // 4 if n_cols % (4 * LANES) == 0 else n_cols
    w_perm = jnp.concatenate([
        w_in[:, o_z:o_z + d_inner], w_in[:, o_q:o_q + 4 * ret_w], w_in[:, o_xbc:o_xbc + conv_dim]],
        axis=1).astype(BF16)
    w_dt = jnp.pad(w_in[:, o_dt:o_dt + n_ssd_heads], ((0, 0), (0, LANES - n_ssd_heads))).astype(BF16)
    proj, proj_dt = _normmm(x0, norm_mix[0], w_perm, w_dt, tm_big, tn)
    cb_x, cb_bc = 5, (6 * d) // bc_w
    assert (6 * d) % bc_w == 0

    expand = (jnp.arange(LANES)[:, None] == (jnp.arange(d_inner) // SSD_HEADDIM)[None, :]).astype(BF16)
    pad_h = LANES - n_ssd_heads
    ssd_consts = [conv_w[0], conv_b[0].reshape(1, conv_dim), jnp.pad(dt_bias[0], (0, pad_h)).reshape(1, LANES),
                  jnp.pad(a_log[0], (0, pad_h)).reshape(1, LANES),
                  jnp.repeat(d_skip[0], SSD_HEADDIM).reshape(1, d_inner), ssd_norm[0].reshape(1, d_inner), expand]
    n_pairs = n_ssd_heads // 2

    def ssd(group, hist0, s0, y_prev):
        nb = group.nb
        return _scan_call(
            _ssd_kernel, group,
            [(proj, d_inner, 0), (proj, d_inner, cb_x), (proj, bc_w, cb_bc), (proj_dt, LANES, 0)],
            ssd_consts, [hist0, _pair_heads(s0)], y_prev, t_pad, d_inner,
            [jax.ShapeDtypeStruct((nb, n_pairs, SSD_STATE, LANES), F32),
             jax.ShapeDtypeStruct((nb, SSD_CONV - 1, conv_dim), F32)],
            [_per_stream((nb, n_pairs, SSD_STATE, LANES)), _per_stream((nb, SSD_CONV - 1, conv_dim))],
            [pltpu.VMEM((HIST_ROWS + group.c, conv_dim), F32)])

    def bcast(s, n):
        return jnp.broadcast_to(s, (n,) + s.shape[1:])

    zeros = jnp.zeros
    y_ssd, s_m, h_m = ssd(g_meta, zeros((1, SSD_CONV - 1, conv_dim), F32),
                          zeros((1, n_ssd_heads, SSD_STATE, SSD_HEADDIM), F32), None)
    y_ssd, ssd_p, conv_p = ssd(g_prompt, bcast(h_m, bp), bcast(_unpair_heads(s_m, SSD_HEADDIM), bp), y_ssd)
    y_ssd, ssd_s, conv_s = ssd(g_sample, state_ssd_conv[0], state_ssd[0], y_ssd)
    ssd_p = _unpair_heads(ssd_p, SSD_HEADDIM)
    ssd_s = _unpair_heads(ssd_s, SSD_HEADDIM)

    def ret(group, pos, s0, y_prev):
        nb = group.nb
        sin, cos = _rotation_tables(pos, LANES)
        dmat, ecum, wend, sdec = _retention_decay(group.c, RET_HEADS, LANES)
        return _scan_call(
            _ret_kernel, group,
            [(proj, ret_w, 1), (proj, ret_w, 2), (proj, ret_w, 3), (proj, ret_w, 4)],
            [(sin, group.chunk_rows(LANES)), (cos, group.chunk_rows(LANES)), dmat, ecum, wend, sdec],
            [s0], y_prev, t_pad, ret_w,
            [jax.ShapeDtypeStruct((nb, RET_HEADS, LANES, LANES), F32)],
            [_per_stream((nb, RET_HEADS, LANES, LANES))], [])

    y_ret, r_m = ret(g_meta, jnp.arange(n_meta), zeros((1, RET_HEADS, LANES, LANES), F32), None)
    y_ret, ret_p = ret(g_prompt, n_meta + jnp.arange(seq), bcast(r_m, bp), y_ret)
    y_ret, ret_s = ret(g_sample, n_meta + PAST_LEN + jnp.arange(dec_seq), state_ret[0], y_ret)

    w_out = w_out_ab[0].astype(BF16)
    x1 = _outproj(x0, [y_ssd, y_ret], [w_out[:d_inner], w_out[d_inner:]], tm)
    ff = w_ffn_gate.shape[2]
    tf = 256 if ff % 256 == 0 else ff
    x2 = _ffn(x1, norm_ffn[0], w_ffn_gate[0].astype(BF16), w_ffn_up[0].astype(BF16), w_ffn_down[0].astype(BF16),
              tm_big, tf)

    w_c = w_in_c[0].astype(BF16)
    w_qig = jnp.concatenate([w_c[:, :d], w_c[:, 2 * d:]], axis=1)
    proj_c, proj_f = _normmm(x2, norm_mix[1], w_qig, w_c[:, d:2 * d], tm_big, (3 * d) // 2)
    lb_soft = jax.nn.softmax(hgrn_lb.astype(F32), axis=0)
    lb = (jnp.cumsum(lb_soft, axis=0) - lb_soft[0])[1].reshape(1, d)
    n_hg = d // HG_DK
    hg_consts = [lb, hgrn_norm[0].reshape(1, LANES)]

    def hgrn(group, s0, y_prev):
        nb = group.nb
        return _scan_call(
            _hgrn_kernel, group,
            [(proj_c, d, 0), (proj_f, d, 0), (proj_c, d, 1), (proj_c, d, 2)],
            hg_consts, [jnp.swapaxes(s0, 2, 3)], y_prev, t_pad, d,
            [jax.ShapeDtypeStruct((nb, n_hg, LANES, HG_DK), F32)],
            [_per_stream((nb, n_hg, LANES, HG_DK))], [])

    y_hg, g_m = hgrn(g_meta, zeros((1, n_hg, HG_DK, LANES), F32), None)
    y_hg, hg_p = hgrn(g_prompt, bcast(jnp.swapaxes(g_m, 2, 3), bp), y_hg)
    y_hg, hg_s = hgrn(g_sample, state_hgrn[0], y_hg)
    hg_p = jnp.swapaxes(hg_p, 2, 3)
    hg_s = jnp.swapaxes(hg_s, 2, 3)

    x3 = _outproj(x2, [y_hg], [w_out_c[0].astype(BF16)], tm)
    n_exp = w_router.shape[2]
    route, counts = _router(x3, norm_ffn[1], w_router[0], tm, t_real)
    tmb = tm
    n_blocks = (TOP_K * t_real + n_exp * (tmb - 1)) // tmb
    pos_out, pos_in, blk_e, blk_n = _moe_plan(route, counts, n_exp, tmb, n_blocks, TOP_K, t_real)
    xs = _dispatch(pos_out, x3, n_blocks * tmb + TOP_K * (t_pad - t_real), tm, TOP_K)
    ffe = w_exp_gate.shape[3]
    tfe = 512 if ffe % 512 == 0 else ffe
    ys = _experts(blk_e, blk_n, xs, norm_ffn[1], w_exp_gate[0].astype(BF16), w_exp_up[0].astype(BF16),
                  w_exp_down[0].astype(BF16), tmb, tfe)
    y = _combine(pos_in, x3, route, norm_final, ys, tm, TOP_K)

    y_prompt = y[:tp].reshape(bp, seq, d)
    y_sample = y[tp:tp + ts].reshape(bs, dec_seq, d)
    return (y_prompt, y_sample, ssd_p[None], conv_p[None], ret_p[None], hg_p[None],
            ssd_s[None], conv_s[None], ret_s[None], hg_s[None])
```

```python
import functools

import jax
import jax.numpy as jnp
from jax import lax
from jax.experimental import pallas as pl
from jax.experimental.pallas import tpu as pltpu

F32 = jnp.float32
BF16 = jnp.bfloat16

CHUNK = 64
N_META = 16
PAST_LEN = 2048
EPS = 1e-6
SSD_HEADDIM = 64
SSD_GROUPS = 2
SSD_STATE = 128
SSD_CONV = 4
RET_HEADS = 8
ROPE_BASE = 10000.0
HG_DK = 128
TOP_K = 2

LANES = 128
SUBLANES = 8
HIST_ROWS = 8
VMEM_LIMIT = 56 * 1024 * 1024
NEG_BIG = -1e30


def _dot(a, b):
    return jnp.dot(a, b, preferred_element_type=F32)


def _dot_nt(a, b):
    return lax.dot_general(a, b, (((1,), (1,)), ((), ())), preferred_element_type=F32)


def _dot_tn(a, b):
    return lax.dot_general(a, b, (((0,), (0,)), ((), ())), preferred_element_type=F32)


def _split3(x):
    hi = x.astype(BF16)
    r = x - hi.astype(F32)
    mid = r.astype(BF16)
    lo = (r - mid.astype(F32)).astype(BF16)
    return hi, mid, lo


def _exact_ldot(x, m01):
    hi, mid, lo = _split3(x)
    return _dot(hi, m01) + _dot(mid, m01) + _dot(lo, m01)


def _exact_rdot(m01, x):
    hi, mid, lo = _split3(x)
    return _dot(m01, hi) + _dot(m01, mid) + _dot(m01, lo)


def _causal(c):
    row = lax.broadcasted_iota(jnp.int32, (c, c), 0)
    col = lax.broadcasted_iota(jnp.int32, (c, c), 1)
    return row >= col


def _cumsum_time(x):
    c = x.shape[0]
    return _exact_rdot(jnp.where(_causal(c), 1.0, 0.0).astype(BF16), x)


def _sigmoid(x):
    return 1.0 / (1.0 + jnp.exp(-x))


def _silu(x):
    return x * _sigmoid(x)


def _rms(x):
    return x * lax.rsqrt(jnp.mean(x * x, axis=-1, keepdims=True) + EPS)


def _normmm_kernel(x_ref, nw_ref, w_ref, ws_ref, o_ref, os_ref, xn_scr):
    @pl.when(pl.program_id(1) == 0)
    def _():
        xn = (_rms(x_ref[...]) * nw_ref[...]).astype(BF16)
        xn_scr[...] = xn
        os_ref[...] = _dot(xn, ws_ref[...])

    o_ref[...] = _dot(xn_scr[...], w_ref[...]).astype(BF16)


def _normmm(x, nw, w, w_side, tm, tn):
    t, k = x.shape
    n, ns = w.shape[1], w_side.shape[1]
    return pl.pallas_call(
        _normmm_kernel,
        grid=(t // tm, n // tn),
        in_specs=[pl.BlockSpec((tm, k), lambda i, j: (i, 0)),
                  pl.BlockSpec((1, k), lambda i, j: (0, 0)),
                  pl.BlockSpec((k, tn), lambda i, j: (0, j)),
                  pl.BlockSpec((k, ns), lambda i, j: (0, 0))],
        out_specs=[pl.BlockSpec((tm, tn), lambda i, j: (i, j)), pl.BlockSpec((tm, ns), lambda i, j: (i, 0))],
        out_shape=[jax.ShapeDtypeStruct((t, n), BF16), jax.ShapeDtypeStruct((t, ns), F32)],
        scratch_shapes=[pltpu.VMEM((tm, k), BF16)],
        compiler_params=pltpu.CompilerParams(dimension_semantics=("parallel", "arbitrary"),
                                             vmem_limit_bytes=VMEM_LIMIT),
    )(x, nw.reshape(1, k), w, w_side)


def _ffn_kernel(*refs, n_in):
    x_ref = refs[0]
    ys = refs[1:1 + n_in]
    ws = refs[1 + n_in:1 + 2 * n_in]
    nw_ref, wg_ref, wu_ref, wd_ref, o_ref, xn_scr, acc_scr = refs[1 + 2 * n_in:]
    f = pl.program_id(1)

    @pl.when(f == 0)
    def _():
        x = x_ref[...]
        for y, w in zip(ys, ws):
            x = x + _dot(y[...], w[...])
        xn_scr[...] = (_rms(x) * nw_ref[...]).astype(BF16)
        acc_scr[...] = x

    xn = xn_scr[...]
    h = _silu(_dot(xn, wg_ref[...])) * _dot(xn, wu_ref[...])
    acc_scr[...] += _dot(h.astype(BF16), wd_ref[...])

    @pl.when(f == pl.num_programs(1) - 1)
    def _():
        o_ref[...] = acc_scr[...]


def _ffn(x, ys, ws, nw, wg, wu, wd, tm, tf):
    t, d = x.shape
    ff = wg.shape[1]
    n_in = len(ys)
    in_specs = [pl.BlockSpec((tm, d), lambda i, f: (i, 0))]
    in_specs += [pl.BlockSpec((tm, y.shape[1]), lambda i, f: (i, 0)) for y in ys]
    in_specs += [pl.BlockSpec(w.shape, lambda i, f: (0, 0)) for w in ws]
    in_specs += [pl.BlockSpec((1, d), lambda i, f: (0, 0)),
                 pl.BlockSpec((d, tf), lambda i, f: (0, f)),
                 pl.BlockSpec((d, tf), lambda i, f: (0, f)),
                 pl.BlockSpec((tf, d), lambda i, f: (f, 0))]
    return pl.pallas_call(
        functools.partial(_ffn_kernel, n_in=n_in),
        grid=(t // tm, ff // tf),
        in_specs=in_specs,
        out_specs=pl.BlockSpec((tm, d), lambda i, f: (i, 0)),
        out_shape=jax.ShapeDtypeStruct((t, d), F32),
        scratch_shapes=[pltpu.VMEM((tm, d), BF16), pltpu.VMEM((tm, d), F32)],
        compiler_params=pltpu.CompilerParams(dimension_semantics=("parallel", "arbitrary"),
                                             vmem_limit_bytes=VMEM_LIMIT),
    )(x, *ys, *ws, nw.reshape(1, d), wg, wu, wd)


ROUTE_E1, ROUTE_E2, ROUTE_R1, ROUTE_R2, ROUTE_G1, ROUTE_G2 = range(6)


def _router_kernel(x_ref, y_ref, wo_ref, nw_ref, whi_ref, wlo_ref, xo_ref, route_ref, count_ref, *, n_experts,
                   t_real):
    i = pl.program_id(0)
    tm = x_ref.shape[0]

    @pl.when(i == 0)
    def _():
        count_ref[...] = jnp.zeros_like(count_ref)

    xo = x_ref[...] + _dot(y_ref[...], wo_ref[...])
    xo_ref[...] = xo
    xn = _rms(xo) * nw_ref[...]
    hi = xn.astype(BF16)
    lo = (xn - hi.astype(F32)).astype(BF16)
    logits = _dot(hi, whi_ref[...]) + _dot(lo, whi_ref[...]) + _dot(hi, wlo_ref[...])
    lane = lax.broadcasted_iota(jnp.int32, logits.shape, 1)
    lane_f = lane.astype(F32)
    logits = jnp.where(lane < n_experts, logits, NEG_BIG)
    m1 = jnp.max(logits, axis=-1, keepdims=True)
    i1 = jnp.min(jnp.where(logits == m1, lane_f, float(LANES)), axis=-1, keepdims=True)
    rest = jnp.where(lane_f == i1, NEG_BIG, logits)
    m2 = jnp.max(rest, axis=-1, keepdims=True)
    i2 = jnp.min(jnp.where(rest == m2, lane_f, float(LANES)), axis=-1, keepdims=True)
    e2 = jnp.exp(m2 - m1)
    g1 = 1.0 / (1.0 + e2)
    g2 = e2 / (1.0 + e2)

    row = lax.broadcasted_iota(jnp.int32, (tm, 1), 0) + i * tm
    valid = row < t_real
    sel = jnp.where(valid & ((lane_f == i1) | (lane_f == i2)), 1.0, 0.0)
    r = lax.broadcasted_iota(jnp.int32, (tm, tm), 0)
    q = lax.broadcasted_iota(jnp.int32, (tm, tm), 1)
    before = jnp.where(q < r, 1.0, 0.0).astype(BF16)
    rank = count_ref[...] + _dot(before, sel.astype(BF16))
    r1 = jnp.sum(jnp.where(lane_f == i1, rank, 0.0), axis=-1, keepdims=True)
    r2 = jnp.sum(jnp.where(lane_f == i2, rank, 0.0), axis=-1, keepdims=True)
    count_ref[...] += jnp.sum(sel, axis=0, keepdims=True)

    rec = jnp.zeros_like(logits)
    for k, v in ((ROUTE_E1, i1), (ROUTE_E2, i2), (ROUTE_R1, r1), (ROUTE_R2, r2), (ROUTE_G1, g1), (ROUTE_G2, g2)):
        rec = jnp.where(lane == k, v, rec)
    route_ref[...] = jnp.where(valid, rec, 0.0)


def _router(x, y, wo, nw, w_router, tm, t_real):
    t, d = x.shape
    e = w_router.shape[1]
    wpad = jnp.zeros((d, LANES), F32).at[:, :e].set(w_router)
    whi = wpad.astype(BF16)
    wlo = (wpad - whi.astype(F32)).astype(BF16)
    return pl.pallas_call(
        functools.partial(_router_kernel, n_experts=e, t_real=t_real),
        grid=(t // tm,),
        in_specs=[pl.BlockSpec((tm, d), lambda i: (i, 0)),
                  pl.BlockSpec((tm, y.shape[1]), lambda i: (i, 0)),
                  pl.BlockSpec(wo.shape, lambda i: (0, 0)),
                  pl.BlockSpec((1, d), lambda i: (0, 0)),
                  pl.BlockSpec((d, LANES), lambda i: (0, 0)),
                  pl.BlockSpec((d, LANES), lambda i: (0, 0))],
        out_specs=[pl.BlockSpec((tm, d), lambda i: (i, 0)), pl.BlockSpec((tm, LANES), lambda i: (i, 0)),
                   pl.BlockSpec((1, LANES), lambda i: (0, 0))],
        out_shape=[jax.ShapeDtypeStruct((t, d), F32), jax.ShapeDtypeStruct((t, LANES), F32),
                   jax.ShapeDtypeStruct((1, LANES), F32)],
        compiler_params=pltpu.CompilerParams(dimension_semantics=("arbitrary",),
                                             vmem_limit_bytes=VMEM_LIMIT),
    )(x, y, wo, nw.reshape(1, d), whi, wlo)


def _row_copy(src, src_row, dst, dst_row, sem):
    return pltpu.make_async_copy(src.at[pl.ds(src_row, 1)], dst.at[pl.ds(dst_row, 1)], sem)


def _dispatch_kernel(pos_ref, x_ref, xs_hbm, sem, *, tm, top_k):
    i = pl.program_id(0)

    def start(g, carry):
        r0 = pl.multiple_of(g * SUBLANES, SUBLANES)
        for b in range(SUBLANES):
            for k in range(top_k):
                _row_copy(x_ref, r0 + b, xs_hbm, pos_ref[(i * tm + r0 + b) * top_k + k], sem).start()
        return carry

    def wait(g, carry):
        for _ in range(SUBLANES * top_k):
            _row_copy(x_ref, 0, xs_hbm, 0, sem).wait()
        return carry

    lax.fori_loop(0, tm // SUBLANES, start, 0)
    lax.fori_loop(0, tm // SUBLANES, wait, 0)


def _dispatch(pos, x, n_rows_out, tm, top_k):
    t, d = x.shape
    return pl.pallas_call(
        functools.partial(_dispatch_kernel, tm=tm, top_k=top_k),
        grid_spec=pltpu.PrefetchScalarGridSpec(
            num_scalar_prefetch=1, grid=(t // tm,),
            in_specs=[pl.BlockSpec((tm, d), lambda i, p: (i, 0))],
            out_specs=pl.BlockSpec(memory_space=pl.ANY),
            scratch_shapes=[pltpu.SemaphoreType.DMA(())]),
        out_shape=jax.ShapeDtypeStruct((n_rows_out, d), F32),
        compiler_params=pltpu.CompilerParams(dimension_semantics=("arbitrary",),
                                             vmem_limit_bytes=VMEM_LIMIT),
    )(pos, x)


def _expert_kernel(blk_e_ref, blk_n_ref, xs_ref, nw_ref, wg_ref, wu_ref, wd_ref, o_ref, xn_scr, acc_scr):
    del blk_e_ref
    j = pl.program_id(0)
    f = pl.program_id(1)
    n_valid = blk_n_ref[j]
    last = f == pl.num_programs(1) - 1

    @pl.when((n_valid > 0) & (f == 0))
    def _():
        row = lax.broadcasted_iota(jnp.int32, (xs_ref.shape[0], 1), 0)
        x = jnp.where(row < n_valid, xs_ref[...], 0.0)
        xn_scr[...] = (_rms(x) * nw_ref[...]).astype(BF16)
        acc_scr[...] = jnp.zeros_like(acc_scr)

    @pl.when(n_valid > 0)
    def _():
        xn = xn_scr[...]
        h = _silu(_dot(xn, wg_ref[...])) * _dot(xn, wu_ref[...])
        acc_scr[...] += _dot(h.astype(BF16), wd_ref[...])

    @pl.when((n_valid > 0) & last)
    def _():
        o_ref[...] = acc_scr[...]

    @pl.when((n_valid == 0) & last)
    def _():
        o_ref[...] = jnp.zeros_like(o_ref)


def _experts(blk_e, blk_n, xs, nw, wg, wu, wd, tmb, tf):
    d = xs.shape[1]
    s = blk_e.shape[0] * tmb
    ff = wg.shape[2]
    nf = ff // tf

    def w_up(j, f, be, bn):
        return (be[j], 0, jnp.where(bn[j] > 0, f, nf - 1))

    def w_down(j, f, be, bn):
        return (be[j], jnp.where(bn[j] > 0, f, nf - 1), 0)

    return pl.pallas_call(
        _expert_kernel,
        grid_spec=pltpu.PrefetchScalarGridSpec(
            num_scalar_prefetch=2, grid=(s // tmb, nf),
            in_specs=[pl.BlockSpec((tmb, d), lambda j, f, be, bn: (j, 0)),
                      pl.BlockSpec((1, d), lambda j, f, be, bn: (0, 0)),
                      pl.BlockSpec((None, d, tf), w_up),
                      pl.BlockSpec((None, d, tf), w_up),
                      pl.BlockSpec((None, tf, d), w_down)],
            out_specs=pl.BlockSpec((tmb, d), lambda j, f, be, bn: (j, 0)),
            scratch_shapes=[pltpu.VMEM((tmb, d), BF16), pltpu.VMEM((tmb, d), F32)]),
        out_shape=jax.ShapeDtypeStruct((s, d), F32),
        compiler_params=pltpu.CompilerParams(dimension_semantics=("parallel", "arbitrary"),
                                             vmem_limit_bytes=VMEM_LIMIT),
    )(blk_e, blk_n, xs, nw.reshape(1, d), wg, wu, wd)


def _combine_kernel(pos_ref, x_ref, route_ref, nw_ref, ys_hbm, op_ref, os_ref, ybuf, sem, *, tm, top_k,
                    n_prompt_tiles):
    i = pl.program_id(0)

    def start(g, carry):
        r0 = pl.multiple_of(g * SUBLANES, SUBLANES)
        for b in range(SUBLANES):
            for k in range(top_k):
                _row_copy(ys_hbm, pos_ref[(i * tm + r0 + b) * top_k + k], ybuf.at[k], r0 + b, sem).start()
        return carry

    def wait(g, carry):
        for _ in range(SUBLANES * top_k):
            _row_copy(ys_hbm, 0, ybuf.at[0], 0, sem).wait()
        return carry

    lax.fori_loop(0, tm // SUBLANES, start, 0)
    lax.fori_loop(0, tm // SUBLANES, wait, 0)
    route = route_ref[...]
    out = x_ref[...] + route[:, ROUTE_G1:ROUTE_G1 + 1] * ybuf[0] + route[:, ROUTE_G2:ROUTE_G2 + 1] * ybuf[1]
    y = _rms(out) * nw_ref[...]

    @pl.when(i < n_prompt_tiles)
    def _():
        op_ref[...] = y

    @pl.when(i >= n_prompt_tiles)
    def _():
        os_ref[...] = y


def _combine(pos, x, route, nw, ys, tm, top_k, tp, ts):
    d = x.shape[1]
    assert top_k == 2 and tp % tm == 0 and ts % tm == 0
    npt, nst = tp // tm, ts // tm
    return pl.pallas_call(
        functools.partial(_combine_kernel, tm=tm, top_k=top_k, n_prompt_tiles=npt),
        grid_spec=pltpu.PrefetchScalarGridSpec(
            num_scalar_prefetch=1, grid=(npt + nst,),
            in_specs=[pl.BlockSpec((tm, d), lambda i, p: (i, 0)),
                      pl.BlockSpec((tm, LANES), lambda i, p: (i, 0)),
                      pl.BlockSpec((1, d), lambda i, p: (0, 0)),
                      pl.BlockSpec(memory_space=pl.ANY)],
            out_specs=[pl.BlockSpec((tm, d), lambda i, p: (jnp.minimum(i, npt - 1), 0)),
                       pl.BlockSpec((tm, d), lambda i, p: (jnp.maximum(i - npt, 0), 0))],
            scratch_shapes=[pltpu.VMEM((top_k, tm, d), F32), pltpu.SemaphoreType.DMA(())]),
        out_shape=[jax.ShapeDtypeStruct((tp, d), F32), jax.ShapeDtypeStruct((ts, d), F32)],
        compiler_params=pltpu.CompilerParams(dimension_semantics=("arbitrary",),
                                             vmem_limit_bytes=VMEM_LIMIT),
    )(pos, x, route, nw.reshape(1, d), ys)


def _moe_plan(route, counts, n_experts, tmb, n_blocks, top_k, t_real):
    cnt = counts[0, :n_experts].astype(jnp.int32)
    nblk = (cnt + tmb - 1) // tmb
    blk_end = jnp.cumsum(nblk)
    blk_start = blk_end - nblk
    slot0 = blk_start * tmb
    e_idx = route[:, ROUTE_E1:ROUTE_E1 + top_k].astype(jnp.int32)
    rank = route[:, ROUTE_R1:ROUTE_R1 + top_k].astype(jnp.int32)
    onehot = e_idx[:, :, None] == jnp.arange(n_experts)[None, None, :]
    pos = jnp.sum(jnp.where(onehot, slot0[None, None, :], 0), axis=-1) + rank
    j = jnp.arange(n_blocks)
    used = j < blk_end[-1]
    blk_e = jnp.minimum(jnp.sum(j[:, None] >= blk_end[None, :], axis=1), n_experts - 1)
    last_e = jnp.max(jnp.where(nblk > 0, jnp.arange(n_experts), 0))
    blk_e = jnp.where(used, blk_e, last_e).astype(jnp.int32)
    blk_n = jnp.where(used, jnp.clip(cnt[blk_e] - (j - blk_start[blk_e]) * tmb, 0, tmb), 0).astype(jnp.int32)
    t = route.shape[0]
    tok = jnp.arange(t)[:, None]
    is_pad = tok >= t_real
    scratch = n_blocks * tmb + (tok - t_real) * top_k + jnp.arange(top_k)[None, :]
    return jnp.where(is_pad, scratch, pos).reshape(-1).astype(jnp.int32), blk_e, blk_n


class _Group:
    def __init__(self, row0, nb, nc, c, nc_grid=None):
        self.row0, self.nb, self.nc, self.c = row0, nb, nc, c
        self.nc_grid = nc if nc_grid is None else nc_grid

    def rows(self, width, col_block):
        base, nc, c = self.row0 // self.c, self.nc, self.c
        return pl.BlockSpec((c, width), lambda b, i: (base + b * nc + i, col_block))

    def chunk_rows(self, width):
        nc, c = self.nc, self.c
        return pl.BlockSpec((c, width), lambda b, i: (jnp.minimum(i, nc - 1), 0))


def _full(shape):
    nd = len(shape)
    return pl.BlockSpec(tuple(shape), lambda b, i: (0,) * nd)


def _per_stream(shape):
    nd = len(shape)
    return pl.BlockSpec((None,) + tuple(shape[1:]), lambda b, i: (b,) + (0,) * (nd - 1))


def _scan_call(kernel, group, row_inputs, const_inputs, stream_inputs, y_prev, t_pad, width, extra_out_shapes,
               extra_out_specs, scratch_shapes):
    arrays, specs = [], []
    for arr, w, cb in row_inputs:
        arrays.append(arr)
        specs.append(group.rows(w, cb))
    for item in const_inputs:
        if isinstance(item, tuple):
            arr, spec = item
        else:
            arr, spec = item, _full(item.shape)
        arrays.append(arr)
        specs.append(spec)
    for arr in stream_inputs:
        arrays.append(arr)
        specs.append(_per_stream(arr.shape))
    aliases = {}
    if y_prev is not None:
        aliases = {len(arrays): 0}
        arrays.append(y_prev)
        specs.append(pl.BlockSpec(memory_space=pl.ANY))
    out_shape = [jax.ShapeDtypeStruct((t_pad, width), BF16)] + list(extra_out_shapes)
    out_specs = [group.rows(width, 0)] + list(extra_out_specs)
    return pl.pallas_call(
        functools.partial(kernel, c=group.c, nc=group.nc, has_prev=y_prev is not None),
        grid=(group.nb, group.nc_grid),
        in_specs=specs,
        out_specs=out_specs,
        out_shape=out_shape,
        scratch_shapes=scratch_shapes,
        input_output_aliases=aliases,
        compiler_params=pltpu.CompilerParams(dimension_semantics=("parallel", "arbitrary"),
                                             vmem_limit_bytes=VMEM_LIMIT),
    )(*arrays)


def _ssd_kernel(*refs, c, nc, has_prev):
    (z_ref, x_ref, bc_ref, dt_ref, convw_ref, convb_ref, dtb_ref, alog_ref, dskip_ref, nw_ref, expand_ref,
     hist0_ref, s0_ref) = refs[:13]
    y_ref, s_ref, hout_ref, hist_scr = refs[13 + int(has_prev):]
    i = pl.program_id(1)
    d_inner = x_ref.shape[1]
    n_state = SSD_STATE
    n_pairs = d_inner // LANES
    pairs_per_group = n_pairs // SSD_GROUPS
    n_hist = SSD_CONV - 1

    @pl.when(i == 0)
    def _():
        s_ref[...] = s0_ref[...]
        hist_scr[HIST_ROWS - n_hist:HIST_ROWS, :] = hist0_ref[...]

    @pl.when(i >= nc)
    def _():
        y_ref[...] = jnp.zeros_like(y_ref)

    @pl.when(i < nc)
    def _():
        hist_scr[HIST_ROWS:HIST_ROWS + c, 0:d_inner] = x_ref[...].astype(F32)
        hist_scr[HIST_ROWS:HIST_ROWS + c, d_inner:] = bc_ref[...].astype(F32)
        conv = convb_ref[...]
        for j in range(SSD_CONV):
            conv = conv + hist_scr[HIST_ROWS - n_hist + j:HIST_ROWS - n_hist + j + c, :] * convw_ref[j:j + 1, :]
        tail = hist_scr[HIST_ROWS + c - n_hist:HIST_ROWS + c, :]
        hist_scr[HIST_ROWS - n_hist:HIST_ROWS, :] = tail
        hout_ref[...] = tail
        xbc = _silu(conv)
        xs = xbc[:, 0:d_inner]
        bm = xbc[:, d_inner:d_inner + SSD_GROUPS * n_state]
        cm = xbc[:, d_inner + SSD_GROUPS * n_state:]

        x_dt = dt_ref[...] + dtb_ref[...]
        dt = jnp.maximum(x_dt, 0.0) + jnp.log1p(jnp.exp(-jnp.abs(x_dt)))
        log_a = dt * (-jnp.exp(alog_ref[...]))
        cum = _cumsum_time(log_a)
        expand = expand_ref[...]
        dt_e = _exact_ldot(dt, expand)
        cum_e = _exact_ldot(cum, expand)
        last_e = cum_e[c - 1:c, :]
        xdt = xs * dt_e
        wx = jnp.exp(last_e - cum_e) * xdt
        ecum = jnp.exp(cum_e)
        sdecay = jnp.exp(last_e)

        row = lax.broadcasted_iota(jnp.int32, (c, 2 * c), 0)
        col = lax.broadcasted_iota(jnp.int32, (c, 2 * c), 1)
        second = col >= c
        tcol = jnp.where(second, col - c, col)
        causal2 = tcol <= row
        diag2 = tcol == row
        r2 = lax.broadcasted_iota(jnp.int32, (2 * c, LANES), 0)
        l2 = lax.broadcasted_iota(jnp.int32, (2 * c, LANES), 1)
        half2 = (r2 >= c) == (l2 >= SSD_HEADDIM)

        ys = []
        for g in range(SSD_GROUPS):
            cg = cm[:, g * n_state:(g + 1) * n_state].astype(BF16)
            bg = bm[:, g * n_state:(g + 1) * n_state].astype(BF16)
            scores2 = _dot_nt(cg, jnp.concatenate([bg, bg], axis=0))
            for p in range(g * pairs_per_group, (g + 1) * pairs_per_group):
                sl = slice(p * LANES, (p + 1) * LANES)
                cum_p = cum_e[:, sl]
                cum_col = jnp.where(second, cum_p[:, SSD_HEADDIM:SSD_HEADDIM + 1], cum_p[:, 0:1])
                cum_row = jnp.sum(jnp.where(diag2, cum_col, 0.0), axis=0, keepdims=True)
                seg = cum_col - cum_row
                m2 = (scores2 * jnp.exp(jnp.where(causal2, seg, NEG_BIG))).astype(BF16)
                xp = xdt[:, sl]
                x2 = jnp.where(half2, jnp.concatenate([xp, xp], axis=0), 0.0).astype(BF16)
                y = _dot(m2, x2)
                y = y + _dot(cg, s_ref[p].astype(BF16)) * ecum[:, sl]
                s_ref[p] = s_ref[p] * sdecay[:, sl] + _dot_tn(bg, wx[:, sl].astype(BF16))
                ys.append(y)
        y = jnp.concatenate(ys, axis=1)
        y = y + xs * dskip_ref[...]
        y = y * _silu(z_ref[...].astype(F32))
        gw = d_inner // SSD_GROUPS
        y = jnp.concatenate([_rms(y[:, g * gw:(g + 1) * gw]) for g in range(SSD_GROUPS)], axis=1)
        y_ref[...] = (y * nw_ref[...]).astype(BF16)


def _ret_kernel(*refs, c, nc, has_prev):
    (q_ref, k_ref, v_ref, g_ref, sin_ref, cos_ref, dmat_ref, ecum_ref, wend_ref, sdec_ref, s0_ref) = refs[:11]
    y_ref, s_ref = refs[11 + int(has_prev):]
    i = pl.program_id(1)
    n_heads = q_ref.shape[1] // LANES

    @pl.when(i == 0)
    def _():
        s_ref[...] = s0_ref[...]

    @pl.when(i >= nc)
    def _():
        y_ref[...] = jnp.zeros_like(y_ref)

    @pl.when(i < nc)
    def _():
        sin = sin_ref[...]
        cos = cos_ref[...]
        even = (lax.broadcasted_iota(jnp.int32, (c, LANES), 1) % 2) == 0

        def rotate(x):
            nxt = pltpu.roll(x, LANES - 1, 1)
            prv = pltpu.roll(x, 1, 1)
            return x * cos + jnp.where(even, -nxt, prv) * sin

        scale = LANES ** -0.5
        ys = []
        for h in range(n_heads):
            sl = slice(h * LANES, (h + 1) * LANES)
            qh = rotate(q_ref[:, sl].astype(F32))
            kh = rotate(k_ref[:, sl].astype(F32)) * scale
            vh = v_ref[:, sl]
            scores = _dot_nt(qh.astype(BF16), kh.astype(BF16)) * dmat_ref[h]
            y = _dot(scores.astype(BF16), vh) + _dot((qh * ecum_ref[:, sl]).astype(BF16), s_ref[h].astype(BF16))
            s_ref[h] = s_ref[h] * sdec_ref[:, sl] + _dot_tn((kh * wend_ref[:, sl]).astype(BF16), vh)
            ys.append(_rms(y) * _silu(g_ref[:, sl].astype(F32)))
        y_ref[...] = jnp.concatenate(ys, axis=1).astype(BF16)


def _hgrn_kernel(*refs, c, nc, has_prev):
    (q_ref, f_ref, v_ref, g_ref, lb_ref, nw_ref, s0_ref) = refs[:7]
    y_ref, sout_ref, s_ref = refs[7 + int(has_prev):]
    i = pl.program_id(1)
    n_heads = q_ref.shape[1] // LANES

    @pl.when(i == 0)
    def _():
        for h in range(n_heads):
            s_ref[h] = s0_ref[h].T

    @pl.when(i >= nc)
    def _():
        y_ref[...] = jnp.zeros_like(y_ref)

    @pl.when(i < nc)
    def _():
        lb = lb_ref[...]
        forget = lb + (1.0 - lb) * _sigmoid(f_ref[...])
        kk = 1.0 - forget
        gc = _cumsum_time(jnp.log(forget))
        last = gc[c - 1:c, :]
        qg = (_silu(q_ref[...].astype(F32)) * jnp.exp(gc)).astype(BF16)
        kg = (kk * jnp.exp(-gc)).astype(BF16)
        kw = (kk * jnp.exp(last - gc)).astype(BF16)
        sdec = jnp.exp(last)
        causal = _causal(c)
        ys = []
        for h in range(n_heads):
            sl = slice(h * LANES, (h + 1) * LANES)
            vh = v_ref[:, sl]
            scores = jnp.where(causal, _dot_nt(qg[:, sl], kg[:, sl]), 0.0)
            y = _dot(scores.astype(BF16), vh) + _dot_nt(qg[:, sl], s_ref[h].astype(BF16))
            s_ref[h] = s_ref[h] * sdec[:, sl] + _dot_tn(vh, kw[:, sl])
            ys.append(_rms(y) * nw_ref[...] * _silu(g_ref[:, sl].astype(F32)))
        y_ref[...] = jnp.concatenate(ys, axis=1).astype(BF16)

        @pl.when(i == nc - 1)
        def _():
            for h in range(n_heads):
                sout_ref[h] = s_ref[h].T


def _rotation_tables(pos, dk):
    inv = 1.0 / (ROPE_BASE ** jnp.linspace(0.0, 1.0, dk // 2, dtype=F32))
    ang = pos.astype(F32)[:, None] * jnp.repeat(inv, 2)[None, :]
    return jnp.sin(ang), jnp.cos(ang)


def _retention_decay(c, n_heads, dv):
    log_gamma = jnp.log1p(-(2.0 ** (-5.0 - jnp.arange(n_heads, dtype=F32))))
    cum = jnp.cumsum(jnp.broadcast_to(log_gamma, (c, n_heads)), axis=0)
    cum_h = cum.T
    causal = jnp.tril(jnp.ones((c, c), dtype=bool))
    dmat = jnp.exp(jnp.where(causal, cum_h[:, :, None] - cum_h[:, None, :], -jnp.inf))
    ecum = jnp.repeat(jnp.exp(cum), dv, axis=1)
    wend = jnp.repeat(jnp.exp(cum[-1][None, :] - cum), dv, axis=1)
    sdec = jnp.repeat(jnp.exp(cum[-1])[None, :], dv, axis=1)
    return dmat, ecum, wend, sdec


def _pair_heads(s):
    nb, h, n, p = s.shape
    return s.reshape(nb, h // 2, 2, n, p).transpose(0, 1, 3, 2, 4).reshape(nb, h // 2, n, 2 * p)


def _unpair_heads(s, p):
    nb, hp, n, _ = s.shape
    return s.reshape(nb, hp, n, 2, p).transpose(0, 1, 3, 2, 4).reshape(nb, 2 * hp, n, p)


def kernel(x_prompt, x_sample, state_ssd, state_ssd_conv, state_ret, state_hgrn, meta_tokens, norm_mix, norm_ffn,
           norm_final, w_in_ab, conv_w, conv_b, dt_bias, a_log, d_skip, ssd_norm, w_out_ab, w_ffn_gate, w_ffn_up,
           w_ffn_down, w_in_c, hgrn_lb, hgrn_norm, w_out_c, w_router, w_exp_gate, w_exp_up, w_exp_down):
    bp, seq, d = x_prompt.shape
    bs, dec_seq, _ = x_sample.shape
    n_meta = meta_tokens.shape[0]
    depth = norm_mix.shape[0]
    assert depth == 2 and seq % CHUNK == 0 and n_meta == N_META and d % LANES == 0
    n_ssd_heads = d_skip.shape[1]
    d_inner = n_ssd_heads * SSD_HEADDIM
    bc_w = 2 * SSD_GROUPS * SSD_STATE
    conv_dim = d_inner + bc_w
    ret_w = RET_HEADS * LANES
    assert d_inner == d and ret_w == d and conv_dim == conv_w.shape[2]

    tp, ts = bp * seq, bs * dec_seq
    t_real = tp + ts + n_meta
    tm = 512 if t_real >= 4096 else 64
    t_pad = -(-t_real // tm) * tm
    tm_big = 2 * tm if t_pad % (2 * tm) == 0 else tm
    g_prompt = _Group(0, bp, seq // CHUNK, CHUNK)
    g_sample = _Group(tp, bs, 1, dec_seq)
    g_meta = _Group(tp + ts, 1, 1, n_meta, nc_grid=(t_pad - tp - ts) // n_meta)
    assert tp % dec_seq == 0 and (tp + ts) % n_meta == 0 and (t_pad - tp - ts) % n_meta == 0

    x0 = jnp.concatenate([x_prompt.reshape(tp, d), x_sample.reshape(ts, d), meta_tokens,
                          jnp.zeros((t_pad - t_real, d), F32)], axis=0)

    w_in = w_in_ab[0]
    o_z, o_xbc, o_dt, o_q = 0, d_inner, d_inner + conv_dim, d_inner + conv_dim + n_ssd_heads
    n_cols = 6 * d + bc_w
    tn = n_cols // 4 if n_cols % (4 * LANES) == 0 else n_cols
    w_perm = jnp.concatenate([
        w_in[:, o_z:o_z + d_inner], w_in[:, o_q:o_q + 4 * ret_w], w_in[:, o_xbc:o_xbc + conv_dim]],
        axis=1).astype(BF16)
    w_dt = jnp.pad(w_in[:, o_dt:o_dt + n_ssd_heads], ((0, 0), (0, LANES - n_ssd_heads))).astype(BF16)
    proj, proj_dt = _normmm(x0, norm_mix[0], w_perm, w_dt, tm_big, tn)
    cb_x, cb_bc = 5, (6 * d) // bc_w
    assert (6 * d) % bc_w == 0

    expand = (jnp.arange(LANES)[:, None] == (jnp.arange(d_inner) // SSD_HEADDIM)[None, :]).astype(BF16)
    pad_h = LANES - n_ssd_heads
    ssd_consts = [conv_w[0], conv_b[0].reshape(1, conv_dim), jnp.pad(dt_bias[0], (0, pad_h)).reshape(1, LANES),
                  jnp.pad(a_log[0], (0, pad_h)).reshape(1, LANES),
                  jnp.repeat(d_skip[0], SSD_HEADDIM).reshape(1, d_inner), ssd_norm[0].reshape(1, d_inner), expand]
    n_pairs = n_ssd_heads // 2

    def ssd(group, hist0, s0, y_prev):
        nb = group.nb
        return _scan_call(
            _ssd_kernel, group,
            [(proj, d_inner, 0), (proj, d_inner, cb_x), (proj, bc_w, cb_bc), (proj_dt, LANES, 0)],
            ssd_consts, [hist0, _pair_heads(s0)], y_prev, t_pad, d_inner,
            [jax.ShapeDtypeStruct((nb, n_pairs, SSD_STATE, LANES), F32),
             jax.ShapeDtypeStruct((nb, SSD_CONV - 1, conv_dim), F32)],
            [_per_stream((nb, n_pairs, SSD_STATE, LANES)), _per_stream((nb, SSD_CONV - 1, conv_dim))],
            [pltpu.VMEM((HIST_ROWS + group.c, conv_dim), F32)])

    def bcast(s, n):
        return jnp.broadcast_to(s, (n,) + s.shape[1:])

    zeros = jnp.zeros
    y_ssd, s_m, h_m = ssd(g_meta, zeros((1, SSD_CONV - 1, conv_dim), F32),
                          zeros((1, n_ssd_heads, SSD_STATE, SSD_HEADDIM), F32), None)
    y_ssd, ssd_p, conv_p = ssd(g_prompt, bcast(h_m, bp), bcast(_unpair_heads(s_m, SSD_HEADDIM), bp), y_ssd)
    y_ssd, ssd_s, conv_s = ssd(g_sample, state_ssd_conv[0], state_ssd[0], y_ssd)
    ssd_p = _unpair_heads(ssd_p, SSD_HEADDIM)
    ssd_s = _unpair_heads(ssd_s, SSD_HEADDIM)

    def ret(group, pos, s0, y_prev):
        nb = group.nb
        sin, cos = _rotation_tables(pos, LANES)
        dmat, ecum, wend, sdec = _retention_decay(group.c, RET_HEADS, LANES)
        return _scan_call(
            _ret_kernel, group,
            [(proj, ret_w, 1), (proj, ret_w, 2), (proj, ret_w, 3), (proj, ret_w, 4)],
            [(sin, group.chunk_rows(LANES)), (cos, group.chunk_rows(LANES)), dmat, ecum, wend, sdec],
            [s0], y_prev, t_pad, ret_w,
            [jax.ShapeDtypeStruct((nb, RET_HEADS, LANES, LANES), F32)],
            [_per_stream((nb, RET_HEADS, LANES, LANES))], [])

    y_ret, r_m = ret(g_meta, jnp.arange(n_meta), zeros((1, RET_HEADS, LANES, LANES), F32), None)
    y_ret, ret_p = ret(g_prompt, n_meta + jnp.arange(seq), bcast(r_m, bp), y_ret)
    y_ret, ret_s = ret(g_sample, n_meta + PAST_LEN + jnp.arange(dec_seq), state_ret[0], y_ret)

    w_out = w_out_ab[0].astype(BF16)
    ff = w_ffn_gate.shape[2]
    tf = 256 if ff % 256 == 0 else ff
    x2 = _ffn(x0, [y_ssd, y_ret], [w_out[:d_inner], w_out[d_inner:]], norm_ffn[0], w_ffn_gate[0].astype(BF16),
              w_ffn_up[0].astype(BF16), w_ffn_down[0].astype(BF16), tm_big, tf)

    w_c = w_in_c[0].astype(BF16)
    w_qig = jnp.concatenate([w_c[:, :d], w_c[:, 2 * d:]], axis=1)
    proj_c, proj_f = _normmm(x2, norm_mix[1], w_qig, w_c[:, d:2 * d], tm_big, (3 * d) // 2)
    lb_soft = jax.nn.softmax(hgrn_lb.astype(F32), axis=0)
    lb = (jnp.cumsum(lb_soft, axis=0) - lb_soft[0])[1].reshape(1, d)
    n_hg = d // HG_DK
    hg_consts = [lb, hgrn_norm[0].reshape(1, LANES)]

    def hgrn(group, s0, y_prev):
        nb = group.nb
        return _scan_call(
            _hgrn_kernel, group,
            [(proj_c, d, 0), (proj_f, d, 0), (proj_c, d, 1), (proj_c, d, 2)],
            hg_consts, [s0], y_prev, t_pad, d,
            [jax.ShapeDtypeStruct((nb, n_hg, HG_DK, LANES), F32)],
            [_per_stream((nb, n_hg, HG_DK, LANES))], [pltpu.VMEM((n_hg, LANES, HG_DK), F32)])

    y_hg, g_m = hgrn(g_meta, zeros((1, n_hg, HG_DK, LANES), F32), None)
    y_hg, hg_p = hgrn(g_prompt, bcast(g_m, bp), y_hg)
    y_hg, hg_s = hgrn(g_sample, state_hgrn[0], y_hg)

    n_exp = w_router.shape[2]
    x3, route, counts = _router(x2, y_hg, w_out_c[0].astype(BF16), norm_ffn[1], w_router[0], tm, t_real)
    tmb = tm
    n_blocks = (TOP_K * t_real + n_exp * (tmb - 1)) // tmb
    pos, blk_e, blk_n = _moe_plan(route, counts, n_exp, tmb, n_blocks, TOP_K, t_real)
    xs = _dispatch(pos, x3, n_blocks * tmb + TOP_K * (t_pad - t_real), tm, TOP_K)
    ffe = w_exp_gate.shape[3]
    tfe = 512 if ffe % 512 == 0 else ffe
    ys = _experts(blk_e, blk_n, xs, norm_ffn[1], w_exp_gate[0].astype(BF16), w_exp_up[0].astype(BF16),
                  w_exp_down[0].astype(BF16), tmb, tfe)
    y_prompt, y_sample = _combine(pos, x3, route, norm_final, ys, tm, TOP_K, tp, ts)
    y_prompt = y_prompt.reshape(bp, seq, d)
    y_sample = y_sample.reshape(bs, dec_seq, d)
    return (y_prompt, y_sample, ssd_p[None], conv_p[None], ret_p[None], hg_p[None],
            ssd_s[None], conv_s[None], ret_s[None], hg_s[None])
```

```python
import functools

import jax
import jax.numpy as jnp
from jax import lax
from jax.experimental import pallas as pl
from jax.experimental.pallas import tpu as pltpu

F32 = jnp.float32
BF16 = jnp.bfloat16

CHUNK = 64
N_META = 16
PAST_LEN = 2048
EPS = 1e-6
SSD_HEADDIM = 64
SSD_GROUPS = 2
SSD_STATE = 128
SSD_CONV = 4
RET_HEADS = 8
ROPE_BASE = 10000.0
HG_DK = 128
TOP_K = 2

LANES = 128
SUBLANES = 8
HIST_ROWS = 8
VMEM_LIMIT = 56 * 1024 * 1024
NEG_BIG = -1e30


def _dot(a, b):
    return jnp.dot(a, b, preferred_element_type=F32)


def _dot_nt(a, b):
    return lax.dot_general(a, b, (((1,), (1,)), ((), ())), preferred_element_type=F32)


def _dot_tn(a, b):
    return lax.dot_general(a, b, (((0,), (0,)), ((), ())), preferred_element_type=F32)


def _split3(x):
    hi = x.astype(BF16)
    r = x - hi.astype(F32)
    mid = r.astype(BF16)
    lo = (r - mid.astype(F32)).astype(BF16)
    return hi, mid, lo


def _exact_ldot(x, m01):
    hi, mid, lo = _split3(x)
    return _dot(hi, m01) + _dot(mid, m01) + _dot(lo, m01)


def _exact_rdot(m01, x):
    hi, mid, lo = _split3(x)
    return _dot(m01, hi) + _dot(m01, mid) + _dot(m01, lo)


def _causal(c):
    row = lax.broadcasted_iota(jnp.int32, (c, c), 0)
    col = lax.broadcasted_iota(jnp.int32, (c, c), 1)
    return row >= col


def _cumsum_time(x):
    c = x.shape[0]
    return _exact_rdot(jnp.where(_causal(c), 1.0, 0.0).astype(BF16), x)


def _sigmoid(x):
    return 1.0 / (1.0 + jnp.exp(-x))


def _silu(x):
    return x * _sigmoid(x)


def _rms(x):
    return x * lax.rsqrt(jnp.mean(x * x, axis=-1, keepdims=True) + EPS)


def _normmm_kernel(*refs, n_src, n_lead_tiles):
    xs, (nw_ref, w_ref, ws_ref, o_ref, os_ref) = refs[:n_src], refs[n_src:n_src + 5]
    xn_scr = refs[-1]

    @pl.when(pl.program_id(1) == 0)
    def _():
        if n_src == 1:
            x_ref = xs[0]
        else:
            lead_ref, tail_ref, meta_ref = xs
            x_ref = refs[n_src + 5]
            i = pl.program_id(0)
            tm = x_ref.shape[0]
            n_tail, n_meta = tail_ref.shape[0], meta_ref.shape[0]

            @pl.when(i < n_lead_tiles)
            def _():
                x_ref[...] = lead_ref[...]

            for t in range(pl.cdiv(n_tail + n_meta, tm)):
                @pl.when(i == n_lead_tiles + t)
                def _(lo=t * tm):
                    a, b = lo, min(lo + tm, n_tail)
                    if a < b:
                        x_ref[a - lo:b - lo, :] = tail_ref[a:b, :]
                    a, b = max(lo, n_tail), min(lo + tm, n_tail + n_meta)
                    if a < b:
                        x_ref[a - lo:b - lo, :] = meta_ref[a - n_tail:b - n_tail, :]
                    if b < lo + tm:
                        x_ref[b - lo:, :] = jnp.zeros((lo + tm - b, x_ref.shape[1]), x_ref.dtype)

        xn = (_rms(x_ref[...]) * nw_ref[...]).astype(BF16)
        xn_scr[...] = xn
        os_ref[...] = _dot(xn, ws_ref[...])

    o_ref[...] = _dot(xn_scr[...], w_ref[...]).astype(BF16)


def _normmm(x, nw, w, w_side, tm, tn, t_pad=None):
    sources = x if isinstance(x, tuple) else (x,)
    k = sources[0].shape[1]
    n, ns = w.shape[1], w_side.shape[1]
    out_specs = [pl.BlockSpec((tm, tn), lambda i, j: (i, j)), pl.BlockSpec((tm, ns), lambda i, j: (i, 0))]
    if len(sources) == 1:
        t, n_lead = x.shape[0], 0
        x_specs = [pl.BlockSpec((tm, k), lambda i, j: (i, 0))]
        out_shape = [jax.ShapeDtypeStruct((t, n), BF16), jax.ShapeDtypeStruct((t, ns), F32)]
    else:
        lead, tail, meta = sources
        t, n_lead = t_pad, lead.shape[0] // tm
        assert lead.shape[0] % tm == 0 and (n_lead + pl.cdiv(tail.shape[0] + meta.shape[0], tm)) * tm == t_pad
        x_specs = [pl.BlockSpec((tm, k), lambda i, j: (jnp.minimum(i, n_lead - 1), 0)),
                   pl.BlockSpec(tail.shape, lambda i, j: (0, 0)),
                   pl.BlockSpec(meta.shape, lambda i, j: (0, 0))]
        out_specs.append(pl.BlockSpec((tm, k), lambda i, j: (i, 0)))
        out_shape = [jax.ShapeDtypeStruct((t, n), BF16), jax.ShapeDtypeStruct((t, ns), F32),
                     jax.ShapeDtypeStruct((t, k), F32)]
    return pl.pallas_call(
        functools.partial(_normmm_kernel, n_src=len(sources), n_lead_tiles=n_lead),
        grid=(t // tm, n // tn),
        in_specs=x_specs + [pl.BlockSpec((1, k), lambda i, j: (0, 0)),
                            pl.BlockSpec((k, tn), lambda i, j: (0, j)),
                            pl.BlockSpec((k, ns), lambda i, j: (0, 0))],
        out_specs=out_specs,
        out_shape=out_shape,
        scratch_shapes=[pltpu.VMEM((tm, k), BF16)],
        compiler_params=pltpu.CompilerParams(dimension_semantics=("parallel", "arbitrary"),
                                             vmem_limit_bytes=VMEM_LIMIT),
    )(*sources, nw.reshape(1, k), w, w_side)


def _ffn_kernel(*refs, n_in):
    x_ref = refs[0]
    ys = refs[1:1 + n_in]
    ws = refs[1 + n_in:1 + 2 * n_in]
    nw_ref, wg_ref, wu_ref, wd_ref, o_ref, xn_scr, acc_scr = refs[1 + 2 * n_in:]
    f = pl.program_id(1)

    @pl.when(f == 0)
    def _():
        x = x_ref[...]
        for y, w in zip(ys, ws):
            x = x + _dot(y[...], w[...])
        xn_scr[...] = (_rms(x) * nw_ref[...]).astype(BF16)
        acc_scr[...] = x

    xn = xn_scr[...]
    h = _silu(_dot(xn, wg_ref[...])) * _dot(xn, wu_ref[...])
    acc_scr[...] += _dot(h.astype(BF16), wd_ref[...])

    @pl.when(f == pl.num_programs(1) - 1)
    def _():
        o_ref[...] = acc_scr[...]


def _ffn(x, ys, ws, nw, wg, wu, wd, tm, tf):
    t, d = x.shape
    ff = wg.shape[1]
    n_in = len(ys)
    in_specs = [pl.BlockSpec((tm, d), lambda i, f: (i, 0))]
    in_specs += [pl.BlockSpec((tm, y.shape[1]), lambda i, f: (i, 0)) for y in ys]
    in_specs += [pl.BlockSpec(w.shape, lambda i, f: (0, 0)) for w in ws]
    in_specs += [pl.BlockSpec((1, d), lambda i, f: (0, 0)),
                 pl.BlockSpec((d, tf), lambda i, f: (0, f)),
                 pl.BlockSpec((d, tf), lambda i, f: (0, f)),
                 pl.BlockSpec((tf, d), lambda i, f: (f, 0))]
    return pl.pallas_call(
        functools.partial(_ffn_kernel, n_in=n_in),
        grid=(t // tm, ff // tf),
        in_specs=in_specs,
        out_specs=pl.BlockSpec((tm, d), lambda i, f: (i, 0)),
        out_shape=jax.ShapeDtypeStruct((t, d), F32),
        scratch_shapes=[pltpu.VMEM((tm, d), BF16), pltpu.VMEM((tm, d), F32)],
        compiler_params=pltpu.CompilerParams(dimension_semantics=("parallel", "arbitrary"),
                                             vmem_limit_bytes=VMEM_LIMIT),
    )(x, *ys, *ws, nw.reshape(1, d), wg, wu, wd)


ROUTE_E1, ROUTE_E2, ROUTE_R1, ROUTE_R2, ROUTE_G1, ROUTE_G2 = range(6)


def _router_kernel(x_ref, y_ref, wo_ref, nw_ref, whi_ref, wlo_ref, xo_ref, route_ref, count_ref, *, n_experts,
                   t_real):
    i = pl.program_id(0)
    tm = x_ref.shape[0]

    @pl.when(i == 0)
    def _():
        count_ref[...] = jnp.zeros_like(count_ref)

    xo = x_ref[...] + _dot(y_ref[...], wo_ref[...])
    xo_ref[...] = xo
    xn = _rms(xo) * nw_ref[...]
    hi = xn.astype(BF16)
    lo = (xn - hi.astype(F32)).astype(BF16)
    logits = _dot(hi, whi_ref[...]) + _dot(lo, whi_ref[...]) + _dot(hi, wlo_ref[...])
    lane = lax.broadcasted_iota(jnp.int32, logits.shape, 1)
    lane_f = lane.astype(F32)
    logits = jnp.where(lane < n_experts, logits, NEG_BIG)
    m1 = jnp.max(logits, axis=-1, keepdims=True)
    i1 = jnp.min(jnp.where(logits == m1, lane_f, float(LANES)), axis=-1, keepdims=True)
    rest = jnp.where(lane_f == i1, NEG_BIG, logits)
    m2 = jnp.max(rest, axis=-1, keepdims=True)
    i2 = jnp.min(jnp.where(rest == m2, lane_f, float(LANES)), axis=-1, keepdims=True)
    e2 = jnp.exp(m2 - m1)
    g1 = 1.0 / (1.0 + e2)
    g2 = e2 / (1.0 + e2)

    row = lax.broadcasted_iota(jnp.int32, (tm, 1), 0) + i * tm
    valid = row < t_real
    sel = jnp.where(valid & ((lane_f == i1) | (lane_f == i2)), 1.0, 0.0)
    r = lax.broadcasted_iota(jnp.int32, (tm, tm), 0)
    q = lax.broadcasted_iota(jnp.int32, (tm, tm), 1)
    before = jnp.where(q < r, 1.0, 0.0).astype(BF16)
    rank = count_ref[...] + _dot(before, sel.astype(BF16))
    r1 = jnp.sum(jnp.where(lane_f == i1, rank, 0.0), axis=-1, keepdims=True)
    r2 = jnp.sum(jnp.where(lane_f == i2, rank, 0.0), axis=-1, keepdims=True)
    count_ref[...] += jnp.sum(sel, axis=0, keepdims=True)

    rec = jnp.zeros_like(logits)
    for k, v in ((ROUTE_E1, i1), (ROUTE_E2, i2), (ROUTE_R1, r1), (ROUTE_R2, r2), (ROUTE_G1, g1), (ROUTE_G2, g2)):
        rec = jnp.where(lane == k, v, rec)
    route_ref[...] = jnp.where(valid, rec, 0.0)


def _router(x, y, wo, nw, w_router, tm, t_real):
    t, d = x.shape
    e = w_router.shape[1]
    wpad = jnp.zeros((d, LANES), F32).at[:, :e].set(w_router)
    whi = wpad.astype(BF16)
    wlo = (wpad - whi.astype(F32)).astype(BF16)
    return pl.pallas_call(
        functools.partial(_router_kernel, n_experts=e, t_real=t_real),
        grid=(t // tm,),
        in_specs=[pl.BlockSpec((tm, d), lambda i: (i, 0)),
                  pl.BlockSpec((tm, y.shape[1]), lambda i: (i, 0)),
                  pl.BlockSpec(wo.shape, lambda i: (0, 0)),
                  pl.BlockSpec((1, d), lambda i: (0, 0)),
                  pl.BlockSpec((d, LANES), lambda i: (0, 0)),
                  pl.BlockSpec((d, LANES), lambda i: (0, 0))],
        out_specs=[pl.BlockSpec((tm, d), lambda i: (i, 0)), pl.BlockSpec((tm, LANES), lambda i: (i, 0)),
                   pl.BlockSpec((1, LANES), lambda i: (0, 0))],
        out_shape=[jax.ShapeDtypeStruct((t, d), F32), jax.ShapeDtypeStruct((t, LANES), F32),
                   jax.ShapeDtypeStruct((1, LANES), F32)],
        compiler_params=pltpu.CompilerParams(dimension_semantics=("arbitrary",),
                                             vmem_limit_bytes=VMEM_LIMIT),
    )(x, y, wo, nw.reshape(1, d), whi, wlo)


def _row_copy(src, src_row, dst, dst_row, sem):
    return pltpu.make_async_copy(src.at[pl.ds(src_row, 1)], dst.at[pl.ds(dst_row, 1)], sem)


def _dispatch_kernel(pos_ref, x_ref, xs_hbm, sem, *, tm, top_k):
    i = pl.program_id(0)

    def start(g, carry):
        r0 = pl.multiple_of(g * SUBLANES, SUBLANES)
        for b in range(SUBLANES):
            for k in range(top_k):
                _row_copy(x_ref, r0 + b, xs_hbm, pos_ref[(i * tm + r0 + b) * top_k + k], sem).start()
        return carry

    def wait(g, carry):
        for _ in range(SUBLANES * top_k):
            _row_copy(x_ref, 0, xs_hbm, 0, sem).wait()
        return carry

    lax.fori_loop(0, tm // SUBLANES, start, 0)
    lax.fori_loop(0, tm // SUBLANES, wait, 0)


def _dispatch(pos, x, n_rows_out, tm, top_k):
    t, d = x.shape
    return pl.pallas_call(
        functools.partial(_dispatch_kernel, tm=tm, top_k=top_k),
        grid_spec=pltpu.PrefetchScalarGridSpec(
            num_scalar_prefetch=1, grid=(t // tm,),
            in_specs=[pl.BlockSpec((tm, d), lambda i, p: (i, 0))],
            out_specs=pl.BlockSpec(memory_space=pl.ANY),
            scratch_shapes=[pltpu.SemaphoreType.DMA(())]),
        out_shape=jax.ShapeDtypeStruct((n_rows_out, d), F32),
        compiler_params=pltpu.CompilerParams(dimension_semantics=("arbitrary",),
                                             vmem_limit_bytes=VMEM_LIMIT),
    )(pos, x)


def _expert_kernel(blk_e_ref, blk_n_ref, xs_ref, nw_ref, wg_ref, wu_ref, wd_ref, o_ref, xn_scr, acc_scr):
    del blk_e_ref
    j = pl.program_id(0)
    f = pl.program_id(1)
    n_valid = blk_n_ref[j]
    last = f == pl.num_programs(1) - 1

    @pl.when((n_valid > 0) & (f == 0))
    def _():
        row = lax.broadcasted_iota(jnp.int32, (xs_ref.shape[0], 1), 0)
        x = jnp.where(row < n_valid, xs_ref[...], 0.0)
        xn_scr[...] = (_rms(x) * nw_ref[...]).astype(BF16)
        acc_scr[...] = jnp.zeros_like(acc_scr)

    @pl.when(n_valid > 0)
    def _():
        xn = xn_scr[...]
        h = _silu(_dot(xn, wg_ref[...])) * _dot(xn, wu_ref[...])
        acc_scr[...] += _dot(h.astype(BF16), wd_ref[...])

    @pl.when((n_valid > 0) & last)
    def _():
        o_ref[...] = acc_scr[...]

    @pl.when((n_valid == 0) & last)
    def _():
        o_ref[...] = jnp.zeros_like(o_ref)


def _experts(blk_e, blk_n, xs, nw, wg, wu, wd, tmb, tf):
    d = xs.shape[1]
    s = blk_e.shape[0] * tmb
    ff = wg.shape[2]
    nf = ff // tf

    def w_up(j, f, be, bn):
        return (be[j], 0, jnp.where(bn[j] > 0, f, nf - 1))

    def w_down(j, f, be, bn):
        return (be[j], jnp.where(bn[j] > 0, f, nf - 1), 0)

    return pl.pallas_call(
        _expert_kernel,
        grid_spec=pltpu.PrefetchScalarGridSpec(
            num_scalar_prefetch=2, grid=(s // tmb, nf),
            in_specs=[pl.BlockSpec((tmb, d), lambda j, f, be, bn: (j, 0)),
                      pl.BlockSpec((1, d), lambda j, f, be, bn: (0, 0)),
                      pl.BlockSpec((None, d, tf), w_up),
                      pl.BlockSpec((None, d, tf), w_up),
                      pl.BlockSpec((None, tf, d), w_down)],
            out_specs=pl.BlockSpec((tmb, d), lambda j, f, be, bn: (j, 0)),
            scratch_shapes=[pltpu.VMEM((tmb, d), BF16), pltpu.VMEM((tmb, d), F32)]),
        out_shape=jax.ShapeDtypeStruct((s, d), F32),
        compiler_params=pltpu.CompilerParams(dimension_semantics=("parallel", "arbitrary"),
                                             vmem_limit_bytes=VMEM_LIMIT),
    )(blk_e, blk_n, xs, nw.reshape(1, d), wg, wu, wd)


def _combine_kernel(pos_ref, x_ref, route_ref, nw_ref, ys_hbm, op_ref, os_ref, ybuf, sem, *, tm, top_k,
                    n_prompt_tiles):
    i = pl.program_id(0)

    def start(g, carry):
        r0 = pl.multiple_of(g * SUBLANES, SUBLANES)
        for b in range(SUBLANES):
            for k in range(top_k):
                _row_copy(ys_hbm, pos_ref[(i * tm + r0 + b) * top_k + k], ybuf.at[k], r0 + b, sem).start()
        return carry

    def wait(g, carry):
        for _ in range(SUBLANES * top_k):
            _row_copy(ys_hbm, 0, ybuf.at[0], 0, sem).wait()
        return carry

    lax.fori_loop(0, tm // SUBLANES, start, 0)
    lax.fori_loop(0, tm // SUBLANES, wait, 0)
    route = route_ref[...]
    out = x_ref[...] + route[:, ROUTE_G1:ROUTE_G1 + 1] * ybuf[0] + route[:, ROUTE_G2:ROUTE_G2 + 1] * ybuf[1]
    y = _rms(out) * nw_ref[...]

    @pl.when(i < n_prompt_tiles)
    def _():
        op_ref[...] = y

    @pl.when(i >= n_prompt_tiles)
    def _():
        os_ref[...] = y


def _combine(pos, x, route, nw, ys, tm, top_k, tp, ts):
    d = x.shape[1]
    assert top_k == 2 and tp % tm == 0 and ts % tm == 0
    npt, nst = tp // tm, ts // tm
    return pl.pallas_call(
        functools.partial(_combine_kernel, tm=tm, top_k=top_k, n_prompt_tiles=npt),
        grid_spec=pltpu.PrefetchScalarGridSpec(
            num_scalar_prefetch=1, grid=(npt + nst,),
            in_specs=[pl.BlockSpec((tm, d), lambda i, p: (i, 0)),
                      pl.BlockSpec((tm, LANES), lambda i, p: (i, 0)),
                      pl.BlockSpec((1, d), lambda i, p: (0, 0)),
                      pl.BlockSpec(memory_space=pl.ANY)],
            out_specs=[pl.BlockSpec((tm, d), lambda i, p: (jnp.minimum(i, npt - 1), 0)),
                       pl.BlockSpec((tm, d), lambda i, p: (jnp.maximum(i - npt, 0), 0))],
            scratch_shapes=[pltpu.VMEM((top_k, tm, d), F32), pltpu.SemaphoreType.DMA(())]),
        out_shape=[jax.ShapeDtypeStruct((tp, d), F32), jax.ShapeDtypeStruct((ts, d), F32)],
        compiler_params=pltpu.CompilerParams(dimension_semantics=("arbitrary",),
                                             vmem_limit_bytes=VMEM_LIMIT),
    )(pos, x, route, nw.reshape(1, d), ys)


def _moe_plan(route, counts, n_experts, tmb, n_blocks, top_k, t_real):
    cnt = counts[0, :n_experts].astype(jnp.int32)
    nblk = (cnt + tmb - 1) // tmb
    blk_end = jnp.cumsum(nblk)
    blk_start = blk_end - nblk
    slot0 = blk_start * tmb
    e_idx = route[:, ROUTE_E1:ROUTE_E1 + top_k].astype(jnp.int32)
    rank = route[:, ROUTE_R1:ROUTE_R1 + top_k].astype(jnp.int32)
    onehot = e_idx[:, :, None] == jnp.arange(n_experts)[None, None, :]
    pos = jnp.sum(jnp.where(onehot, slot0[None, None, :], 0), axis=-1) + rank
    j = jnp.arange(n_blocks)
    used = j < blk_end[-1]
    blk_e = jnp.minimum(jnp.sum(j[:, None] >= blk_end[None, :], axis=1), n_experts - 1)
    last_e = jnp.max(jnp.where(nblk > 0, jnp.arange(n_experts), 0))
    blk_e = jnp.where(used, blk_e, last_e).astype(jnp.int32)
    blk_n = jnp.where(used, jnp.clip(cnt[blk_e] - (j - blk_start[blk_e]) * tmb, 0, tmb), 0).astype(jnp.int32)
    t = route.shape[0]
    tok = jnp.arange(t)[:, None]
    is_pad = tok >= t_real
    scratch = n_blocks * tmb + (tok - t_real) * top_k + jnp.arange(top_k)[None, :]
    return jnp.where(is_pad, scratch, pos).reshape(-1).astype(jnp.int32), blk_e, blk_n


class _Group:
    def __init__(self, row0, nb, nc, c, out_rows=None):
        self.row0, self.nb, self.nc, self.c = row0, nb, nc, c
        self.out_rows = c if out_rows is None else out_rows
        assert self.out_rows == c or (nb == 1 and nc == 1 and row0 % self.out_rows == 0)

    def rows(self, width, col_block):
        base, nc, c = self.row0 // self.c, self.nc, self.c
        return pl.BlockSpec((c, width), lambda b, i: (base + b * nc + i, col_block))

    def out_block(self, width):
        base, nc, r = self.row0 // self.out_rows, self.nc, self.out_rows
        return pl.BlockSpec((r, width), lambda b, i: (base + b * nc + i, 0))

    def chunk_rows(self, width):
        return pl.BlockSpec((self.c, width), lambda b, i: (i, 0))


def _full(shape):
    nd = len(shape)
    return pl.BlockSpec(tuple(shape), lambda b, i: (0,) * nd)


def _per_stream(shape):
    nd = len(shape)
    return pl.BlockSpec((None,) + tuple(shape[1:]), lambda b, i: (b,) + (0,) * (nd - 1))


def _scan_call(kernel, group, row_inputs, const_inputs, stream_inputs, y_prev, t_pad, width, extra_out_shapes,
               extra_out_specs, scratch_shapes):
    arrays, specs = [], []
    for arr, w, cb in row_inputs:
        arrays.append(arr)
        specs.append(group.rows(w, cb))
    for item in const_inputs:
        if isinstance(item, tuple):
            arr, spec = item
        else:
            arr, spec = item, _full(item.shape)
        arrays.append(arr)
        specs.append(spec)
    for arr in stream_inputs:
        arrays.append(arr)
        specs.append(_per_stream(arr.shape))
    aliases = {}
    if y_prev is not None:
        aliases = {len(arrays): 0}
        arrays.append(y_prev)
        specs.append(pl.BlockSpec(memory_space=pl.ANY))
    out_shape = [jax.ShapeDtypeStruct((t_pad, width), BF16)] + list(extra_out_shapes)
    out_specs = [group.out_block(width)] + list(extra_out_specs)
    return pl.pallas_call(
        functools.partial(kernel, c=group.c, nc=group.nc, has_prev=y_prev is not None),
        grid=(group.nb, group.nc),
        in_specs=specs,
        out_specs=out_specs,
        out_shape=out_shape,
        scratch_shapes=scratch_shapes,
        input_output_aliases=aliases,
        compiler_params=pltpu.CompilerParams(dimension_semantics=("parallel", "arbitrary"),
                                             vmem_limit_bytes=VMEM_LIMIT),
    )(*arrays)


def _store_rows(y_ref, y):
    c = y.shape[0]
    y_ref[0:c, :] = y
    if y_ref.shape[0] > c:
        y_ref[c:, :] = jnp.zeros((y_ref.shape[0] - c, y_ref.shape[1]), y_ref.dtype)


def _ssd_kernel(*refs, c, nc, has_prev):
    (z_ref, x_ref, bc_ref, dt_ref, convw_ref, convb_ref, dtb_ref, alog_ref, dskip_ref, nw_ref, expand_ref,
     hist0_ref, s0_ref) = refs[:13]
    y_ref, s_ref, hout_ref, hist_scr = refs[13 + int(has_prev):]
    i = pl.program_id(1)
    d_inner = x_ref.shape[1]
    n_state = SSD_STATE
    n_pairs = d_inner // LANES
    pairs_per_group = n_pairs // SSD_GROUPS
    n_hist = SSD_CONV - 1

    @pl.when(i == 0)
    def _():
        s_ref[...] = s0_ref[...]
        hist_scr[HIST_ROWS - n_hist:HIST_ROWS, :] = hist0_ref[...]

    def chunk():
        hist_scr[HIST_ROWS:HIST_ROWS + c, 0:d_inner] = x_ref[...].astype(F32)
        hist_scr[HIST_ROWS:HIST_ROWS + c, d_inner:] = bc_ref[...].astype(F32)
        conv = convb_ref[...]
        for j in range(SSD_CONV):
            conv = conv + hist_scr[HIST_ROWS - n_hist + j:HIST_ROWS - n_hist + j + c, :] * convw_ref[j:j + 1, :]
        tail = hist_scr[HIST_ROWS + c - n_hist:HIST_ROWS + c, :]
        hist_scr[HIST_ROWS - n_hist:HIST_ROWS, :] = tail
        hout_ref[...] = tail
        xbc = _silu(conv)
        xs = xbc[:, 0:d_inner]
        bm = xbc[:, d_inner:d_inner + SSD_GROUPS * n_state]
        cm = xbc[:, d_inner + SSD_GROUPS * n_state:]

        x_dt = dt_ref[...] + dtb_ref[...]
        dt = jnp.maximum(x_dt, 0.0) + jnp.log1p(jnp.exp(-jnp.abs(x_dt)))
        log_a = dt * (-jnp.exp(alog_ref[...]))
        cum = _cumsum_time(log_a)
        expand = expand_ref[...]
        dt_e = _exact_ldot(dt, expand)
        cum_e = _exact_ldot(cum, expand)
        last_e = cum_e[c - 1:c, :]
        xdt = xs * dt_e
        wx = jnp.exp(last_e - cum_e) * xdt
        ecum = jnp.exp(cum_e)
        sdecay = jnp.exp(last_e)

        row = lax.broadcasted_iota(jnp.int32, (c, 2 * c), 0)
        col = lax.broadcasted_iota(jnp.int32, (c, 2 * c), 1)
        second = col >= c
        tcol = jnp.where(second, col - c, col)
        causal2 = tcol <= row
        diag2 = tcol == row
        r2 = lax.broadcasted_iota(jnp.int32, (2 * c, LANES), 0)
        l2 = lax.broadcasted_iota(jnp.int32, (2 * c, LANES), 1)
        half2 = (r2 >= c) == (l2 >= SSD_HEADDIM)

        ys = []
        for g in range(SSD_GROUPS):
            cg = cm[:, g * n_state:(g + 1) * n_state].astype(BF16)
            bg = bm[:, g * n_state:(g + 1) * n_state].astype(BF16)
            scores2 = _dot_nt(cg, jnp.concatenate([bg, bg], axis=0))
            for p in range(g * pairs_per_group, (g + 1) * pairs_per_group):
                sl = slice(p * LANES, (p + 1) * LANES)
                cum_p = cum_e[:, sl]
                cum_col = jnp.where(second, cum_p[:, SSD_HEADDIM:SSD_HEADDIM + 1], cum_p[:, 0:1])
                cum_row = jnp.sum(jnp.where(diag2, cum_col, 0.0), axis=0, keepdims=True)
                seg = cum_col - cum_row
                m2 = (scores2 * jnp.exp(jnp.where(causal2, seg, NEG_BIG))).astype(BF16)
                xp = xdt[:, sl]
                x2 = jnp.where(half2, jnp.concatenate([xp, xp], axis=0), 0.0).astype(BF16)
                y = _dot(m2, x2)
                y = y + _dot(cg, s_ref[p].astype(BF16)) * ecum[:, sl]
                s_ref[p] = s_ref[p] * sdecay[:, sl] + _dot_tn(bg, wx[:, sl].astype(BF16))
                ys.append(y)
        y = jnp.concatenate(ys, axis=1)
        y = y + xs * dskip_ref[...]
        y = y * _silu(z_ref[...].astype(F32))
        gw = d_inner // SSD_GROUPS
        y = jnp.concatenate([_rms(y[:, g * gw:(g + 1) * gw]) for g in range(SSD_GROUPS)], axis=1)
        _store_rows(y_ref, (y * nw_ref[...]).astype(BF16))

    chunk()


def _ret_kernel(*refs, c, nc, has_prev):
    (q_ref, k_ref, v_ref, g_ref, sin_ref, cos_ref, dmat_ref, ecum_ref, wend_ref, sdec_ref, s0_ref) = refs[:11]
    y_ref, s_ref = refs[11 + int(has_prev):]
    i = pl.program_id(1)
    n_heads = q_ref.shape[1] // LANES

    @pl.when(i == 0)
    def _():
        s_ref[...] = s0_ref[...]

    def chunk():
        sin = sin_ref[...]
        cos = cos_ref[...]
        even = (lax.broadcasted_iota(jnp.int32, (c, LANES), 1) % 2) == 0

        def rotate(x):
            nxt = pltpu.roll(x, LANES - 1, 1)
            prv = pltpu.roll(x, 1, 1)
            return x * cos + jnp.where(even, -nxt, prv) * sin

        scale = LANES ** -0.5
        ys = []
        for h in range(n_heads):
            sl = slice(h * LANES, (h + 1) * LANES)
            qh = rotate(q_ref[:, sl].astype(F32))
            kh = rotate(k_ref[:, sl].astype(F32)) * scale
            vh = v_ref[:, sl]
            scores = _dot_nt(qh.astype(BF16), kh.astype(BF16)) * dmat_ref[h]
            y = _dot(scores.astype(BF16), vh) + _dot((qh * ecum_ref[:, sl]).astype(BF16), s_ref[h].astype(BF16))
            s_ref[h] = s_ref[h] * sdec_ref[:, sl] + _dot_tn((kh * wend_ref[:, sl]).astype(BF16), vh)
            ys.append(_rms(y) * _silu(g_ref[:, sl].astype(F32)))
        _store_rows(y_ref, jnp.concatenate(ys, axis=1).astype(BF16))

    chunk()


def _hgrn_kernel(*refs, c, nc, has_prev):
    (q_ref, f_ref, v_ref, g_ref, lb_ref, nw_ref, s0_ref) = refs[:7]
    y_ref, sout_ref, s_ref = refs[7 + int(has_prev):]
    i = pl.program_id(1)
    n_heads = q_ref.shape[1] // LANES

    @pl.when(i == 0)
    def _():
        for h in range(n_heads):
            s_ref[h] = s0_ref[h].T

    def chunk():
        lb = lb_ref[...]
        forget = lb + (1.0 - lb) * _sigmoid(f_ref[...])
        kk = 1.0 - forget
        gc = _cumsum_time(jnp.log(forget))
        last = gc[c - 1:c, :]
        qg = (_silu(q_ref[...].astype(F32)) * jnp.exp(gc)).astype(BF16)
        kg = (kk * jnp.exp(-gc)).astype(BF16)
        kw = (kk * jnp.exp(last - gc)).astype(BF16)
        sdec = jnp.exp(last)
        causal = _causal(c)
        ys = []
        for h in range(n_heads):
            sl = slice(h * LANES, (h + 1) * LANES)
            vh = v_ref[:, sl]
            scores = jnp.where(causal, _dot_nt(qg[:, sl], kg[:, sl]), 0.0)
            y = _dot(scores.astype(BF16), vh) + _dot_nt(qg[:, sl], s_ref[h].astype(BF16))
            s_ref[h] = s_ref[h] * sdec[:, sl] + _dot_tn(vh, kw[:, sl])
            ys.append(_rms(y) * nw_ref[...] * _silu(g_ref[:, sl].astype(F32)))
        _store_rows(y_ref, jnp.concatenate(ys, axis=1).astype(BF16))

        @pl.when(i == nc - 1)
        def _():
            for h in range(n_heads):
                sout_ref[h] = s_ref[h].T

    chunk()


def _rotation_tables(pos, dk):
    inv = 1.0 / (ROPE_BASE ** jnp.linspace(0.0, 1.0, dk // 2, dtype=F32))
    ang = pos.astype(F32)[:, None] * jnp.repeat(inv, 2)[None, :]
    return jnp.sin(ang), jnp.cos(ang)


def _retention_decay(c, n_heads, dv):
    log_gamma = jnp.log1p(-(2.0 ** (-5.0 - jnp.arange(n_heads, dtype=F32))))
    cum = jnp.cumsum(jnp.broadcast_to(log_gamma, (c, n_heads)), axis=0)
    cum_h = cum.T
    causal = jnp.tril(jnp.ones((c, c), dtype=bool))
    dmat = jnp.exp(jnp.where(causal, cum_h[:, :, None] - cum_h[:, None, :], -jnp.inf))
    ecum = jnp.repeat(jnp.exp(cum), dv, axis=1)
    wend = jnp.repeat(jnp.exp(cum[-1][None, :] - cum), dv, axis=1)
    sdec = jnp.repeat(jnp.exp(cum[-1])[None, :], dv, axis=1)
    return dmat, ecum, wend, sdec


EXPERT_FF_TILE_MAX = 1792
FFN_FF_TILE_MAX = 1408


def _largest_tile(n, cap):
    best = None
    for k in range(1, n // LANES + 1):
        if n % k == 0 and (n // k) % LANES == 0 and n // k <= cap:
            best = n // k
            break
    return n if best is None else best


def _pair_heads(s):
    nb, h, n, p = s.shape
    return s.reshape(nb, h // 2, 2, n, p).transpose(0, 1, 3, 2, 4).reshape(nb, h // 2, n, 2 * p)


def _unpair_heads(s, p):
    nb, hp, n, _ = s.shape
    return s.reshape(nb, hp, n, 2, p).transpose(0, 1, 3, 2, 4).reshape(nb, 2 * hp, n, p)


def kernel(x_prompt, x_sample, state_ssd, state_ssd_conv, state_ret, state_hgrn, meta_tokens, norm_mix, norm_ffn,
           norm_final, w_in_ab, conv_w, conv_b, dt_bias, a_log, d_skip, ssd_norm, w_out_ab, w_ffn_gate, w_ffn_up,
           w_ffn_down, w_in_c, hgrn_lb, hgrn_norm, w_out_c, w_router, w_exp_gate, w_exp_up, w_exp_down):
    bp, seq, d = x_prompt.shape
    bs, dec_seq, _ = x_sample.shape
    n_meta = meta_tokens.shape[0]
    depth = norm_mix.shape[0]
    assert depth == 2 and seq % CHUNK == 0 and n_meta == N_META and d % LANES == 0
    n_ssd_heads = d_skip.shape[1]
    d_inner = n_ssd_heads * SSD_HEADDIM
    bc_w = 2 * SSD_GROUPS * SSD_STATE
    conv_dim = d_inner + bc_w
    ret_w = RET_HEADS * LANES
    assert d_inner == d and ret_w == d and conv_dim == conv_w.shape[2]

    tp, ts = bp * seq, bs * dec_seq
    t_real = tp + ts + n_meta
    tm = 512 if t_real >= 4096 else 64
    t_pad = -(-t_real // tm) * tm
    tm_big = 2 * tm if t_pad % (2 * tm) == 0 else tm
    g_prompt = _Group(0, bp, seq // CHUNK, CHUNK)
    g_sample = _Group(tp, bs, 1, dec_seq)
    g_meta = _Group(tp + ts, 1, 1, n_meta, out_rows=t_pad - tp - ts)
    assert tp % dec_seq == 0 and (tp + ts) % n_meta == 0

    w_in = w_in_ab[0]
    o_z, o_xbc, o_dt, o_q = 0, d_inner, d_inner + conv_dim, d_inner + conv_dim + n_ssd_heads
    n_cols = 6 * d + bc_w
    tn = n_cols // 4 if n_cols % (4 * LANES) == 0 else n_cols
    w_perm = jnp.concatenate([
        w_in[:, o_z:o_z + d_inner], w_in[:, o_q:o_q + 4 * ret_w], w_in[:, o_xbc:o_xbc + conv_dim]],
        axis=1).astype(BF16)
    w_dt = jnp.pad(w_in[:, o_dt:o_dt + n_ssd_heads], ((0, 0), (0, LANES - n_ssd_heads))).astype(BF16)
    proj, proj_dt, x0 = _normmm((x_prompt.reshape(tp, d), x_sample.reshape(ts, d), meta_tokens), norm_mix[0],
                                w_perm, w_dt, tm_big, tn, t_pad)
    cb_x, cb_bc = 5, (6 * d) // bc_w
    assert (6 * d) % bc_w == 0

    expand = (jnp.arange(LANES)[:, None] == (jnp.arange(d_inner) // SSD_HEADDIM)[None, :]).astype(BF16)
    pad_h = LANES - n_ssd_heads
    ssd_consts = [conv_w[0], conv_b[0].reshape(1, conv_dim), jnp.pad(dt_bias[0], (0, pad_h)).reshape(1, LANES),
                  jnp.pad(a_log[0], (0, pad_h)).reshape(1, LANES),
                  jnp.repeat(d_skip[0], SSD_HEADDIM).reshape(1, d_inner), ssd_norm[0].reshape(1, d_inner), expand]
    n_pairs = n_ssd_heads // 2

    def ssd(group, hist0, s0, y_prev):
        nb = group.nb
        return _scan_call(
            _ssd_kernel, group,
            [(proj, d_inner, 0), (proj, d_inner, cb_x), (proj, bc_w, cb_bc), (proj_dt, LANES, 0)],
            ssd_consts, [hist0, _pair_heads(s0)], y_prev, t_pad, d_inner,
            [jax.ShapeDtypeStruct((nb, n_pairs, SSD_STATE, LANES), F32),
             jax.ShapeDtypeStruct((nb, SSD_CONV - 1, conv_dim), F32)],
            [_per_stream((nb, n_pairs, SSD_STATE, LANES)), _per_stream((nb, SSD_CONV - 1, conv_dim))],
            [pltpu.VMEM((HIST_ROWS + group.c, conv_dim), F32)])

    def bcast(s, n):
        return jnp.broadcast_to(s, (n,) + s.shape[1:])

    zeros = jnp.zeros
    y_ssd, s_m, h_m = ssd(g_meta, zeros((1, SSD_CONV - 1, conv_dim), F32),
                          zeros((1, n_ssd_heads, SSD_STATE, SSD_HEADDIM), F32), None)
    y_ssd, ssd_p, conv_p = ssd(g_prompt, bcast(h_m, bp), bcast(_unpair_heads(s_m, SSD_HEADDIM), bp), y_ssd)
    y_ssd, ssd_s, conv_s = ssd(g_sample, state_ssd_conv[0], state_ssd[0], y_ssd)
    ssd_p = _unpair_heads(ssd_p, SSD_HEADDIM)
    ssd_s = _unpair_heads(ssd_s, SSD_HEADDIM)

    def ret(group, pos, s0, y_prev):
        nb = group.nb
        sin, cos = _rotation_tables(pos, LANES)
        dmat, ecum, wend, sdec = _retention_decay(group.c, RET_HEADS, LANES)
        return _scan_call(
            _ret_kernel, group,
            [(proj, ret_w, 1), (proj, ret_w, 2), (proj, ret_w, 3), (proj, ret_w, 4)],
            [(sin, group.chunk_rows(LANES)), (cos, group.chunk_rows(LANES)), dmat, ecum, wend, sdec],
            [s0], y_prev, t_pad, ret_w,
            [jax.ShapeDtypeStruct((nb, RET_HEADS, LANES, LANES), F32)],
            [_per_stream((nb, RET_HEADS, LANES, LANES))], [])

    y_ret, r_m = ret(g_meta, jnp.arange(n_meta), zeros((1, RET_HEADS, LANES, LANES), F32), None)
    y_ret, ret_p = ret(g_prompt, n_meta + jnp.arange(seq), bcast(r_m, bp), y_ret)
    y_ret, ret_s = ret(g_sample, n_meta + PAST_LEN + jnp.arange(dec_seq), state_ret[0], y_ret)

    w_out = w_out_ab[0].astype(BF16)
    ff = w_ffn_gate.shape[2]
    tf = _largest_tile(ff, FFN_FF_TILE_MAX)
    x2 = _ffn(x0, [y_ssd, y_ret], [w_out[:d_inner], w_out[d_inner:]], norm_ffn[0], w_ffn_gate[0].astype(BF16),
              w_ffn_up[0].astype(BF16), w_ffn_down[0].astype(BF16), tm, tf)

    w_c = w_in_c[0].astype(BF16)
    w_qig = jnp.concatenate([w_c[:, :d], w_c[:, 2 * d:]], axis=1)
    proj_c, proj_f = _normmm(x2, norm_mix[1], w_qig, w_c[:, d:2 * d], tm_big, (3 * d) // 2)
    lb_soft = jax.nn.softmax(hgrn_lb.astype(F32), axis=0)
    lb = (jnp.cumsum(lb_soft, axis=0) - lb_soft[0])[1].reshape(1, d)
    n_hg = d // HG_DK
    hg_consts = [lb, hgrn_norm[0].reshape(1, LANES)]

    def hgrn(group, s0, y_prev):
        nb = group.nb
        return _scan_call(
            _hgrn_kernel, group,
            [(proj_c, d, 0), (proj_f, d, 0), (proj_c, d, 1), (proj_c, d, 2)],
            hg_consts, [s0], y_prev, t_pad, d,
            [jax.ShapeDtypeStruct((nb, n_hg, HG_DK, LANES), F32)],
            [_per_stream((nb, n_hg, HG_DK, LANES))], [pltpu.VMEM((n_hg, LANES, HG_DK), F32)])

    y_hg, g_m = hgrn(g_meta, zeros((1, n_hg, HG_DK, LANES), F32), None)
    y_hg, hg_p = hgrn(g_prompt, bcast(g_m, bp), y_hg)
    y_hg, hg_s = hgrn(g_sample, state_hgrn[0], y_hg)

    n_exp = w_router.shape[2]
    x3, route, counts = _router(x2, y_hg, w_out_c[0].astype(BF16), norm_ffn[1], w_router[0], tm, t_real)
    tmb = tm
    n_blocks = (TOP_K * t_real + n_exp * (tmb - 1)) // tmb
    pos, blk_e, blk_n = _moe_plan(route, counts, n_exp, tmb, n_blocks, TOP_K, t_real)
    xs = _dispatch(pos, x3, n_blocks * tmb + TOP_K * (t_pad - t_real), tm, TOP_K)
    ffe = w_exp_gate.shape[3]
    tfe = _largest_tile(ffe, EXPERT_FF_TILE_MAX)
    ys = _experts(blk_e, blk_n, xs, norm_ffn[1], w_exp_gate[0].astype(BF16), w_exp_up[0].astype(BF16),
                  w_exp_down[0].astype(BF16), tmb, tfe)
    y_prompt, y_sample = _combine(pos, x3, route, norm_final, ys, tm, TOP_K, tp, ts)
    y_prompt = y_prompt.reshape(bp, seq, d)
    y_sample = y_sample.reshape(bs, dec_seq, d)
    return (y_prompt, y_sample, ssd_p[None], conv_p[None], ret_p[None], hg_p[None],
            ssd_s[None], conv_s[None], ret_s[None], hg_s[None])
```

```python
import functools

import jax
import jax.numpy as jnp
import numpy as np
from jax import lax
from jax.experimental import pallas as pl
from jax.experimental.pallas import tpu as pltpu

F32 = jnp.float32
BF16 = jnp.bfloat16

CHUNK = 64
N_META = 16
PAST_LEN = 2048
EPS = 1e-6
SSD_HEADDIM = 64
SSD_GROUPS = 2
SSD_STATE = 128
SSD_CONV = 4
RET_HEADS = 8
ROPE_BASE = 10000.0
HG_DK = 128
TOP_K = 2

LANES = 128
SUBLANES = 8
HIST_ROWS = 8
VMEM_LIMIT = 56 * 1024 * 1024
NEG_BIG = -1e30


def _dot(a, b):
    return jnp.dot(a, b, preferred_element_type=F32)


def _dot_nt(a, b):
    return lax.dot_general(a, b, (((1,), (1,)), ((), ())), preferred_element_type=F32)


def _dot_tn(a, b):
    return lax.dot_general(a, b, (((0,), (0,)), ((), ())), preferred_element_type=F32)


def _split3(x):
    hi = x.astype(BF16)
    r = x - hi.astype(F32)
    mid = r.astype(BF16)
    lo = (r - mid.astype(F32)).astype(BF16)
    return hi, mid, lo


def _exact_ldot(x, m01):
    hi, mid, lo = _split3(x)
    return _dot(hi, m01) + _dot(mid, m01) + _dot(lo, m01)


def _exact_rdot(m01, x):
    hi, mid, lo = _split3(x)
    return _dot(m01, hi) + _dot(m01, mid) + _dot(m01, lo)


def _causal(c):
    row = lax.broadcasted_iota(jnp.int32, (c, c), 0)
    col = lax.broadcasted_iota(jnp.int32, (c, c), 1)
    return row >= col


def _cumsum_time(x):
    c = x.shape[0]
    return _exact_rdot(jnp.where(_causal(c), 1.0, 0.0).astype(BF16), x)


def _sigmoid(x):
    return 1.0 / (1.0 + jnp.exp(-x))


def _silu(x):
    return x * _sigmoid(x)


def _rms(x):
    return x * lax.rsqrt(jnp.mean(x * x, axis=-1, keepdims=True) + EPS)


def _normmm_kernel(*refs, n_src, n_lead_tiles):
    xs, (nw_ref, w_ref, ws_ref, o_ref, os_ref) = refs[:n_src], refs[n_src:n_src + 5]
    xn_scr = refs[-1]

    @pl.when(pl.program_id(1) == 0)
    def _():
        if n_src == 1:
            x_ref = xs[0]
        else:
            lead_ref, tail_ref, meta_ref = xs
            x_ref = refs[n_src + 5]
            i = pl.program_id(0)
            tm = x_ref.shape[0]
            n_tail, n_meta = tail_ref.shape[0], meta_ref.shape[0]

            @pl.when(i < n_lead_tiles)
            def _():
                x_ref[...] = lead_ref[...]

            for t in range(pl.cdiv(n_tail + n_meta, tm)):
                @pl.when(i == n_lead_tiles + t)
                def _(lo=t * tm):
                    a, b = lo, min(lo + tm, n_tail)
                    if a < b:
                        x_ref[a - lo:b - lo, :] = tail_ref[a:b, :]
                    a, b = max(lo, n_tail), min(lo + tm, n_tail + n_meta)
                    if a < b:
                        x_ref[a - lo:b - lo, :] = meta_ref[a - n_tail:b - n_tail, :]
                    if b < lo + tm:
                        x_ref[b - lo:, :] = jnp.zeros((lo + tm - b, x_ref.shape[1]), x_ref.dtype)

        xn = (_rms(x_ref[...]) * nw_ref[...]).astype(BF16)
        xn_scr[...] = xn
        os_ref[...] = _dot(xn, ws_ref[...])

    o_ref[...] = _dot(xn_scr[...], w_ref[...]).astype(BF16)


def _normmm(x, nw, w, w_side, tm, tn, t_pad=None):
    sources = x if isinstance(x, tuple) else (x,)
    k = sources[0].shape[1]
    n, ns = w.shape[1], w_side.shape[1]
    out_specs = [pl.BlockSpec((tm, tn), lambda i, j: (i, j)), pl.BlockSpec((tm, ns), lambda i, j: (i, 0))]
    if len(sources) == 1:
        t, n_lead = x.shape[0], 0
        x_specs = [pl.BlockSpec((tm, k), lambda i, j: (i, 0))]
        out_shape = [jax.ShapeDtypeStruct((t, n), BF16), jax.ShapeDtypeStruct((t, ns), F32)]
    else:
        lead, tail, meta = sources
        t, n_lead = t_pad, lead.shape[0] // tm
        assert lead.shape[0] % tm == 0 and (n_lead + pl.cdiv(tail.shape[0] + meta.shape[0], tm)) * tm == t_pad
        x_specs = [pl.BlockSpec((tm, k), lambda i, j: (jnp.minimum(i, n_lead - 1), 0)),
                   pl.BlockSpec(tail.shape, lambda i, j: (0, 0)),
                   pl.BlockSpec(meta.shape, lambda i, j: (0, 0))]
        out_specs.append(pl.BlockSpec((tm, k), lambda i, j: (i, 0)))
        out_shape = [jax.ShapeDtypeStruct((t, n), BF16), jax.ShapeDtypeStruct((t, ns), F32),
                     jax.ShapeDtypeStruct((t, k), F32)]
    return pl.pallas_call(
        functools.partial(_normmm_kernel, n_src=len(sources), n_lead_tiles=n_lead),
        grid=(t // tm, n // tn),
        in_specs=x_specs + [pl.BlockSpec((1, k), lambda i, j: (0, 0)),
                            pl.BlockSpec((k, tn), lambda i, j: (0, j)),
                            pl.BlockSpec((k, ns), lambda i, j: (0, 0))],
        out_specs=out_specs,
        out_shape=out_shape,
        scratch_shapes=[pltpu.VMEM((tm, k), BF16)],
        compiler_params=pltpu.CompilerParams(dimension_semantics=("parallel", "arbitrary"),
                                             vmem_limit_bytes=VMEM_LIMIT),
    )(*sources, nw.reshape(1, k), w, w_side)


def _ffn_kernel(*refs, n_in):
    x_ref = refs[0]
    ys = refs[1:1 + n_in]
    ws = refs[1 + n_in:1 + 2 * n_in]
    nw_ref, wg_ref, wu_ref, wd_ref, o_ref, xn_scr, acc_scr = refs[1 + 2 * n_in:]
    f = pl.program_id(1)

    @pl.when(f == 0)
    def _():
        x = x_ref[...]
        for y, w in zip(ys, ws):
            x = x + _dot(y[...], w[...])
        xn_scr[...] = (_rms(x) * nw_ref[...]).astype(BF16)
        acc_scr[...] = x

    xn = xn_scr[...]
    h = _silu(_dot(xn, wg_ref[...])) * _dot(xn, wu_ref[...])
    acc_scr[...] += _dot(h.astype(BF16), wd_ref[...])

    @pl.when(f == pl.num_programs(1) - 1)
    def _():
        o_ref[...] = acc_scr[...]


def _ffn(x, ys, ws, nw, wg, wu, wd, tm, tf):
    t, d = x.shape
    ff = wg.shape[1]
    n_in = len(ys)
    in_specs = [pl.BlockSpec((tm, d), lambda i, f: (i, 0))]
    in_specs += [pl.BlockSpec((tm, y.shape[1]), lambda i, f: (i, 0)) for y in ys]
    in_specs += [pl.BlockSpec(w.shape, lambda i, f: (0, 0)) for w in ws]
    in_specs += [pl.BlockSpec((1, d), lambda i, f: (0, 0)),
                 pl.BlockSpec((d, tf), lambda i, f: (0, f)),
                 pl.BlockSpec((d, tf), lambda i, f: (0, f)),
                 pl.BlockSpec((tf, d), lambda i, f: (f, 0))]
    return pl.pallas_call(
        functools.partial(_ffn_kernel, n_in=n_in),
        grid=(t // tm, ff // tf),
        in_specs=in_specs,
        out_specs=pl.BlockSpec((tm, d), lambda i, f: (i, 0)),
        out_shape=jax.ShapeDtypeStruct((t, d), F32),
        scratch_shapes=[pltpu.VMEM((tm, d), BF16), pltpu.VMEM((tm, d), F32)],
        compiler_params=pltpu.CompilerParams(dimension_semantics=("parallel", "arbitrary"),
                                             vmem_limit_bytes=VMEM_LIMIT),
    )(x, *ys, *ws, nw.reshape(1, d), wg, wu, wd)


ROUTE_E1, ROUTE_E2, ROUTE_R1, ROUTE_R2, ROUTE_G1, ROUTE_G2 = range(6)


def _router_kernel(x_ref, y_ref, wo_ref, nw_ref, whi_ref, wlo_ref, earlier_ref, xo_ref, route_ref, count_ref, *,
                   n_experts, t_real):
    i = pl.program_id(0)
    tm = x_ref.shape[0]
    sub = earlier_ref.shape[0]
    e_rows = count_ref.shape[0]

    @pl.when(i == 0)
    def _():
        count_ref[...] = jnp.zeros_like(count_ref)

    count = count_ref[:, 0:1]
    row = lax.broadcasted_iota(jnp.int32, (e_rows, sub), 0)
    row_f = row.astype(F32)
    field = lax.broadcasted_iota(jnp.int32, (LANES, sub), 0)
    for r0 in range(0, tm, sub):
        rows = slice(r0, r0 + sub)
        xo = x_ref[rows, :] + _dot(y_ref[rows, :], wo_ref[...])
        xo_ref[rows, :] = xo
        xn = _rms(xo) * nw_ref[...]
        hi = xn.astype(BF16)
        lo = (xn - hi.astype(F32)).astype(BF16)
        logits = _dot(hi, whi_ref[...]) + _dot(lo, whi_ref[...]) + _dot(hi, wlo_ref[...])
        lt = jnp.where(row < n_experts, logits.T[0:e_rows, :], NEG_BIG)
        m1 = jnp.max(lt, axis=0, keepdims=True)
        i1 = jnp.min(jnp.where(lt == m1, row_f, float(e_rows)), axis=0, keepdims=True)
        rest = jnp.where(row_f == i1, NEG_BIG, lt)
        m2 = jnp.max(rest, axis=0, keepdims=True)
        i2 = jnp.min(jnp.where(rest == m2, row_f, float(e_rows)), axis=0, keepdims=True)
        e2 = jnp.exp(m2 - m1)
        g1 = 1.0 / (1.0 + e2)
        g2 = e2 / (1.0 + e2)

        tok = lax.broadcasted_iota(jnp.int32, (1, sub), 1) + (i * tm + r0)
        valid = tok < t_real
        sel = jnp.where(valid & ((row_f == i1) | (row_f == i2)), 1.0, 0.0)
        rank = count + _dot(sel.astype(BF16), earlier_ref[...])
        r1 = jnp.sum(jnp.where(row_f == i1, rank, 0.0), axis=0, keepdims=True)
        r2 = jnp.sum(jnp.where(row_f == i2, rank, 0.0), axis=0, keepdims=True)
        count = count + jnp.sum(sel, axis=1, keepdims=True)

        rec = jnp.zeros((LANES, sub), F32)
        for k, v in ((ROUTE_E1, i1), (ROUTE_E2, i2), (ROUTE_R1, r1), (ROUTE_R2, r2), (ROUTE_G1, g1),
                     (ROUTE_G2, g2)):
            rec = jnp.where(field == k, v, rec)
        route_ref[rows, :] = jnp.where(valid, rec, 0.0).T
    count_ref[...] = jnp.broadcast_to(count, count_ref.shape)


def _router(x, y, wo, nw, w_router, tm, sub, t_real):
    t, d = x.shape
    e = w_router.shape[1]
    wpad = jnp.zeros((d, LANES), F32).at[:, :e].set(w_router)
    whi = wpad.astype(BF16)
    wlo = (wpad - whi.astype(F32)).astype(BF16)
    e_rows = -(-e // SUBLANES) * SUBLANES
    earlier = jnp.asarray(np.arange(sub)[:, None] < np.arange(sub)[None, :], BF16)
    return pl.pallas_call(
        functools.partial(_router_kernel, n_experts=e, t_real=t_real),
        grid=(t // tm,),
        in_specs=[pl.BlockSpec((tm, d), lambda i: (i, 0)),
                  pl.BlockSpec((tm, y.shape[1]), lambda i: (i, 0)),
                  pl.BlockSpec(wo.shape, lambda i: (0, 0)),
                  pl.BlockSpec((1, d), lambda i: (0, 0)),
                  pl.BlockSpec((d, LANES), lambda i: (0, 0)),
                  pl.BlockSpec((d, LANES), lambda i: (0, 0)),
                  pl.BlockSpec((sub, sub), lambda i: (0, 0))],
        out_specs=[pl.BlockSpec((tm, d), lambda i: (i, 0)), pl.BlockSpec((tm, LANES), lambda i: (i, 0)),
                   pl.BlockSpec((e_rows, LANES), lambda i: (0, 0))],
        out_shape=[jax.ShapeDtypeStruct((t, d), F32), jax.ShapeDtypeStruct((t, LANES), F32),
                   jax.ShapeDtypeStruct((e_rows, LANES), F32)],
        compiler_params=pltpu.CompilerParams(dimension_semantics=("arbitrary",),
                                             vmem_limit_bytes=VMEM_LIMIT),
    )(x, y, wo, nw.reshape(1, d), whi, wlo, earlier)


def _row_copy(src, src_row, dst, dst_row, sem):
    return pltpu.make_async_copy(src.at[pl.ds(src_row, 1)], dst.at[pl.ds(dst_row, 1)], sem)


def _dispatch_kernel(pos_ref, x_ref, xs_hbm, sem, *, tm, top_k):
    i = pl.program_id(0)

    def start(g, carry):
        r0 = pl.multiple_of(g * SUBLANES, SUBLANES)
        for b in range(SUBLANES):
            for k in range(top_k):
                _row_copy(x_ref, r0 + b, xs_hbm, pos_ref[(i * tm + r0 + b) * top_k + k], sem).start()
        return carry

    def wait(g, carry):
        for _ in range(SUBLANES * top_k):
            _row_copy(x_ref, 0, xs_hbm, 0, sem).wait()
        return carry

    lax.fori_loop(0, tm // SUBLANES, start, 0)
    lax.fori_loop(0, tm // SUBLANES, wait, 0)


def _dispatch(pos, x, n_rows_out, tm, top_k):
    t, d = x.shape
    return pl.pallas_call(
        functools.partial(_dispatch_kernel, tm=tm, top_k=top_k),
        grid_spec=pltpu.PrefetchScalarGridSpec(
            num_scalar_prefetch=1, grid=(t // tm,),
            in_specs=[pl.BlockSpec((tm, d), lambda i, p: (i, 0))],
            out_specs=pl.BlockSpec(memory_space=pl.ANY),
            scratch_shapes=[pltpu.SemaphoreType.DMA(())]),
        out_shape=jax.ShapeDtypeStruct((n_rows_out, d), F32),
        compiler_params=pltpu.CompilerParams(dimension_semantics=("arbitrary",),
                                             vmem_limit_bytes=VMEM_LIMIT),
    )(pos, x)


def _expert_kernel(blk_e_ref, blk_n_ref, xs_ref, nw_ref, wg_ref, wu_ref, wd_ref, o_ref, xn_scr, acc_scr):
    del blk_e_ref
    j = pl.program_id(0)
    f = pl.program_id(1)
    n_valid = blk_n_ref[j]
    last = f == pl.num_programs(1) - 1

    @pl.when((n_valid > 0) & (f == 0))
    def _():
        row = lax.broadcasted_iota(jnp.int32, (xs_ref.shape[0], 1), 0)
        x = jnp.where(row < n_valid, xs_ref[...], 0.0)
        xn_scr[...] = (_rms(x) * nw_ref[...]).astype(BF16)
        acc_scr[...] = jnp.zeros_like(acc_scr)

    @pl.when(n_valid > 0)
    def _():
        xn = xn_scr[...]
        h = _silu(_dot(xn, wg_ref[...])) * _dot(xn, wu_ref[...])
        acc_scr[...] += _dot(h.astype(BF16), wd_ref[...])

    @pl.when((n_valid > 0) & last)
    def _():
        o_ref[...] = acc_scr[...]

    @pl.when((n_valid == 0) & last)
    def _():
        o_ref[...] = jnp.zeros_like(o_ref)


def _experts(blk_e, blk_n, xs, nw, wg, wu, wd, tmb, tf):
    d = xs.shape[1]
    s = blk_e.shape[0] * tmb
    ff = wg.shape[2]
    nf = ff // tf

    def w_up(j, f, be, bn):
        return (be[j], 0, jnp.where(bn[j] > 0, f, nf - 1))

    def w_down(j, f, be, bn):
        return (be[j], jnp.where(bn[j] > 0, f, nf - 1), 0)

    return pl.pallas_call(
        _expert_kernel,
        grid_spec=pltpu.PrefetchScalarGridSpec(
            num_scalar_prefetch=2, grid=(s // tmb, nf),
            in_specs=[pl.BlockSpec((tmb, d), lambda j, f, be, bn: (j, 0)),
                      pl.BlockSpec((1, d), lambda j, f, be, bn: (0, 0)),
                      pl.BlockSpec((None, d, tf), w_up),
                      pl.BlockSpec((None, d, tf), w_up),
                      pl.BlockSpec((None, tf, d), w_down)],
            out_specs=pl.BlockSpec((tmb, d), lambda j, f, be, bn: (j, 0)),
            scratch_shapes=[pltpu.VMEM((tmb, d), BF16), pltpu.VMEM((tmb, d), F32)]),
        out_shape=jax.ShapeDtypeStruct((s, d), F32),
        compiler_params=pltpu.CompilerParams(dimension_semantics=("parallel", "arbitrary"),
                                             vmem_limit_bytes=VMEM_LIMIT),
    )(blk_e, blk_n, xs, nw.reshape(1, d), wg, wu, wd)


def _combine_kernel(pos_ref, x_ref, route_ref, nw_ref, ys_hbm, op_ref, os_ref, ybuf, sem, *, tm, top_k,
                    n_prompt_tiles):
    i = pl.program_id(0)

    def start(g, carry):
        r0 = pl.multiple_of(g * SUBLANES, SUBLANES)
        for b in range(SUBLANES):
            for k in range(top_k):
                _row_copy(ys_hbm, pos_ref[(i * tm + r0 + b) * top_k + k], ybuf.at[k], r0 + b, sem).start()
        return carry

    def wait(g, carry):
        for _ in range(SUBLANES * top_k):
            _row_copy(ys_hbm, 0, ybuf.at[0], 0, sem).wait()
        return carry

    lax.fori_loop(0, tm // SUBLANES, start, 0)
    lax.fori_loop(0, tm // SUBLANES, wait, 0)
    route = route_ref[...]
    out = x_ref[...] + route[:, ROUTE_G1:ROUTE_G1 + 1] * ybuf[0] + route[:, ROUTE_G2:ROUTE_G2 + 1] * ybuf[1]
    y = _rms(out) * nw_ref[...]

    @pl.when(i < n_prompt_tiles)
    def _():
        op_ref[...] = y

    @pl.when(i >= n_prompt_tiles)
    def _():
        os_ref[...] = y


def _combine(pos, x, route, nw, ys, tm, top_k, tp, ts):
    d = x.shape[1]
    assert top_k == 2 and tp % tm == 0 and ts % tm == 0
    npt, nst = tp // tm, ts // tm
    return pl.pallas_call(
        functools.partial(_combine_kernel, tm=tm, top_k=top_k, n_prompt_tiles=npt),
        grid_spec=pltpu.PrefetchScalarGridSpec(
            num_scalar_prefetch=1, grid=(npt + nst,),
            in_specs=[pl.BlockSpec((tm, d), lambda i, p: (i, 0)),
                      pl.BlockSpec((tm, LANES), lambda i, p: (i, 0)),
                      pl.BlockSpec((1, d), lambda i, p: (0, 0)),
                      pl.BlockSpec(memory_space=pl.ANY)],
            out_specs=[pl.BlockSpec((tm, d), lambda i, p: (jnp.minimum(i, npt - 1), 0)),
                       pl.BlockSpec((tm, d), lambda i, p: (jnp.maximum(i - npt, 0), 0))],
            scratch_shapes=[pltpu.VMEM((top_k, tm, d), F32), pltpu.SemaphoreType.DMA(())]),
        out_shape=[jax.ShapeDtypeStruct((tp, d), F32), jax.ShapeDtypeStruct((ts, d), F32)],
        compiler_params=pltpu.CompilerParams(dimension_semantics=("arbitrary",),
                                             vmem_limit_bytes=VMEM_LIMIT),
    )(pos, x, route, nw.reshape(1, d), ys)


def _moe_plan(route, counts, n_experts, tmb, n_blocks, top_k, t_real):
    cnt = counts[:n_experts, 0].astype(jnp.int32)
    nblk = (cnt + tmb - 1) // tmb
    blk_end = jnp.cumsum(nblk)
    blk_start = blk_end - nblk
    slot0 = blk_start * tmb
    e_idx = route[:, ROUTE_E1:ROUTE_E1 + top_k].astype(jnp.int32)
    rank = route[:, ROUTE_R1:ROUTE_R1 + top_k].astype(jnp.int32)
    onehot = e_idx[:, :, None] == jnp.arange(n_experts)[None, None, :]
    pos = jnp.sum(jnp.where(onehot, slot0[None, None, :], 0), axis=-1) + rank
    j = jnp.arange(n_blocks)
    used = j < blk_end[-1]
    blk_e = jnp.minimum(jnp.sum(j[:, None] >= blk_end[None, :], axis=1), n_experts - 1)
    last_e = jnp.max(jnp.where(nblk > 0, jnp.arange(n_experts), 0))
    blk_e = jnp.where(used, blk_e, last_e).astype(jnp.int32)
    blk_n = jnp.where(used, jnp.clip(cnt[blk_e] - (j - blk_start[blk_e]) * tmb, 0, tmb), 0).astype(jnp.int32)
    t = route.shape[0]
    tok = jnp.arange(t)[:, None]
    is_pad = tok >= t_real
    scratch = n_blocks * tmb + (tok - t_real) * top_k + jnp.arange(top_k)[None, :]
    return jnp.where(is_pad, scratch, pos).reshape(-1).astype(jnp.int32), blk_e, blk_n


class _Group:
    def __init__(self, row0, nb, nc, c, out_rows=None):
        self.row0, self.nb, self.nc, self.c = row0, nb, nc, c
        self.out_rows = c if out_rows is None else out_rows
        assert self.out_rows == c or (nb == 1 and nc == 1 and row0 % self.out_rows == 0)

    def rows(self, width, col_block):
        base, nc, c = self.row0 // self.c, self.nc, self.c
        return pl.BlockSpec((c, width), lambda b, i: (base + b * nc + i, col_block))

    def out_block(self, width):
        base, nc, r = self.row0 // self.out_rows, self.nc, self.out_rows
        return pl.BlockSpec((r, width), lambda b, i: (base + b * nc + i, 0))

    def chunk_rows(self, width):
        return pl.BlockSpec((self.c, width), lambda b, i: (i, 0))


def _full(shape):
    nd = len(shape)
    return pl.BlockSpec(tuple(shape), lambda b, i: (0,) * nd)


def _per_stream(shape):
    nd = len(shape)
    return pl.BlockSpec((None,) + tuple(shape[1:]), lambda b, i: (b,) + (0,) * (nd - 1))


def _scan_call(kernel, group, row_inputs, const_inputs, stream_inputs, y_prev, t_pad, width, extra_out_shapes,
               extra_out_specs, scratch_shapes):
    arrays, specs = [], []
    for arr, w, cb in row_inputs:
        arrays.append(arr)
        specs.append(group.rows(w, cb))
    for item in const_inputs:
        if isinstance(item, tuple):
            arr, spec = item
        else:
            arr, spec = item, _full(item.shape)
        arrays.append(arr)
        specs.append(spec)
    for arr in stream_inputs:
        arrays.append(arr)
        specs.append(_per_stream(arr.shape))
    aliases = {}
    if y_prev is not None:
        aliases = {len(arrays): 0}
        arrays.append(y_prev)
        specs.append(pl.BlockSpec(memory_space=pl.ANY))
    out_shape = [jax.ShapeDtypeStruct((t_pad, width), BF16)] + list(extra_out_shapes)
    out_specs = [group.out_block(width)] + list(extra_out_specs)
    return pl.pallas_call(
        functools.partial(kernel, c=group.c, nc=group.nc, has_prev=y_prev is not None),
        grid=(group.nb, group.nc),
        in_specs=specs,
        out_specs=out_specs,
        out_shape=out_shape,
        scratch_shapes=scratch_shapes,
        input_output_aliases=aliases,
        compiler_params=pltpu.CompilerParams(dimension_semantics=("parallel", "arbitrary"),
                                             vmem_limit_bytes=VMEM_LIMIT),
    )(*arrays)


def _store_rows(y_ref, y):
    c = y.shape[0]
    y_ref[0:c, :] = y
    if y_ref.shape[0] > c:
        y_ref[c:, :] = jnp.zeros((y_ref.shape[0] - c, y_ref.shape[1]), y_ref.dtype)


def _ssd_kernel(*refs, c, nc, has_prev):
    (z_ref, x_ref, bc_ref, dt_ref, convw_ref, convb_ref, dtb_ref, alog_ref, dskip_ref, nw_ref, expand_ref,
     hist0_ref, s0_ref) = refs[:13]
    y_ref, sout_ref, hout_ref, hist_scr, s_ref = refs[13 + int(has_prev):]
    i = pl.program_id(1)
    d_inner = x_ref.shape[1]
    n_state = SSD_STATE
    n_pairs = d_inner // LANES
    pairs_per_group = n_pairs // SSD_GROUPS
    n_hist = SSD_CONV - 1

    @pl.when(i == 0)
    def _():
        for p in range(n_pairs):
            s_ref[p] = jnp.concatenate([s0_ref[2 * p], s0_ref[2 * p + 1]], axis=1)
        hist_scr[HIST_ROWS - n_hist:HIST_ROWS, :] = hist0_ref[...]

    def chunk():
        hist_scr[HIST_ROWS:HIST_ROWS + c, 0:d_inner] = x_ref[...].astype(F32)
        hist_scr[HIST_ROWS:HIST_ROWS + c, d_inner:] = bc_ref[...].astype(F32)
        conv = convb_ref[...]
        for j in range(SSD_CONV):
            conv = conv + hist_scr[HIST_ROWS - n_hist + j:HIST_ROWS - n_hist + j + c, :] * convw_ref[j:j + 1, :]
        tail = hist_scr[HIST_ROWS + c - n_hist:HIST_ROWS + c, :]
        hist_scr[HIST_ROWS - n_hist:HIST_ROWS, :] = tail
        hout_ref[...] = tail
        xbc = _silu(conv)
        xs = xbc[:, 0:d_inner]
        bm = xbc[:, d_inner:d_inner + SSD_GROUPS * n_state]
        cm = xbc[:, d_inner + SSD_GROUPS * n_state:]

        x_dt = dt_ref[...] + dtb_ref[...]
        dt = jnp.maximum(x_dt, 0.0) + jnp.log1p(jnp.exp(-jnp.abs(x_dt)))
        log_a = dt * (-jnp.exp(alog_ref[...]))
        cum = _cumsum_time(log_a)
        expand = expand_ref[...]
        dt_e = _exact_ldot(dt, expand)
        cum_e = _exact_ldot(cum, expand)
        last_e = cum_e[c - 1:c, :]
        xdt = xs * dt_e
        wx = jnp.exp(last_e - cum_e) * xdt
        ecum = jnp.exp(cum_e)
        sdecay = jnp.exp(last_e)

        row = lax.broadcasted_iota(jnp.int32, (c, 2 * c), 0)
        col = lax.broadcasted_iota(jnp.int32, (c, 2 * c), 1)
        second = col >= c
        tcol = jnp.where(second, col - c, col)
        causal2 = tcol <= row
        diag2 = tcol == row
        r2 = lax.broadcasted_iota(jnp.int32, (2 * c, LANES), 0)
        l2 = lax.broadcasted_iota(jnp.int32, (2 * c, LANES), 1)
        half2 = (r2 >= c) == (l2 >= SSD_HEADDIM)

        ys = []
        for g in range(SSD_GROUPS):
            cg = cm[:, g * n_state:(g + 1) * n_state].astype(BF16)
            bg = bm[:, g * n_state:(g + 1) * n_state].astype(BF16)
            scores2 = _dot_nt(cg, jnp.concatenate([bg, bg], axis=0))
            for p in range(g * pairs_per_group, (g + 1) * pairs_per_group):
                sl = slice(p * LANES, (p + 1) * LANES)
                cum_p = cum_e[:, sl]
                cum_col = jnp.where(second, cum_p[:, SSD_HEADDIM:SSD_HEADDIM + 1], cum_p[:, 0:1])
                cum_row = jnp.sum(jnp.where(diag2, cum_col, 0.0), axis=0, keepdims=True)
                seg = cum_col - cum_row
                m2 = (scores2 * jnp.exp(jnp.where(causal2, seg, NEG_BIG))).astype(BF16)
                xp = xdt[:, sl]
                x2 = jnp.where(half2, jnp.concatenate([xp, xp], axis=0), 0.0).astype(BF16)
                y = _dot(m2, x2)
                y = y + _dot(cg, s_ref[p].astype(BF16)) * ecum[:, sl]
                s_ref[p] = s_ref[p] * sdecay[:, sl] + _dot_tn(bg, wx[:, sl].astype(BF16))
                ys.append(y)
        y = jnp.concatenate(ys, axis=1)
        y = y + xs * dskip_ref[...]
        y = y * _silu(z_ref[...].astype(F32))
        gw = d_inner // SSD_GROUPS
        y = jnp.concatenate([_rms(y[:, g * gw:(g + 1) * gw]) for g in range(SSD_GROUPS)], axis=1)
        _store_rows(y_ref, (y * nw_ref[...]).astype(BF16))

        @pl.when(i == nc - 1)
        def _():
            for p in range(n_pairs):
                sout_ref[2 * p] = s_ref[p][:, 0:SSD_HEADDIM]
                sout_ref[2 * p + 1] = s_ref[p][:, SSD_HEADDIM:]

    chunk()


def _ret_kernel(*refs, c, nc, has_prev):
    (q_ref, k_ref, v_ref, g_ref, sin_ref, cos_ref, dmat_ref, ecum_ref, wend_ref, sdec_ref, s0_ref) = refs[:11]
    y_ref, s_ref = refs[11 + int(has_prev):]
    i = pl.program_id(1)
    n_heads = q_ref.shape[1] // LANES

    @pl.when(i == 0)
    def _():
        s_ref[...] = s0_ref[...]

    def chunk():
        sin = sin_ref[...]
        cos = cos_ref[...]
        even = (lax.broadcasted_iota(jnp.int32, (c, LANES), 1) % 2) == 0

        def rotate(x):
            nxt = pltpu.roll(x, LANES - 1, 1)
            prv = pltpu.roll(x, 1, 1)
            return x * cos + jnp.where(even, -nxt, prv) * sin

        scale = LANES ** -0.5
        ys = []
        for h in range(n_heads):
            sl = slice(h * LANES, (h + 1) * LANES)
            qh = rotate(q_ref[:, sl].astype(F32))
            kh = rotate(k_ref[:, sl].astype(F32)) * scale
            vh = v_ref[:, sl]
            scores = _dot_nt(qh.astype(BF16), kh.astype(BF16)) * dmat_ref[h]
            y = _dot(scores.astype(BF16), vh) + _dot((qh * ecum_ref[:, sl]).astype(BF16), s_ref[h].astype(BF16))
            s_ref[h] = s_ref[h] * sdec_ref[:, sl] + _dot_tn((kh * wend_ref[:, sl]).astype(BF16), vh)
            ys.append(_rms(y) * _silu(g_ref[:, sl].astype(F32)))
        _store_rows(y_ref, jnp.concatenate(ys, axis=1).astype(BF16))

    chunk()


def _hgrn_kernel(*refs, c, nc, has_prev):
    (q_ref, f_ref, v_ref, g_ref, lb_ref, nw_ref, s0_ref) = refs[:7]
    y_ref, sout_ref, s_ref = refs[7 + int(has_prev):]
    i = pl.program_id(1)
    n_heads = q_ref.shape[1] // LANES

    @pl.when(i == 0)
    def _():
        for h in range(n_heads):
            s_ref[h] = s0_ref[h].T

    def chunk():
        lb = lb_ref[...]
        forget = lb + (1.0 - lb) * _sigmoid(f_ref[...])
        kk = 1.0 - forget
        gc = _cumsum_time(jnp.log(forget))
        last = gc[c - 1:c, :]
        qg = (_silu(q_ref[...].astype(F32)) * jnp.exp(gc)).astype(BF16)
        kg = (kk * jnp.exp(-gc)).astype(BF16)
        kw = (kk * jnp.exp(last - gc)).astype(BF16)
        sdec = jnp.exp(last)
        causal = _causal(c)
        ys = []
        for h in range(n_heads):
            sl = slice(h * LANES, (h + 1) * LANES)
            vh = v_ref[:, sl]
            scores = jnp.where(causal, _dot_nt(qg[:, sl], kg[:, sl]), 0.0)
            y = _dot(scores.astype(BF16), vh) + _dot_nt(qg[:, sl], s_ref[h].astype(BF16))
            s_ref[h] = s_ref[h] * sdec[:, sl] + _dot_tn(vh, kw[:, sl])
            ys.append(_rms(y) * nw_ref[...] * _silu(g_ref[:, sl].astype(F32)))
        _store_rows(y_ref, jnp.concatenate(ys, axis=1).astype(BF16))

        @pl.when(i == nc - 1)
        def _():
            for h in range(n_heads):
                sout_ref[h] = s_ref[h].T

    chunk()


def _rotation_tables(pos, dk):
    inv = 1.0 / (ROPE_BASE ** jnp.linspace(0.0, 1.0, dk // 2, dtype=F32))
    ang = pos.astype(F32)[:, None] * jnp.repeat(inv, 2)[None, :]
    return jnp.sin(ang), jnp.cos(ang)


def _retention_decay(c, n_heads, dv):
    log_gamma = jnp.log1p(-(2.0 ** (-5.0 - jnp.arange(n_heads, dtype=F32))))
    cum = jnp.cumsum(jnp.broadcast_to(log_gamma, (c, n_heads)), axis=0)
    cum_h = cum.T
    causal = jnp.tril(jnp.ones((c, c), dtype=bool))
    dmat = jnp.exp(jnp.where(causal, cum_h[:, :, None] - cum_h[:, None, :], -jnp.inf))
    ecum = jnp.repeat(jnp.exp(cum), dv, axis=1)
    wend = jnp.repeat(jnp.exp(cum[-1][None, :] - cum), dv, axis=1)
    sdec = jnp.repeat(jnp.exp(cum[-1])[None, :], dv, axis=1)
    return dmat, ecum, wend, sdec


EXPERT_FF_TILE_MAX = 1792
FFN_FF_TILE_MAX = 1408


def _largest_tile(n, cap):
    best = None
    for k in range(1, n // LANES + 1):
        if n % k == 0 and (n // k) % LANES == 0 and n // k <= cap:
            best = n // k
            break
    return n if best is None else best


def kernel(x_prompt, x_sample, state_ssd, state_ssd_conv, state_ret, state_hgrn, meta_tokens, norm_mix, norm_ffn,
           norm_final, w_in_ab, conv_w, conv_b, dt_bias, a_log, d_skip, ssd_norm, w_out_ab, w_ffn_gate, w_ffn_up,
           w_ffn_down, w_in_c, hgrn_lb, hgrn_norm, w_out_c, w_router, w_exp_gate, w_exp_up, w_exp_down):
    bp, seq, d = x_prompt.shape
    bs, dec_seq, _ = x_sample.shape
    n_meta = meta_tokens.shape[0]
    depth = norm_mix.shape[0]
    assert depth == 2 and seq % CHUNK == 0 and n_meta == N_META and d % LANES == 0
    n_ssd_heads = d_skip.shape[1]
    d_inner = n_ssd_heads * SSD_HEADDIM
    bc_w = 2 * SSD_GROUPS * SSD_STATE
    conv_dim = d_inner + bc_w
    ret_w = RET_HEADS * LANES
    assert d_inner == d and ret_w == d and conv_dim == conv_w.shape[2]

    tp, ts = bp * seq, bs * dec_seq
    t_real = tp + ts + n_meta
    tm = 512 if t_real >= 4096 else 64
    t_pad = -(-t_real // tm) * tm
    tm_big = 2 * tm if t_pad % (2 * tm) == 0 else tm
    g_prompt = _Group(0, bp, seq // CHUNK, CHUNK)
    g_sample = _Group(tp, bs, 1, dec_seq)
    g_meta = _Group(tp + ts, 1, 1, n_meta, out_rows=t_pad - tp - ts)
    assert tp % dec_seq == 0 and (tp + ts) % n_meta == 0

    w_in = w_in_ab[0]
    o_z, o_xbc, o_dt, o_q = 0, d_inner, d_inner + conv_dim, d_inner + conv_dim + n_ssd_heads
    n_cols = 6 * d + bc_w
    tn = n_cols // 4 if n_cols % (4 * LANES) == 0 else n_cols
    w_perm = jnp.concatenate([
        w_in[:, o_z:o_z + d_inner], w_in[:, o_q:o_q + 4 * ret_w], w_in[:, o_xbc:o_xbc + conv_dim]],
        axis=1).astype(BF16)
    w_dt = jnp.pad(w_in[:, o_dt:o_dt + n_ssd_heads], ((0, 0), (0, LANES - n_ssd_heads))).astype(BF16)
    proj, proj_dt, x0 = _normmm((x_prompt.reshape(tp, d), x_sample.reshape(ts, d), meta_tokens), norm_mix[0],
                                w_perm, w_dt, tm_big, tn, t_pad)
    cb_x, cb_bc = 5, (6 * d) // bc_w
    assert (6 * d) % bc_w == 0

    expand = jnp.asarray(np.arange(LANES)[:, None] == (np.arange(d_inner) // SSD_HEADDIM)[None, :], BF16)
    pad_h = LANES - n_ssd_heads
    ssd_consts = [conv_w[0], conv_b[0].reshape(1, conv_dim), jnp.pad(dt_bias[0], (0, pad_h)).reshape(1, LANES),
                  jnp.pad(a_log[0], (0, pad_h)).reshape(1, LANES),
                  jnp.repeat(d_skip[0], SSD_HEADDIM).reshape(1, d_inner), ssd_norm[0].reshape(1, d_inner), expand]
    def ssd(group, hist0, s0, y_prev):
        nb = group.nb
        return _scan_call(
            _ssd_kernel, group,
            [(proj, d_inner, 0), (proj, d_inner, cb_x), (proj, bc_w, cb_bc), (proj_dt, LANES, 0)],
            ssd_consts, [hist0, s0], y_prev, t_pad, d_inner,
            [jax.ShapeDtypeStruct((nb, n_ssd_heads, SSD_STATE, SSD_HEADDIM), F32),
             jax.ShapeDtypeStruct((nb, SSD_CONV - 1, conv_dim), F32)],
            [_per_stream((nb, n_ssd_heads, SSD_STATE, SSD_HEADDIM)), _per_stream((nb, SSD_CONV - 1, conv_dim))],
            [pltpu.VMEM((HIST_ROWS + group.c, conv_dim), F32),
             pltpu.VMEM((n_ssd_heads // 2, SSD_STATE, LANES), F32)])

    def bcast(s, n):
        return jnp.broadcast_to(s, (n,) + s.shape[1:])

    zeros = jnp.zeros
    y_ssd, s_m, h_m = ssd(g_meta, zeros((1, SSD_CONV - 1, conv_dim), F32),
                          zeros((1, n_ssd_heads, SSD_STATE, SSD_HEADDIM), F32), None)
    y_ssd, ssd_p, conv_p = ssd(g_prompt, bcast(h_m, bp), bcast(s_m, bp), y_ssd)
    y_ssd, ssd_s, conv_s = ssd(g_sample, state_ssd_conv[0], state_ssd[0], y_ssd)

    def ret(group, pos, s0, y_prev):
        nb = group.nb
        sin, cos = _rotation_tables(pos, LANES)
        dmat, ecum, wend, sdec = _retention_decay(group.c, RET_HEADS, LANES)
        return _scan_call(
            _ret_kernel, group,
            [(proj, ret_w, 1), (proj, ret_w, 2), (proj, ret_w, 3), (proj, ret_w, 4)],
            [(sin, group.chunk_rows(LANES)), (cos, group.chunk_rows(LANES)), dmat, ecum, wend, sdec],
            [s0], y_prev, t_pad, ret_w,
            [jax.ShapeDtypeStruct((nb, RET_HEADS, LANES, LANES), F32)],
            [_per_stream((nb, RET_HEADS, LANES, LANES))], [])

    y_ret, r_m = ret(g_meta, jnp.arange(n_meta), zeros((1, RET_HEADS, LANES, LANES), F32), None)
    y_ret, ret_p = ret(g_prompt, n_meta + jnp.arange(seq), bcast(r_m, bp), y_ret)
    y_ret, ret_s = ret(g_sample, n_meta + PAST_LEN + jnp.arange(dec_seq), state_ret[0], y_ret)

    w_out = w_out_ab[0].astype(BF16)
    ff = w_ffn_gate.shape[2]
    tf = _largest_tile(ff, FFN_FF_TILE_MAX)
    x2 = _ffn(x0, [y_ssd, y_ret], [w_out[:d_inner], w_out[d_inner:]], norm_ffn[0], w_ffn_gate[0].astype(BF16),
              w_ffn_up[0].astype(BF16), w_ffn_down[0].astype(BF16), tm, tf)

    w_c = w_in_c[0].astype(BF16)
    w_qig = jnp.concatenate([w_c[:, :d], w_c[:, 2 * d:]], axis=1)
    proj_c, proj_f = _normmm(x2, norm_mix[1], w_qig, w_c[:, d:2 * d], tm_big, 3 * d)
    lb_soft = jax.nn.softmax(hgrn_lb.astype(F32), axis=0)
    lb = (jnp.cumsum(lb_soft, axis=0) - lb_soft[0])[1].reshape(1, d)
    n_hg = d // HG_DK
    hg_consts = [lb, hgrn_norm[0].reshape(1, LANES)]

    def hgrn(group, s0, y_prev):
        nb = group.nb
        return _scan_call(
            _hgrn_kernel, group,
            [(proj_c, d, 0), (proj_f, d, 0), (proj_c, d, 1), (proj_c, d, 2)],
            hg_consts, [s0], y_prev, t_pad, d,
            [jax.ShapeDtypeStruct((nb, n_hg, HG_DK, LANES), F32)],
            [_per_stream((nb, n_hg, HG_DK, LANES))], [pltpu.VMEM((n_hg, LANES, HG_DK), F32)])

    y_hg, g_m = hgrn(g_meta, zeros((1, n_hg, HG_DK, LANES), F32), None)
    y_hg, hg_p = hgrn(g_prompt, bcast(g_m, bp), y_hg)
    y_hg, hg_s = hgrn(g_sample, state_hgrn[0], y_hg)

    n_exp = w_router.shape[2]
    x3, route, counts = _router(x2, y_hg, w_out_c[0].astype(BF16), norm_ffn[1], w_router[0], tm_big, tm, t_real)
    tmb = tm
    n_blocks = (TOP_K * t_real + n_exp * (tmb - 1)) // tmb
    pos, blk_e, blk_n = _moe_plan(route, counts, n_exp, tmb, n_blocks, TOP_K, t_real)
    xs = _dispatch(pos, x3, n_blocks * tmb + TOP_K * (t_pad - t_real), tm, TOP_K)
    ffe = w_exp_gate.shape[3]
    tfe = _largest_tile(ffe, EXPERT_FF_TILE_MAX)
    ys = _experts(blk_e, blk_n, xs, norm_ffn[1], w_exp_gate[0].astype(BF16), w_exp_up[0].astype(BF16),
                  w_exp_down[0].astype(BF16), tmb, tfe)
    y_prompt, y_sample = _combine(pos, x3, route, norm_final, ys, tm, TOP_K, tp, ts)
    y_prompt = y_prompt.reshape(bp, seq, d)
    y_sample = y_sample.reshape(bs, dec_seq, d)
    return (y_prompt, y_sample, ssd_p[None], conv_p[None], ret_p[None], hg_p[None],
            ssd_s[None], conv_s[None], ret_s[None], hg_s[None])
```

```python
import functools

import jax
import jax.numpy as jnp
import numpy as np
from jax import lax
from jax.experimental import pallas as pl
from jax.experimental.pallas import tpu as pltpu

F32 = jnp.float32
BF16 = jnp.bfloat16

CHUNK = 64
N_META = 16
PAST_LEN = 2048
EPS = 1e-6
SSD_HEADDIM = 64
SSD_GROUPS = 2
SSD_STATE = 128
SSD_CONV = 4
RET_HEADS = 8
ROPE_BASE = 10000.0
HG_DK = 128
TOP_K = 2

LANES = 128
SUBLANES = 8
HIST_ROWS = 8
VMEM_LIMIT = 56 * 1024 * 1024
NEG_BIG = -1e30


def _dot(a, b):
    return jnp.dot(a, b, preferred_element_type=F32)


def _dot_nt(a, b):
    return lax.dot_general(a, b, (((1,), (1,)), ((), ())), preferred_element_type=F32)


def _dot_tn(a, b):
    return lax.dot_general(a, b, (((0,), (0,)), ((), ())), preferred_element_type=F32)


def _split3(x):
    hi = x.astype(BF16)
    r = x - hi.astype(F32)
    mid = r.astype(BF16)
    lo = (r - mid.astype(F32)).astype(BF16)
    return hi, mid, lo


def _exact_ldot(x, m01):
    hi, mid, lo = _split3(x)
    return _dot(hi, m01) + _dot(mid, m01) + _dot(lo, m01)


def _exact_rdot(m01, x):
    hi, mid, lo = _split3(x)
    return _dot(m01, hi) + _dot(m01, mid) + _dot(m01, lo)


def _causal(c):
    row = lax.broadcasted_iota(jnp.int32, (c, c), 0)
    col = lax.broadcasted_iota(jnp.int32, (c, c), 1)
    return row >= col


def _cumsum_time(x):
    c = x.shape[0]
    return _exact_rdot(jnp.where(_causal(c), 1.0, 0.0).astype(BF16), x)


def _sigmoid(x):
    return 1.0 / (1.0 + jnp.exp(-x))


def _silu(x):
    return x * _sigmoid(x)


def _rms(x):
    return x * lax.rsqrt(jnp.mean(x * x, axis=-1, keepdims=True) + EPS)


def _normmm_kernel(*refs, n_src, n_lead_tiles):
    xs, (nw_ref, w_ref, ws_ref, o_ref, os_ref) = refs[:n_src], refs[n_src:n_src + 5]
    xn_scr = refs[-1]

    @pl.when(pl.program_id(1) == 0)
    def _():
        if n_src == 1:
            x_ref = xs[0]
        else:
            lead_ref, tail_ref, meta_ref = xs
            x_ref = refs[n_src + 5]
            i = pl.program_id(0)
            tm = x_ref.shape[0]
            n_tail, n_meta = tail_ref.shape[0], meta_ref.shape[0]

            @pl.when(i < n_lead_tiles)
            def _():
                x_ref[...] = lead_ref[...]

            for t in range(pl.cdiv(n_tail + n_meta, tm)):
                @pl.when(i == n_lead_tiles + t)
                def _(lo=t * tm):
                    a, b = lo, min(lo + tm, n_tail)
                    if a < b:
                        x_ref[a - lo:b - lo, :] = tail_ref[a:b, :]
                    a, b = max(lo, n_tail), min(lo + tm, n_tail + n_meta)
                    if a < b:
                        x_ref[a - lo:b - lo, :] = meta_ref[a - n_tail:b - n_tail, :]
                    if b < lo + tm:
                        x_ref[b - lo:, :] = jnp.zeros((lo + tm - b, x_ref.shape[1]), x_ref.dtype)

        xn = (_rms(x_ref[...]) * nw_ref[...]).astype(BF16)
        xn_scr[...] = xn
        os_ref[...] = _dot(xn, ws_ref[...])

    o_ref[...] = _dot(xn_scr[...], w_ref[...]).astype(BF16)


def _normmm(x, nw, w, w_side, tm, tn, t_pad=None):
    sources = x if isinstance(x, tuple) else (x,)
    k = sources[0].shape[1]
    n, ns = w.shape[1], w_side.shape[1]
    out_specs = [pl.BlockSpec((tm, tn), lambda i, j: (i, j)), pl.BlockSpec((tm, ns), lambda i, j: (i, 0))]
    if len(sources) == 1:
        t, n_lead = x.shape[0], 0
        x_specs = [pl.BlockSpec((tm, k), lambda i, j: (i, 0))]
        out_shape = [jax.ShapeDtypeStruct((t, n), BF16), jax.ShapeDtypeStruct((t, ns), F32)]
    else:
        lead, tail, meta = sources
        t, n_lead = t_pad, lead.shape[0] // tm
        assert lead.shape[0] % tm == 0 and (n_lead + pl.cdiv(tail.shape[0] + meta.shape[0], tm)) * tm == t_pad
        x_specs = [pl.BlockSpec((tm, k), lambda i, j: (jnp.minimum(i, n_lead - 1), 0)),
                   pl.BlockSpec(tail.shape, lambda i, j: (0, 0)),
                   pl.BlockSpec(meta.shape, lambda i, j: (0, 0))]
        out_specs.append(pl.BlockSpec((tm, k), lambda i, j: (i, 0)))
        out_shape = [jax.ShapeDtypeStruct((t, n), BF16), jax.ShapeDtypeStruct((t, ns), F32),
                     jax.ShapeDtypeStruct((t, k), F32)]
    return pl.pallas_call(
        functools.partial(_normmm_kernel, n_src=len(sources), n_lead_tiles=n_lead),
        grid=(t // tm, n // tn),
        in_specs=x_specs + [pl.BlockSpec((1, k), lambda i, j: (0, 0)),
                            pl.BlockSpec((k, tn), lambda i, j: (0, j)),
                            pl.BlockSpec((k, ns), lambda i, j: (0, 0))],
        out_specs=out_specs,
        out_shape=out_shape,
        scratch_shapes=[pltpu.VMEM((tm, k), BF16)],
        compiler_params=pltpu.CompilerParams(dimension_semantics=("parallel", "arbitrary"),
                                             vmem_limit_bytes=VMEM_LIMIT),
    )(*sources, nw.reshape(1, k), w, w_side)


def _regroup_kernel(w_ref, o_ref, os_ref, *, pieces, side_piece):
    col = 0
    for a, b in pieces:
        o_ref[:, col:col + b - a] = w_ref[:, a:b].astype(BF16)
        col += b - a
    a, b = side_piece
    os_ref[...] = jnp.zeros_like(os_ref)
    os_ref[:, 0:b - a] = w_ref[:, a:b].astype(BF16)


def _regroup_columns(w, pieces, side_piece, rows):
    k, n = w.shape
    n_out = sum(b - a for a, b in pieces)
    return pl.pallas_call(
        functools.partial(_regroup_kernel, pieces=pieces, side_piece=side_piece),
        grid=(k // rows,),
        in_specs=[pl.BlockSpec((rows, n), lambda i: (i, 0))],
        out_specs=[pl.BlockSpec((rows, n_out), lambda i: (i, 0)), pl.BlockSpec((rows, LANES), lambda i: (i, 0))],
        out_shape=[jax.ShapeDtypeStruct((k, n_out), BF16), jax.ShapeDtypeStruct((k, LANES), BF16)],
        compiler_params=pltpu.CompilerParams(dimension_semantics=("parallel",), vmem_limit_bytes=VMEM_LIMIT),
    )(w)


def _ffn_kernel(*refs, n_in):
    x_ref = refs[0]
    ys = refs[1:1 + n_in]
    ws = refs[1 + n_in:1 + 2 * n_in]
    nw_ref, wg_ref, wu_ref, wd_ref, o_ref, xn_scr, acc_scr = refs[1 + 2 * n_in:]
    f = pl.program_id(1)

    @pl.when(f == 0)
    def _():
        x = x_ref[...]
        for y, w in zip(ys, ws):
            x = x + _dot(y[...], w[...])
        xn_scr[...] = (_rms(x) * nw_ref[...]).astype(BF16)
        acc_scr[...] = x

    xn = xn_scr[...]
    h = _silu(_dot(xn, wg_ref[...])) * _dot(xn, wu_ref[...])
    acc_scr[...] += _dot(h.astype(BF16), wd_ref[...])

    @pl.when(f == pl.num_programs(1) - 1)
    def _():
        o_ref[...] = acc_scr[...]


def _ffn(x, ys, ws, nw, wg, wu, wd, tm, tf):
    t, d = x.shape
    ff = wg.shape[1]
    n_in = len(ys)
    in_specs = [pl.BlockSpec((tm, d), lambda i, f: (i, 0))]
    in_specs += [pl.BlockSpec((tm, y.shape[1]), lambda i, f: (i, 0)) for y in ys]
    in_specs += [pl.BlockSpec(w.shape, lambda i, f: (0, 0)) for w in ws]
    in_specs += [pl.BlockSpec((1, d), lambda i, f: (0, 0)),
                 pl.BlockSpec((d, tf), lambda i, f: (0, f)),
                 pl.BlockSpec((d, tf), lambda i, f: (0, f)),
                 pl.BlockSpec((tf, d), lambda i, f: (f, 0))]
    return pl.pallas_call(
        functools.partial(_ffn_kernel, n_in=n_in),
        grid=(t // tm, ff // tf),
        in_specs=in_specs,
        out_specs=pl.BlockSpec((tm, d), lambda i, f: (i, 0)),
        out_shape=jax.ShapeDtypeStruct((t, d), F32),
        scratch_shapes=[pltpu.VMEM((tm, d), BF16), pltpu.VMEM((tm, d), F32)],
        compiler_params=pltpu.CompilerParams(dimension_semantics=("parallel", "arbitrary"),
                                             vmem_limit_bytes=VMEM_LIMIT),
    )(x, *ys, *ws, nw.reshape(1, d), wg, wu, wd)


ROUTE_E1, ROUTE_E2, ROUTE_R1, ROUTE_R2, ROUTE_G1, ROUTE_G2 = range(6)


def _router_kernel(x_ref, y_ref, wo_ref, nw_ref, whi_ref, wlo_ref, earlier_ref, xo_ref, route_ref, routet_ref,
                   count_ref, *, n_experts, t_real):
    i = pl.program_id(0)
    tm = x_ref.shape[0]
    sub = earlier_ref.shape[0]
    e_rows = count_ref.shape[0]

    @pl.when(i == 0)
    def _():
        count_ref[...] = jnp.zeros_like(count_ref)

    count = count_ref[:, 0:1]
    row = lax.broadcasted_iota(jnp.int32, (e_rows, sub), 0)
    row_f = row.astype(F32)
    field = lax.broadcasted_iota(jnp.int32, (LANES, sub), 0)
    for r0 in range(0, tm, sub):
        rows = slice(r0, r0 + sub)
        xo = x_ref[rows, :] + _dot(y_ref[rows, :], wo_ref[...])
        xo_ref[rows, :] = xo
        xn = _rms(xo) * nw_ref[...]
        hi = xn.astype(BF16)
        lo = (xn - hi.astype(F32)).astype(BF16)
        logits = _dot(hi, whi_ref[...]) + _dot(lo, whi_ref[...]) + _dot(hi, wlo_ref[...])
        lt = jnp.where(row < n_experts, logits.T[0:e_rows, :], NEG_BIG)
        m1 = jnp.max(lt, axis=0, keepdims=True)
        i1 = jnp.min(jnp.where(lt == m1, row_f, float(e_rows)), axis=0, keepdims=True)
        rest = jnp.where(row_f == i1, NEG_BIG, lt)
        m2 = jnp.max(rest, axis=0, keepdims=True)
        i2 = jnp.min(jnp.where(rest == m2, row_f, float(e_rows)), axis=0, keepdims=True)
        e2 = jnp.exp(m2 - m1)
        g1 = 1.0 / (1.0 + e2)
        g2 = e2 / (1.0 + e2)

        tok = lax.broadcasted_iota(jnp.int32, (1, sub), 1) + (i * tm + r0)
        valid = tok < t_real
        sel = jnp.where(valid & ((row_f == i1) | (row_f == i2)), 1.0, 0.0)
        rank = count + _dot(sel.astype(BF16), earlier_ref[...])
        r1 = jnp.sum(jnp.where(row_f == i1, rank, 0.0), axis=0, keepdims=True)
        r2 = jnp.sum(jnp.where(row_f == i2, rank, 0.0), axis=0, keepdims=True)
        count = count + jnp.sum(sel, axis=1, keepdims=True)

        rec = jnp.zeros((LANES, sub), F32)
        for k, v in ((ROUTE_E1, i1), (ROUTE_E2, i2), (ROUTE_R1, r1), (ROUTE_R2, r2), (ROUTE_G1, g1),
                     (ROUTE_G2, g2)):
            rec = jnp.where(field == k, v, rec)
        rec = jnp.where(valid, rec, 0.0)
        route_ref[rows, :] = rec.T
        routet_ref[:, rows] = rec[0:SUBLANES, :]
    count_ref[...] = jnp.broadcast_to(count, count_ref.shape)


def _router(x, y, wo, nw, w_router, tm, sub, t_real):
    t, d = x.shape
    e = w_router.shape[1]
    wpad = jnp.zeros((d, LANES), F32).at[:, :e].set(w_router)
    whi = wpad.astype(BF16)
    wlo = (wpad - whi.astype(F32)).astype(BF16)
    e_rows = -(-e // SUBLANES) * SUBLANES
    earlier = jnp.asarray(np.arange(sub)[:, None] < np.arange(sub)[None, :], BF16)
    return pl.pallas_call(
        functools.partial(_router_kernel, n_experts=e, t_real=t_real),
        grid=(t // tm,),
        in_specs=[pl.BlockSpec((tm, d), lambda i: (i, 0)),
                  pl.BlockSpec((tm, y.shape[1]), lambda i: (i, 0)),
                  pl.BlockSpec(wo.shape, lambda i: (0, 0)),
                  pl.BlockSpec((1, d), lambda i: (0, 0)),
                  pl.BlockSpec((d, LANES), lambda i: (0, 0)),
                  pl.BlockSpec((d, LANES), lambda i: (0, 0)),
                  pl.BlockSpec((sub, sub), lambda i: (0, 0))],
        out_specs=[pl.BlockSpec((tm, d), lambda i: (i, 0)), pl.BlockSpec((tm, LANES), lambda i: (i, 0)),
                   pl.BlockSpec((SUBLANES, tm), lambda i: (0, i)), pl.BlockSpec((e_rows, LANES), lambda i: (0, 0))],
        out_shape=[jax.ShapeDtypeStruct((t, d), F32), jax.ShapeDtypeStruct((t, LANES), F32),
                   jax.ShapeDtypeStruct((SUBLANES, t), F32), jax.ShapeDtypeStruct((e_rows, LANES), F32)],
        compiler_params=pltpu.CompilerParams(dimension_semantics=("arbitrary",),
                                             vmem_limit_bytes=VMEM_LIMIT),
    )(x, y, wo, nw.reshape(1, d), whi, wlo, earlier)


def _row_copy(src, src_row, dst, dst_row, sem):
    return pltpu.make_async_copy(src.at[pl.ds(src_row, 1)], dst.at[pl.ds(dst_row, 1)], sem)


def _dispatch_kernel(pos_ref, x_ref, xs_hbm, sem, *, tm, top_k, n_tok):
    i = pl.program_id(0)

    def start(g, carry):
        r0 = pl.multiple_of(g * SUBLANES, SUBLANES)
        for b in range(SUBLANES):
            for k in range(top_k):
                _row_copy(x_ref, r0 + b, xs_hbm, pos_ref[k * n_tok + i * tm + r0 + b], sem).start()
        return carry

    def wait(g, carry):
        for _ in range(SUBLANES * top_k):
            _row_copy(x_ref, 0, xs_hbm, 0, sem).wait()
        return carry

    lax.fori_loop(0, tm // SUBLANES, start, 0)
    lax.fori_loop(0, tm // SUBLANES, wait, 0)


def _dispatch(pos, x, n_rows_out, tm, top_k):
    t, d = x.shape
    return pl.pallas_call(
        functools.partial(_dispatch_kernel, tm=tm, top_k=top_k, n_tok=t),
        grid_spec=pltpu.PrefetchScalarGridSpec(
            num_scalar_prefetch=1, grid=(t // tm,),
            in_specs=[pl.BlockSpec((tm, d), lambda i, p: (i, 0))],
            out_specs=pl.BlockSpec(memory_space=pl.ANY),
            scratch_shapes=[pltpu.SemaphoreType.DMA(())]),
        out_shape=jax.ShapeDtypeStruct((n_rows_out, d), F32),
        compiler_params=pltpu.CompilerParams(dimension_semantics=("arbitrary",),
                                             vmem_limit_bytes=VMEM_LIMIT),
    )(pos, x)


def _expert_kernel(blk_e_ref, blk_n_ref, xs_ref, nw_ref, wg_ref, wu_ref, wd_ref, o_ref, xn_scr, acc_scr):
    del blk_e_ref
    j = pl.program_id(0)
    f = pl.program_id(1)
    n_valid = blk_n_ref[j]
    last = f == pl.num_programs(1) - 1

    @pl.when((n_valid > 0) & (f == 0))
    def _():
        row = lax.broadcasted_iota(jnp.int32, (xs_ref.shape[0], 1), 0)
        x = jnp.where(row < n_valid, xs_ref[...], 0.0)
        xn_scr[...] = (_rms(x) * nw_ref[...]).astype(BF16)
        acc_scr[...] = jnp.zeros_like(acc_scr)

    @pl.when(n_valid > 0)
    def _():
        xn = xn_scr[...]
        h = _silu(_dot(xn, wg_ref[...])) * _dot(xn, wu_ref[...])
        acc_scr[...] += _dot(h.astype(BF16), wd_ref[...])

    @pl.when((n_valid > 0) & last)
    def _():
        o_ref[...] = acc_scr[...]

    @pl.when((n_valid == 0) & last)
    def _():
        o_ref[...] = jnp.zeros_like(o_ref)


def _experts(blk_e, blk_n, xs, nw, wg, wu, wd, tmb, tf):
    d = xs.shape[1]
    s = blk_e.shape[0] * tmb
    ff = wg.shape[2]
    nf = ff // tf

    def w_up(j, f, be, bn):
        return (be[j], 0, jnp.where(bn[j] > 0, f, nf - 1))

    def w_down(j, f, be, bn):
        return (be[j], jnp.where(bn[j] > 0, f, nf - 1), 0)

    return pl.pallas_call(
        _expert_kernel,
        grid_spec=pltpu.PrefetchScalarGridSpec(
            num_scalar_prefetch=2, grid=(s // tmb, nf),
            in_specs=[pl.BlockSpec((tmb, d), lambda j, f, be, bn: (j, 0)),
                      pl.BlockSpec((1, d), lambda j, f, be, bn: (0, 0)),
                      pl.BlockSpec((None, d, tf), w_up),
                      pl.BlockSpec((None, d, tf), w_up),
                      pl.BlockSpec((None, tf, d), w_down)],
            out_specs=pl.BlockSpec((tmb, d), lambda j, f, be, bn: (j, 0)),
            scratch_shapes=[pltpu.VMEM((tmb, d), BF16), pltpu.VMEM((tmb, d), F32)]),
        out_shape=jax.ShapeDtypeStruct((s, d), F32),
        compiler_params=pltpu.CompilerParams(dimension_semantics=("parallel", "arbitrary"),
                                             vmem_limit_bytes=VMEM_LIMIT),
    )(blk_e, blk_n, xs, nw.reshape(1, d), wg, wu, wd)


def _combine_kernel(pos_ref, x_ref, route_ref, nw_ref, ys_hbm, op_ref, os_ref, ybuf, sem, *, tm, top_k, n_tok,
                    n_prompt_tiles):
    i = pl.program_id(0)

    def start(g, carry):
        r0 = pl.multiple_of(g * SUBLANES, SUBLANES)
        for b in range(SUBLANES):
            for k in range(top_k):
                _row_copy(ys_hbm, pos_ref[k * n_tok + i * tm + r0 + b], ybuf.at[k], r0 + b, sem).start()
        return carry

    def wait(g, carry):
        for _ in range(SUBLANES * top_k):
            _row_copy(ys_hbm, 0, ybuf.at[0], 0, sem).wait()
        return carry

    lax.fori_loop(0, tm // SUBLANES, start, 0)
    lax.fori_loop(0, tm // SUBLANES, wait, 0)
    route = route_ref[...]
    out = x_ref[...] + route[:, ROUTE_G1:ROUTE_G1 + 1] * ybuf[0] + route[:, ROUTE_G2:ROUTE_G2 + 1] * ybuf[1]
    y = _rms(out) * nw_ref[...]

    @pl.when(i < n_prompt_tiles)
    def _():
        op_ref[...] = y

    @pl.when(i >= n_prompt_tiles)
    def _():
        os_ref[...] = y


def _combine(pos, x, route, nw, ys, tm, top_k, tp, ts):
    d = x.shape[1]
    assert top_k == 2 and tp % tm == 0 and ts % tm == 0
    npt, nst = tp // tm, ts // tm
    return pl.pallas_call(
        functools.partial(_combine_kernel, tm=tm, top_k=top_k, n_tok=x.shape[0], n_prompt_tiles=npt),
        grid_spec=pltpu.PrefetchScalarGridSpec(
            num_scalar_prefetch=1, grid=(npt + nst,),
            in_specs=[pl.BlockSpec((tm, d), lambda i, p: (i, 0)),
                      pl.BlockSpec((tm, LANES), lambda i, p: (i, 0)),
                      pl.BlockSpec((1, d), lambda i, p: (0, 0)),
                      pl.BlockSpec(memory_space=pl.ANY)],
            out_specs=[pl.BlockSpec((tm, d), lambda i, p: (jnp.minimum(i, npt - 1), 0)),
                       pl.BlockSpec((tm, d), lambda i, p: (jnp.maximum(i - npt, 0), 0))],
            scratch_shapes=[pltpu.VMEM((top_k, tm, d), F32), pltpu.SemaphoreType.DMA(())]),
        out_shape=[jax.ShapeDtypeStruct((tp, d), F32), jax.ShapeDtypeStruct((ts, d), F32)],
        compiler_params=pltpu.CompilerParams(dimension_semantics=("arbitrary",),
                                             vmem_limit_bytes=VMEM_LIMIT),
    )(pos, x, route, nw.reshape(1, d), ys)


def _moe_plan(route_t, counts, n_experts, tmb, n_blocks, top_k, t_real):
    cnt = counts[:n_experts, 0].astype(jnp.int32)
    nblk = (cnt + tmb - 1) // tmb
    blk_end = jnp.cumsum(nblk)
    blk_start = blk_end - nblk
    slot0 = blk_start * tmb
    t = route_t.shape[1]
    tok = jnp.arange(t)
    experts = jnp.arange(n_experts)[:, None]
    pos = []
    for k in range(top_k):
        e_k = route_t[ROUTE_E1 + k].astype(jnp.int32)
        rank = route_t[ROUTE_R1 + k].astype(jnp.int32)
        sorted_slot = jnp.sum(jnp.where(e_k[None, :] == experts, slot0[:, None], 0), axis=0) + rank
        scratch = n_blocks * tmb + (tok - t_real) * top_k + k
        pos.append(jnp.where(tok >= t_real, scratch, sorted_slot))
    j = jnp.arange(n_blocks)
    used = j < blk_end[-1]
    blk_e = jnp.minimum(jnp.sum(j[:, None] >= blk_end[None, :], axis=1), n_experts - 1)
    last_e = jnp.max(jnp.where(nblk > 0, jnp.arange(n_experts), 0))
    blk_e = jnp.where(used, blk_e, last_e).astype(jnp.int32)
    blk_n = jnp.where(used, jnp.clip(cnt[blk_e] - (j - blk_start[blk_e]) * tmb, 0, tmb), 0).astype(jnp.int32)
    return jnp.concatenate(pos).astype(jnp.int32), blk_e, blk_n


class _Group:
    def __init__(self, row0, nb, nc, c, out_rows=None):
        self.row0, self.nb, self.nc, self.c = row0, nb, nc, c
        self.out_rows = c if out_rows is None else out_rows
        assert self.out_rows == c or (nb == 1 and nc == 1 and row0 % self.out_rows == 0)

    def rows(self, width, col_block):
        base, nc, c = self.row0 // self.c, self.nc, self.c
        return pl.BlockSpec((c, width), lambda b, i: (base + b * nc + i, col_block))

    def out_block(self, width):
        base, nc, r = self.row0 // self.out_rows, self.nc, self.out_rows
        return pl.BlockSpec((r, width), lambda b, i: (base + b * nc + i, 0))

    def chunk_rows(self, width):
        return pl.BlockSpec((self.c, width), lambda b, i: (i, 0))


def _full(shape):
    nd = len(shape)
    return pl.BlockSpec(tuple(shape), lambda b, i: (0,) * nd)


def _per_stream(shape):
    nd = len(shape)
    return pl.BlockSpec((None,) + tuple(shape[1:]), lambda b, i: (b,) + (0,) * (nd - 1))


def _scan_call(kernel, group, row_inputs, const_inputs, stream_inputs, y_prev, t_pad, width, extra_out_shapes,
               extra_out_specs, scratch_shapes):
    arrays, specs = [], []
    for arr, w, cb in row_inputs:
        arrays.append(arr)
        specs.append(group.rows(w, cb))
    for item in const_inputs:
        if isinstance(item, tuple):
            arr, spec = item
        else:
            arr, spec = item, _full(item.shape)
        arrays.append(arr)
        specs.append(spec)
    for arr in stream_inputs:
        arrays.append(arr)
        specs.append(_per_stream(arr.shape))
    aliases = {}
    if y_prev is not None:
        aliases = {len(arrays): 0}
        arrays.append(y_prev)
        specs.append(pl.BlockSpec(memory_space=pl.ANY))
    out_shape = [jax.ShapeDtypeStruct((t_pad, width), BF16)] + list(extra_out_shapes)
    out_specs = [group.out_block(width)] + list(extra_out_specs)
    return pl.pallas_call(
        functools.partial(kernel, c=group.c, nc=group.nc, has_prev=y_prev is not None),
        grid=(group.nb, group.nc),
        in_specs=specs,
        out_specs=out_specs,
        out_shape=out_shape,
        scratch_shapes=scratch_shapes,
        input_output_aliases=aliases,
        compiler_params=pltpu.CompilerParams(dimension_semantics=("parallel", "arbitrary"),
                                             vmem_limit_bytes=VMEM_LIMIT),
    )(*arrays)


def _store_rows(y_ref, y):
    c = y.shape[0]
    y_ref[0:c, :] = y
    if y_ref.shape[0] > c:
        y_ref[c:, :] = jnp.zeros((y_ref.shape[0] - c, y_ref.shape[1]), y_ref.dtype)


def _ssd_kernel(*refs, c, nc, has_prev):
    (z_ref, x_ref, bc_ref, dt_ref, convw_ref, convb_ref, dtb_ref, alog_ref, dskip_ref, nw_ref, expand_ref,
     hist0_ref, s0_ref) = refs[:13]
    y_ref, sout_ref, hout_ref, hist_scr, s_ref = refs[13 + int(has_prev):]
    i = pl.program_id(1)
    d_inner = x_ref.shape[1]
    n_state = SSD_STATE
    n_pairs = d_inner // LANES
    pairs_per_group = n_pairs // SSD_GROUPS
    n_hist = SSD_CONV - 1

    @pl.when(i == 0)
    def _():
        for p in range(n_pairs):
            s_ref[p] = jnp.concatenate([s0_ref[2 * p], s0_ref[2 * p + 1]], axis=1)
        hist_scr[HIST_ROWS - n_hist:HIST_ROWS, :] = hist0_ref[...]

    def chunk():
        hist_scr[HIST_ROWS:HIST_ROWS + c, 0:d_inner] = x_ref[...].astype(F32)
        hist_scr[HIST_ROWS:HIST_ROWS + c, d_inner:] = bc_ref[...].astype(F32)
        conv = convb_ref[...]
        for j in range(SSD_CONV):
            conv = conv + hist_scr[HIST_ROWS - n_hist + j:HIST_ROWS - n_hist + j + c, :] * convw_ref[j:j + 1, :]
        tail = hist_scr[HIST_ROWS + c - n_hist:HIST_ROWS + c, :]
        hist_scr[HIST_ROWS - n_hist:HIST_ROWS, :] = tail
        hout_ref[...] = tail
        xbc = _silu(conv)
        xs = xbc[:, 0:d_inner]
        bm = xbc[:, d_inner:d_inner + SSD_GROUPS * n_state]
        cm = xbc[:, d_inner + SSD_GROUPS * n_state:]

        x_dt = dt_ref[...] + dtb_ref[...]
        dt = jnp.maximum(x_dt, 0.0) + jnp.log1p(jnp.exp(-jnp.abs(x_dt)))
        log_a = dt * (-jnp.exp(alog_ref[...]))
        cum = _cumsum_time(log_a)
        expand = expand_ref[...]
        dt_e = _exact_ldot(dt, expand)
        cum_e = _exact_ldot(cum, expand)
        last_e = cum_e[c - 1:c, :]
        xdt = xs * dt_e
        wx = jnp.exp(last_e - cum_e) * xdt
        ecum = jnp.exp(cum_e)
        sdecay = jnp.exp(last_e)

        row = lax.broadcasted_iota(jnp.int32, (c, 2 * c), 0)
        col = lax.broadcasted_iota(jnp.int32, (c, 2 * c), 1)
        second = col >= c
        tcol = jnp.where(second, col - c, col)
        causal2 = tcol <= row
        diag2 = tcol == row
        r2 = lax.broadcasted_iota(jnp.int32, (2 * c, LANES), 0)
        l2 = lax.broadcasted_iota(jnp.int32, (2 * c, LANES), 1)
        half2 = (r2 >= c) == (l2 >= SSD_HEADDIM)

        ys = []
        for g in range(SSD_GROUPS):
            cg = cm[:, g * n_state:(g + 1) * n_state].astype(BF16)
            bg = bm[:, g * n_state:(g + 1) * n_state].astype(BF16)
            scores2 = _dot_nt(cg, jnp.concatenate([bg, bg], axis=0))
            for p in range(g * pairs_per_group, (g + 1) * pairs_per_group):
                sl = slice(p * LANES, (p + 1) * LANES)
                cum_p = cum_e[:, sl]
                cum_col = jnp.where(second, cum_p[:, SSD_HEADDIM:SSD_HEADDIM + 1], cum_p[:, 0:1])
                cum_row = jnp.sum(jnp.where(diag2, cum_col, 0.0), axis=0, keepdims=True)
                seg = cum_col - cum_row
                m2 = (scores2 * jnp.exp(jnp.where(causal2, seg, NEG_BIG))).astype(BF16)
                xp = xdt[:, sl]
                x2 = jnp.where(half2, jnp.concatenate([xp, xp], axis=0), 0.0).astype(BF16)
                y = _dot(m2, x2)
                y = y + _dot(cg, s_ref[p].astype(BF16)) * ecum[:, sl]
                s_ref[p] = s_ref[p] * sdecay[:, sl] + _dot_tn(bg, wx[:, sl].astype(BF16))
                ys.append(y)
        y = jnp.concatenate(ys, axis=1)
        y = y + xs * dskip_ref[...]
        y = y * _silu(z_ref[...].astype(F32))
        gw = d_inner // SSD_GROUPS
        y = jnp.concatenate([_rms(y[:, g * gw:(g + 1) * gw]) for g in range(SSD_GROUPS)], axis=1)
        _store_rows(y_ref, (y * nw_ref[...]).astype(BF16))

        @pl.when(i == nc - 1)
        def _():
            for p in range(n_pairs):
                sout_ref[2 * p] = s_ref[p][:, 0:SSD_HEADDIM]
                sout_ref[2 * p + 1] = s_ref[p][:, SSD_HEADDIM:]

    chunk()


def _ret_kernel(*refs, c, nc, has_prev):
    (q_ref, k_ref, v_ref, g_ref, sin_ref, cos_ref, dmat_ref, ecum_ref, wend_ref, sdec_ref, s0_ref) = refs[:11]
    y_ref, s_ref = refs[11 + int(has_prev):]
    i = pl.program_id(1)
    n_heads = q_ref.shape[1] // LANES

    @pl.when(i == 0)
    def _():
        s_ref[...] = s0_ref[...]

    def chunk():
        sin = sin_ref[...]
        cos = cos_ref[...]
        even = (lax.broadcasted_iota(jnp.int32, (c, LANES), 1) % 2) == 0

        def rotate(x):
            nxt = pltpu.roll(x, LANES - 1, 1)
            prv = pltpu.roll(x, 1, 1)
            return x * cos + jnp.where(even, -nxt, prv) * sin

        scale = LANES ** -0.5
        ys = []
        for h in range(n_heads):
            sl = slice(h * LANES, (h + 1) * LANES)
            qh = rotate(q_ref[:, sl].astype(F32))
            kh = rotate(k_ref[:, sl].astype(F32)) * scale
            vh = v_ref[:, sl]
            scores = _dot_nt(qh.astype(BF16), kh.astype(BF16)) * dmat_ref[h]
            y = _dot(scores.astype(BF16), vh) + _dot((qh * ecum_ref[:, sl]).astype(BF16), s_ref[h].astype(BF16))
            s_ref[h] = s_ref[h] * sdec_ref[:, sl] + _dot_tn((kh * wend_ref[:, sl]).astype(BF16), vh)
            ys.append(_rms(y) * _silu(g_ref[:, sl].astype(F32)))
        _store_rows(y_ref, jnp.concatenate(ys, axis=1).astype(BF16))

    chunk()


def _hgrn_kernel(*refs, c, nc, has_prev):
    (q_ref, f_ref, v_ref, g_ref, lb_ref, nw_ref, s0_ref) = refs[:7]
    y_ref, sout_ref, s_ref = refs[7 + int(has_prev):]
    i = pl.program_id(1)
    n_heads = q_ref.shape[1] // LANES

    @pl.when(i == 0)
    def _():
        for h in range(n_heads):
            s_ref[h] = s0_ref[h].T

    def chunk():
        lb = lb_ref[...]
        forget = lb + (1.0 - lb) * _sigmoid(f_ref[...])
        kk = 1.0 - forget
        gc = _cumsum_time(jnp.log(forget))
        last = gc[c - 1:c, :]
        qg = (_silu(q_ref[...].astype(F32)) * jnp.exp(gc)).astype(BF16)
        kg = (kk * jnp.exp(-gc)).astype(BF16)
        kw = (kk * jnp.exp(last - gc)).astype(BF16)
        sdec = jnp.exp(last)
        causal = _causal(c)
        ys = []
        for h in range(n_heads):
            sl = slice(h * LANES, (h + 1) * LANES)
            vh = v_ref[:, sl]
            scores = jnp.where(causal, _dot_nt(qg[:, sl], kg[:, sl]), 0.0)
            y = _dot(scores.astype(BF16), vh) + _dot_nt(qg[:, sl], s_ref[h].astype(BF16))
            s_ref[h] = s_ref[h] * sdec[:, sl] + _dot_tn(vh, kw[:, sl])
            ys.append(_rms(y) * nw_ref[...] * _silu(g_ref[:, sl].astype(F32)))
        _store_rows(y_ref, jnp.concatenate(ys, axis=1).astype(BF16))

        @pl.when(i == nc - 1)
        def _():
            for h in range(n_heads):
                sout_ref[h] = s_ref[h].T

    chunk()


def _rotation_tables(pos, dk):
    inv = 1.0 / (ROPE_BASE ** jnp.linspace(0.0, 1.0, dk // 2, dtype=F32))
    ang = pos.astype(F32)[:, None] * jnp.repeat(inv, 2)[None, :]
    return jnp.sin(ang), jnp.cos(ang)


def _retention_decay(c, n_heads, dv):
    log_gamma = jnp.log1p(-(2.0 ** (-5.0 - jnp.arange(n_heads, dtype=F32))))
    cum = jnp.cumsum(jnp.broadcast_to(log_gamma, (c, n_heads)), axis=0)
    cum_h = cum.T
    causal = jnp.tril(jnp.ones((c, c), dtype=bool))
    dmat = jnp.exp(jnp.where(causal, cum_h[:, :, None] - cum_h[:, None, :], -jnp.inf))
    ecum = jnp.repeat(jnp.exp(cum), dv, axis=1)
    wend = jnp.repeat(jnp.exp(cum[-1][None, :] - cum), dv, axis=1)
    sdec = jnp.repeat(jnp.exp(cum[-1])[None, :], dv, axis=1)
    return dmat, ecum, wend, sdec


EXPERT_FF_TILE_MAX = 1792
FFN_FF_TILE_MAX = 1408


def _largest_tile(n, cap):
    best = None
    for k in range(1, n // LANES + 1):
        if n % k == 0 and (n // k) % LANES == 0 and n // k <= cap:
            best = n // k
            break
    return n if best is None else best


def kernel(x_prompt, x_sample, state_ssd, state_ssd_conv, state_ret, state_hgrn, meta_tokens, norm_mix, norm_ffn,
           norm_final, w_in_ab, conv_w, conv_b, dt_bias, a_log, d_skip, ssd_norm, w_out_ab, w_ffn_gate, w_ffn_up,
           w_ffn_down, w_in_c, hgrn_lb, hgrn_norm, w_out_c, w_router, w_exp_gate, w_exp_up, w_exp_down):
    bp, seq, d = x_prompt.shape
    bs, dec_seq, _ = x_sample.shape
    n_meta = meta_tokens.shape[0]
    depth = norm_mix.shape[0]
    assert depth == 2 and seq % CHUNK == 0 and n_meta == N_META and d % LANES == 0
    n_ssd_heads = d_skip.shape[1]
    d_inner = n_ssd_heads * SSD_HEADDIM
    bc_w = 2 * SSD_GROUPS * SSD_STATE
    conv_dim = d_inner + bc_w
    ret_w = RET_HEADS * LANES
    assert d_inner == d and ret_w == d and conv_dim == conv_w.shape[2]

    tp, ts = bp * seq, bs * dec_seq
    t_real = tp + ts + n_meta
    tm = 512 if t_real >= 4096 else 64
    t_pad = -(-t_real // tm) * tm
    tm_big = 2 * tm if t_pad % (2 * tm) == 0 else tm
    g_prompt = _Group(0, bp, seq // CHUNK, CHUNK)
    g_sample = _Group(tp, bs, 1, dec_seq)
    g_meta = _Group(tp + ts, 1, 1, n_meta, out_rows=t_pad - tp - ts)
    assert tp % dec_seq == 0 and (tp + ts) % n_meta == 0

    w_in = w_in_ab[0]
    o_z, o_xbc, o_dt, o_q = 0, d_inner, d_inner + conv_dim, d_inner + conv_dim + n_ssd_heads
    n_cols = 6 * d + bc_w
    tn = n_cols // 2 if n_cols % (2 * LANES) == 0 else n_cols
    w_perm, w_dt = _regroup_columns(
        w_in, ((o_z, o_z + d_inner), (o_q, o_q + 4 * ret_w), (o_xbc, o_xbc + conv_dim)),
        (o_dt, o_dt + n_ssd_heads), LANES)
    proj, proj_dt, x0 = _normmm((x_prompt.reshape(tp, d), x_sample.reshape(ts, d), meta_tokens), norm_mix[0],
                                w_perm, w_dt, tm_big, tn, t_pad)
    cb_x, cb_bc = 5, (6 * d) // bc_w
    assert (6 * d) % bc_w == 0

    expand = jnp.asarray(np.arange(LANES)[:, None] == (np.arange(d_inner) // SSD_HEADDIM)[None, :], BF16)
    pad_h = LANES - n_ssd_heads
    ssd_consts = [conv_w[0], conv_b[0].reshape(1, conv_dim), jnp.pad(dt_bias[0], (0, pad_h)).reshape(1, LANES),
                  jnp.pad(a_log[0], (0, pad_h)).reshape(1, LANES),
                  jnp.repeat(d_skip[0], SSD_HEADDIM).reshape(1, d_inner), ssd_norm[0].reshape(1, d_inner), expand]
    def ssd(group, hist0, s0, y_prev):
        nb = group.nb
        return _scan_call(
            _ssd_kernel, group,
            [(proj, d_inner, 0), (proj, d_inner, cb_x), (proj, bc_w, cb_bc), (proj_dt, LANES, 0)],
            ssd_consts, [hist0, s0], y_prev, t_pad, d_inner,
            [jax.ShapeDtypeStruct((nb, n_ssd_heads, SSD_STATE, SSD_HEADDIM), F32),
             jax.ShapeDtypeStruct((nb, SSD_CONV - 1, conv_dim), F32)],
            [_per_stream((nb, n_ssd_heads, SSD_STATE, SSD_HEADDIM)), _per_stream((nb, SSD_CONV - 1, conv_dim))],
            [pltpu.VMEM((HIST_ROWS + group.c, conv_dim), F32),
             pltpu.VMEM((n_ssd_heads // 2, SSD_STATE, LANES), F32)])

    def bcast(s, n):
        return jnp.broadcast_to(s, (n,) + s.shape[1:])

    zeros = jnp.zeros
    y_ssd, s_m, h_m = ssd(g_meta, zeros((1, SSD_CONV - 1, conv_dim), F32),
                          zeros((1, n_ssd_heads, SSD_STATE, SSD_HEADDIM), F32), None)
    y_ssd, ssd_p, conv_p = ssd(g_prompt, bcast(h_m, bp), bcast(s_m, bp), y_ssd)
    y_ssd, ssd_s, conv_s = ssd(g_sample, state_ssd_conv[0], state_ssd[0], y_ssd)

    def ret(group, pos, s0, y_prev):
        nb = group.nb
        sin, cos = _rotation_tables(pos, LANES)
        dmat, ecum, wend, sdec = _retention_decay(group.c, RET_HEADS, LANES)
        return _scan_call(
            _ret_kernel, group,
            [(proj, ret_w, 1), (proj, ret_w, 2), (proj, ret_w, 3), (proj, ret_w, 4)],
            [(sin, group.chunk_rows(LANES)), (cos, group.chunk_rows(LANES)), dmat, ecum, wend, sdec],
            [s0], y_prev, t_pad, ret_w,
            [jax.ShapeDtypeStruct((nb, RET_HEADS, LANES, LANES), F32)],
            [_per_stream((nb, RET_HEADS, LANES, LANES))], [])

    y_ret, r_m = ret(g_meta, jnp.arange(n_meta), zeros((1, RET_HEADS, LANES, LANES), F32), None)
    y_ret, ret_p = ret(g_prompt, n_meta + jnp.arange(seq), bcast(r_m, bp), y_ret)
    y_ret, ret_s = ret(g_sample, n_meta + PAST_LEN + jnp.arange(dec_seq), state_ret[0], y_ret)

    w_out = w_out_ab[0].astype(BF16)
    ff = w_ffn_gate.shape[2]
    tf = _largest_tile(ff, FFN_FF_TILE_MAX)
    x2 = _ffn(x0, [y_ssd, y_ret], [w_out[:d_inner], w_out[d_inner:]], norm_ffn[0], w_ffn_gate[0].astype(BF16),
              w_ffn_up[0].astype(BF16), w_ffn_down[0].astype(BF16), tm, tf)

    w_c = w_in_c[0].astype(BF16)
    w_qig = jnp.concatenate([w_c[:, :d], w_c[:, 2 * d:]], axis=1)
    proj_c, proj_f = _normmm(x2, norm_mix[1], w_qig, w_c[:, d:2 * d], tm_big, 3 * d)
    lb_soft = jax.nn.softmax(hgrn_lb.astype(F32), axis=0)
    lb = (jnp.cumsum(lb_soft, axis=0) - lb_soft[0])[1].reshape(1, d)
    n_hg = d // HG_DK
    hg_consts = [lb, hgrn_norm[0].reshape(1, LANES)]

    def hgrn(group, s0, y_prev):
        nb = group.nb
        return _scan_call(
            _hgrn_kernel, group,
            [(proj_c, d, 0), (proj_f, d, 0), (proj_c, d, 1), (proj_c, d, 2)],
            hg_consts, [s0], y_prev, t_pad, d,
            [jax.ShapeDtypeStruct((nb, n_hg, HG_DK, LANES), F32)],
            [_per_stream((nb, n_hg, HG_DK, LANES))], [pltpu.VMEM((n_hg, LANES, HG_DK), F32)])

    y_hg, g_m = hgrn(g_meta, zeros((1, n_hg, HG_DK, LANES), F32), None)
    y_hg, hg_p = hgrn(g_prompt, bcast(g_m, bp), y_hg)
    y_hg, hg_s = hgrn(g_sample, state_hgrn[0], y_hg)

    n_exp = w_router.shape[2]
    x3, route, route_t, counts = _router(x2, y_hg, w_out_c[0].astype(BF16), norm_ffn[1], w_router[0], tm_big, tm, t_real)
    tmb = tm
    n_blocks = (TOP_K * t_real + n_exp * (tmb - 1)) // tmb
    pos, blk_e, blk_n = _moe_plan(route_t, counts, n_exp, tmb, n_blocks, TOP_K, t_real)
    xs = _dispatch(pos, x3, n_blocks * tmb + TOP_K * (t_pad - t_real), tm, TOP_K)
    ffe = w_exp_gate.shape[3]
    tfe = _largest_tile(ffe, EXPERT_FF_TILE_MAX)
    ys = _experts(blk_e, blk_n, xs, norm_ffn[1], w_exp_gate[0].astype(BF16), w_exp_up[0].astype(BF16),
                  w_exp_down[0].astype(BF16), tmb, tfe)
    y_prompt, y_sample = _combine(pos, x3, route, norm_final, ys, tm, TOP_K, tp, ts)
    y_prompt = y_prompt.reshape(bp, seq, d)
    y_sample = y_sample.reshape(bs, dec_seq, d)
    return (y_prompt, y_sample, ssd_p[None], conv_p[None], ret_p[None], hg_p[None],
            ssd_s[None], conv_s[None], ret_s[None], hg_s[None])
```

```python
import functools

import jax
import jax.numpy as jnp
import numpy as np
from jax import lax
from jax.experimental import pallas as pl
from jax.experimental.pallas import tpu as pltpu

F32 = jnp.float32
BF16 = jnp.bfloat16

CHUNK = 64
N_META = 16
PAST_LEN = 2048
EPS = 1e-6
SSD_HEADDIM = 64
SSD_GROUPS = 2
SSD_STATE = 128
SSD_CONV = 4
RET_HEADS = 8
ROPE_BASE = 10000.0
HG_DK = 128
TOP_K = 2

LANES = 128
SUBLANES = 8
HIST_ROWS = 8
VMEM_LIMIT = 56 * 1024 * 1024
NEG_BIG = -1e30


def _dot(a, b):
    return jnp.dot(a, b, preferred_element_type=F32)


def _dot_nt(a, b):
    return lax.dot_general(a, b, (((1,), (1,)), ((), ())), preferred_element_type=F32)


def _dot_tn(a, b):
    return lax.dot_general(a, b, (((0,), (0,)), ((), ())), preferred_element_type=F32)


def _split3(x):
    hi = x.astype(BF16)
    r = x - hi.astype(F32)
    mid = r.astype(BF16)
    lo = (r - mid.astype(F32)).astype(BF16)
    return hi, mid, lo


def _exact_ldot(x, m01):
    hi, mid, lo = _split3(x)
    return _dot(hi, m01) + _dot(mid, m01) + _dot(lo, m01)


def _exact_rdot(m01, x):
    hi, mid, lo = _split3(x)
    return _dot(m01, hi) + _dot(m01, mid) + _dot(m01, lo)


def _causal(c):
    row = lax.broadcasted_iota(jnp.int32, (c, c), 0)
    col = lax.broadcasted_iota(jnp.int32, (c, c), 1)
    return row >= col


def _cumsum_time(x):
    c = x.shape[0]
    return _exact_rdot(jnp.where(_causal(c), 1.0, 0.0).astype(BF16), x)


def _sigmoid(x):
    return 1.0 / (1.0 + jnp.exp(-x))


def _silu(x):
    return x * _sigmoid(x)


def _rms(x):
    return x * lax.rsqrt(jnp.mean(x * x, axis=-1, keepdims=True) + EPS)


def _normmm_kernel(*refs, n_src, n_lead_tiles):
    xs, (nw_ref, w_ref, ws_ref, o_ref, os_ref) = refs[:n_src], refs[n_src:n_src + 5]
    xn_scr = refs[-1]

    @pl.when(pl.program_id(1) == 0)
    def _():
        if n_src == 1:
            x_ref = xs[0]
        else:
            lead_ref, tail_ref, meta_ref = xs
            x_ref = refs[n_src + 5]
            i = pl.program_id(0)
            tm = x_ref.shape[0]
            n_tail, n_meta = tail_ref.shape[0], meta_ref.shape[0]

            @pl.when(i < n_lead_tiles)
            def _():
                x_ref[...] = lead_ref[...]

            for t in range(pl.cdiv(n_tail + n_meta, tm)):
                @pl.when(i == n_lead_tiles + t)
                def _(lo=t * tm):
                    a, b = lo, min(lo + tm, n_tail)
                    if a < b:
                        x_ref[a - lo:b - lo, :] = tail_ref[a:b, :]
                    a, b = max(lo, n_tail), min(lo + tm, n_tail + n_meta)
                    if a < b:
                        x_ref[a - lo:b - lo, :] = meta_ref[a - n_tail:b - n_tail, :]
                    if b < lo + tm:
                        x_ref[b - lo:, :] = jnp.zeros((lo + tm - b, x_ref.shape[1]), x_ref.dtype)

        xn = (_rms(x_ref[...]) * nw_ref[...]).astype(BF16)
        xn_scr[...] = xn
        os_ref[...] = _dot(xn, ws_ref[...])

    o_ref[...] = _dot(xn_scr[...], w_ref[...]).astype(BF16)


def _normmm(x, nw, w, w_side, tm, tn, t_pad=None):
    sources = x if isinstance(x, tuple) else (x,)
    k = sources[0].shape[1]
    n, ns = w.shape[1], w_side.shape[1]
    out_specs = [pl.BlockSpec((tm, tn), lambda i, j: (i, j)), pl.BlockSpec((tm, ns), lambda i, j: (i, 0))]
    if len(sources) == 1:
        t, n_lead = x.shape[0], 0
        x_specs = [pl.BlockSpec((tm, k), lambda i, j: (i, 0))]
        out_shape = [jax.ShapeDtypeStruct((t, n), BF16), jax.ShapeDtypeStruct((t, ns), F32)]
    else:
        lead, tail, meta = sources
        t, n_lead = t_pad, lead.shape[0] // tm
        assert lead.shape[0] % tm == 0 and (n_lead + pl.cdiv(tail.shape[0] + meta.shape[0], tm)) * tm == t_pad
        x_specs = [pl.BlockSpec((tm, k), lambda i, j: (jnp.minimum(i, n_lead - 1), 0)),
                   pl.BlockSpec(tail.shape, lambda i, j: (0, 0)),
                   pl.BlockSpec(meta.shape, lambda i, j: (0, 0))]
        out_specs.append(pl.BlockSpec((tm, k), lambda i, j: (i, 0)))
        out_shape = [jax.ShapeDtypeStruct((t, n), BF16), jax.ShapeDtypeStruct((t, ns), F32),
                     jax.ShapeDtypeStruct((t, k), F32)]
    return pl.pallas_call(
        functools.partial(_normmm_kernel, n_src=len(sources), n_lead_tiles=n_lead),
        grid=(t // tm, n // tn),
        in_specs=x_specs + [pl.BlockSpec((1, k), lambda i, j: (0, 0)),
                            pl.BlockSpec((k, tn), lambda i, j: (0, j)),
                            pl.BlockSpec((k, ns), lambda i, j: (0, 0))],
        out_specs=out_specs,
        out_shape=out_shape,
        scratch_shapes=[pltpu.VMEM((tm, k), BF16)],
        compiler_params=pltpu.CompilerParams(dimension_semantics=("parallel", "arbitrary"),
                                             vmem_limit_bytes=VMEM_LIMIT),
    )(*sources, nw.reshape(1, k), w, w_side)


def _regroup_kernel(w_ref, o_ref, os_ref, *, pieces, side_piece):
    col = 0
    for a, b in pieces:
        o_ref[:, col:col + b - a] = w_ref[:, a:b].astype(BF16)
        col += b - a
    a, b = side_piece
    os_ref[...] = jnp.zeros_like(os_ref)
    os_ref[:, 0:b - a] = w_ref[:, a:b].astype(BF16)


def _regroup_columns(w, pieces, side_piece, rows):
    k, n = w.shape
    n_out = sum(b - a for a, b in pieces)
    return pl.pallas_call(
        functools.partial(_regroup_kernel, pieces=pieces, side_piece=side_piece),
        grid=(k // rows,),
        in_specs=[pl.BlockSpec((rows, n), lambda i: (i, 0))],
        out_specs=[pl.BlockSpec((rows, n_out), lambda i: (i, 0)), pl.BlockSpec((rows, LANES), lambda i: (i, 0))],
        out_shape=[jax.ShapeDtypeStruct((k, n_out), BF16), jax.ShapeDtypeStruct((k, LANES), BF16)],
        compiler_params=pltpu.CompilerParams(dimension_semantics=("parallel",), vmem_limit_bytes=VMEM_LIMIT),
    )(w)


def _ffn_kernel(*refs, n_in):
    x_ref = refs[0]
    ys = refs[1:1 + n_in]
    ws = refs[1 + n_in:1 + 2 * n_in]
    nw_ref, wg_ref, wu_ref, wd_ref, o_ref, xn_scr, acc_scr = refs[1 + 2 * n_in:]
    f = pl.program_id(1)

    @pl.when(f == 0)
    def _():
        x = x_ref[...]
        for y, w in zip(ys, ws):
            x = x + _dot(y[...], w[...])
        xn_scr[...] = (_rms(x) * nw_ref[...]).astype(BF16)
        acc_scr[...] = x

    xn = xn_scr[...]
    h = _silu(_dot(xn, wg_ref[...])) * _dot(xn, wu_ref[...])
    acc_scr[...] += _dot(h.astype(BF16), wd_ref[...])

    @pl.when(f == pl.num_programs(1) - 1)
    def _():
        o_ref[...] = acc_scr[...]


def _ffn(x, ys, ws, nw, wg, wu, wd, tm, tf):
    t, d = x.shape
    ff = wg.shape[1]
    n_in = len(ys)
    in_specs = [pl.BlockSpec((tm, d), lambda i, f: (i, 0))]
    in_specs += [pl.BlockSpec((tm, y.shape[1]), lambda i, f: (i, 0)) for y in ys]
    in_specs += [pl.BlockSpec(w.shape, lambda i, f: (0, 0)) for w in ws]
    in_specs += [pl.BlockSpec((1, d), lambda i, f: (0, 0)),
                 pl.BlockSpec((d, tf), lambda i, f: (0, f)),
                 pl.BlockSpec((d, tf), lambda i, f: (0, f)),
                 pl.BlockSpec((tf, d), lambda i, f: (f, 0))]
    return pl.pallas_call(
        functools.partial(_ffn_kernel, n_in=n_in),
        grid=(t // tm, ff // tf),
        in_specs=in_specs,
        out_specs=pl.BlockSpec((tm, d), lambda i, f: (i, 0)),
        out_shape=jax.ShapeDtypeStruct((t, d), F32),
        scratch_shapes=[pltpu.VMEM((tm, d), BF16), pltpu.VMEM((tm, d), F32)],
        compiler_params=pltpu.CompilerParams(dimension_semantics=("parallel", "arbitrary"),
                                             vmem_limit_bytes=VMEM_LIMIT),
    )(x, *ys, *ws, nw.reshape(1, d), wg, wu, wd)


ROUTE_E1, ROUTE_E2, ROUTE_R1, ROUTE_R2, ROUTE_G1, ROUTE_G2 = range(6)


def _router_kernel(x_ref, y_ref, wo_ref, nw_ref, whi_ref, wlo_ref, earlier_ref, xo_ref, route_ref, routet_ref,
                   count_ref, *, n_experts, t_real):
    i = pl.program_id(0)
    tm = x_ref.shape[0]
    sub = earlier_ref.shape[0]
    e_rows = count_ref.shape[0]

    @pl.when(i == 0)
    def _():
        count_ref[...] = jnp.zeros_like(count_ref)

    count = count_ref[:, 0:1]
    row = lax.broadcasted_iota(jnp.int32, (e_rows, sub), 0)
    row_f = row.astype(F32)
    field = lax.broadcasted_iota(jnp.int32, (LANES, sub), 0)
    for r0 in range(0, tm, sub):
        rows = slice(r0, r0 + sub)
        xo = x_ref[rows, :] + _dot(y_ref[rows, :], wo_ref[...])
        xo_ref[rows, :] = xo
        xn = _rms(xo) * nw_ref[...]
        hi = xn.astype(BF16)
        lo = (xn - hi.astype(F32)).astype(BF16)
        logits = _dot(hi, whi_ref[...]) + _dot(lo, whi_ref[...]) + _dot(hi, wlo_ref[...])
        lt = jnp.where(row < n_experts, logits.T[0:e_rows, :], NEG_BIG)
        m1 = jnp.max(lt, axis=0, keepdims=True)
        i1 = jnp.min(jnp.where(lt == m1, row_f, float(e_rows)), axis=0, keepdims=True)
        rest = jnp.where(row_f == i1, NEG_BIG, lt)
        m2 = jnp.max(rest, axis=0, keepdims=True)
        i2 = jnp.min(jnp.where(rest == m2, row_f, float(e_rows)), axis=0, keepdims=True)
        e2 = jnp.exp(m2 - m1)
        g1 = 1.0 / (1.0 + e2)
        g2 = e2 / (1.0 + e2)

        tok = lax.broadcasted_iota(jnp.int32, (1, sub), 1) + (i * tm + r0)
        valid = tok < t_real
        sel = jnp.where(valid & ((row_f == i1) | (row_f == i2)), 1.0, 0.0)
        rank = count + _dot(sel.astype(BF16), earlier_ref[...])
        r1 = jnp.sum(jnp.where(row_f == i1, rank, 0.0), axis=0, keepdims=True)
        r2 = jnp.sum(jnp.where(row_f == i2, rank, 0.0), axis=0, keepdims=True)
        count = count + jnp.sum(sel, axis=1, keepdims=True)

        rec = jnp.zeros((LANES, sub), F32)
        for k, v in ((ROUTE_E1, i1), (ROUTE_E2, i2), (ROUTE_R1, r1), (ROUTE_R2, r2), (ROUTE_G1, g1),
                     (ROUTE_G2, g2)):
            rec = jnp.where(field == k, v, rec)
        rec = jnp.where(valid, rec, 0.0)
        route_ref[rows, :] = rec.T
        routet_ref[:, rows] = rec[0:SUBLANES, :]
    count_ref[...] = jnp.broadcast_to(count, count_ref.shape)


def _router(x, y, wo, nw, w_router, tm, sub, t_real):
    t, d = x.shape
    e = w_router.shape[1]
    wpad = jnp.zeros((d, LANES), F32).at[:, :e].set(w_router)
    whi = wpad.astype(BF16)
    wlo = (wpad - whi.astype(F32)).astype(BF16)
    e_rows = -(-e // SUBLANES) * SUBLANES
    earlier = jnp.asarray(np.arange(sub)[:, None] < np.arange(sub)[None, :], BF16)
    return pl.pallas_call(
        functools.partial(_router_kernel, n_experts=e, t_real=t_real),
        grid=(t // tm,),
        in_specs=[pl.BlockSpec((tm, d), lambda i: (i, 0)),
                  pl.BlockSpec((tm, y.shape[1]), lambda i: (i, 0)),
                  pl.BlockSpec(wo.shape, lambda i: (0, 0)),
                  pl.BlockSpec((1, d), lambda i: (0, 0)),
                  pl.BlockSpec((d, LANES), lambda i: (0, 0)),
                  pl.BlockSpec((d, LANES), lambda i: (0, 0)),
                  pl.BlockSpec((sub, sub), lambda i: (0, 0))],
        out_specs=[pl.BlockSpec((tm, d), lambda i: (i, 0)), pl.BlockSpec((tm, LANES), lambda i: (i, 0)),
                   pl.BlockSpec((SUBLANES, tm), lambda i: (0, i)), pl.BlockSpec((e_rows, LANES), lambda i: (0, 0))],
        out_shape=[jax.ShapeDtypeStruct((t, d), F32), jax.ShapeDtypeStruct((t, LANES), F32),
                   jax.ShapeDtypeStruct((SUBLANES, t), F32), jax.ShapeDtypeStruct((e_rows, LANES), F32)],
        compiler_params=pltpu.CompilerParams(dimension_semantics=("arbitrary",),
                                             vmem_limit_bytes=VMEM_LIMIT),
    )(x, y, wo, nw.reshape(1, d), whi, wlo, earlier)


ROW_GROUP = 2 * SUBLANES


def _row_copy(src, src_row, dst, dst_row, sem):
    return pltpu.make_async_copy(src.at[pl.ds(src_row, 1)], dst.at[pl.ds(dst_row, 1)], sem)


def _dispatch_kernel(pos_ref, x_ref, xs_hbm, sem, *, tm, top_k, n_tok):
    i = pl.program_id(0)

    def start(g, carry):
        r0 = pl.multiple_of(g * ROW_GROUP, ROW_GROUP)
        for b in range(ROW_GROUP):
            for k in range(top_k):
                _row_copy(x_ref, r0 + b, xs_hbm, pos_ref[k * n_tok + i * tm + r0 + b], sem).start()
        return carry

    def wait(g, carry):
        for _ in range(ROW_GROUP * top_k):
            _row_copy(x_ref, 0, xs_hbm, 0, sem).wait()
        return carry

    lax.fori_loop(0, tm // ROW_GROUP, start, 0)
    lax.fori_loop(0, tm // ROW_GROUP, wait, 0)


def _dispatch(pos, x, n_rows_out, tm, top_k):
    t, d = x.shape
    return pl.pallas_call(
        functools.partial(_dispatch_kernel, tm=tm, top_k=top_k, n_tok=t),
        grid_spec=pltpu.PrefetchScalarGridSpec(
            num_scalar_prefetch=1, grid=(t // tm,),
            in_specs=[pl.BlockSpec((tm, d), lambda i, p: (i, 0))],
            out_specs=pl.BlockSpec(memory_space=pl.ANY),
            scratch_shapes=[pltpu.SemaphoreType.DMA(())]),
        out_shape=jax.ShapeDtypeStruct((n_rows_out, d), F32),
        compiler_params=pltpu.CompilerParams(dimension_semantics=("arbitrary",),
                                             vmem_limit_bytes=VMEM_LIMIT),
    )(pos, x)


def _expert_kernel(blk_e_ref, blk_n_ref, xs_ref, nw_ref, wg_ref, wu_ref, wd_ref, o_ref, xn_scr, acc_scr):
    del blk_e_ref
    j = pl.program_id(0)
    f = pl.program_id(1)
    n_valid = blk_n_ref[j]
    last = f == pl.num_programs(1) - 1

    @pl.when((n_valid > 0) & (f == 0))
    def _():
        row = lax.broadcasted_iota(jnp.int32, (xs_ref.shape[0], 1), 0)
        x = jnp.where(row < n_valid, xs_ref[...], 0.0)
        xn_scr[...] = (_rms(x) * nw_ref[...]).astype(BF16)
        acc_scr[...] = jnp.zeros_like(acc_scr)

    @pl.when(n_valid > 0)
    def _():
        xn = xn_scr[...]
        h = _silu(_dot(xn, wg_ref[...])) * _dot(xn, wu_ref[...])
        acc_scr[...] += _dot(h.astype(BF16), wd_ref[...])

    @pl.when((n_valid > 0) & last)
    def _():
        o_ref[...] = acc_scr[...]

    @pl.when((n_valid == 0) & last)
    def _():
        o_ref[...] = jnp.zeros_like(o_ref)


def _experts(blk_e, blk_n, xs, nw, wg, wu, wd, tmb, tf):
    d = xs.shape[1]
    s = blk_e.shape[0] * tmb
    ff = wg.shape[2]
    nf = ff // tf

    def w_up(j, f, be, bn):
        return (be[j], 0, jnp.where(bn[j] > 0, f, nf - 1))

    def w_down(j, f, be, bn):
        return (be[j], jnp.where(bn[j] > 0, f, nf - 1), 0)

    return pl.pallas_call(
        _expert_kernel,
        grid_spec=pltpu.PrefetchScalarGridSpec(
            num_scalar_prefetch=2, grid=(s // tmb, nf),
            in_specs=[pl.BlockSpec((tmb, d), lambda j, f, be, bn: (j, 0)),
                      pl.BlockSpec((1, d), lambda j, f, be, bn: (0, 0)),
                      pl.BlockSpec((None, d, tf), w_up),
                      pl.BlockSpec((None, d, tf), w_up),
                      pl.BlockSpec((None, tf, d), w_down)],
            out_specs=pl.BlockSpec((tmb, d), lambda j, f, be, bn: (j, 0)),
            scratch_shapes=[pltpu.VMEM((tmb, d), BF16), pltpu.VMEM((tmb, d), F32)]),
        out_shape=jax.ShapeDtypeStruct((s, d), F32),
        compiler_params=pltpu.CompilerParams(dimension_semantics=("parallel", "arbitrary"),
                                             vmem_limit_bytes=VMEM_LIMIT),
    )(blk_e, blk_n, xs, nw.reshape(1, d), wg, wu, wd)


def _combine_kernel(pos_ref, x_ref, route_ref, nw_ref, ys_hbm, op_ref, os_ref, ybuf, sem, *, tm, top_k, n_tok,
                    n_prompt_tiles):
    i = pl.program_id(0)

    def start(g, carry):
        r0 = pl.multiple_of(g * ROW_GROUP, ROW_GROUP)
        for b in range(ROW_GROUP):
            for k in range(top_k):
                _row_copy(ys_hbm, pos_ref[k * n_tok + i * tm + r0 + b], ybuf.at[k], r0 + b, sem).start()
        return carry

    def wait(g, carry):
        for _ in range(ROW_GROUP * top_k):
            _row_copy(ys_hbm, 0, ybuf.at[0], 0, sem).wait()
        return carry

    lax.fori_loop(0, tm // ROW_GROUP, start, 0)
    lax.fori_loop(0, tm // ROW_GROUP, wait, 0)
    route = route_ref[...]
    out = x_ref[...] + route[:, ROUTE_G1:ROUTE_G1 + 1] * ybuf[0] + route[:, ROUTE_G2:ROUTE_G2 + 1] * ybuf[1]
    y = _rms(out) * nw_ref[...]

    @pl.when(i < n_prompt_tiles)
    def _():
        op_ref[...] = y

    @pl.when(i >= n_prompt_tiles)
    def _():
        os_ref[...] = y


def _combine(pos, x, route, nw, ys, tm, top_k, tp, ts):
    d = x.shape[1]
    assert top_k == 2 and tp % tm == 0 and ts % tm == 0
    npt, nst = tp // tm, ts // tm
    return pl.pallas_call(
        functools.partial(_combine_kernel, tm=tm, top_k=top_k, n_tok=x.shape[0], n_prompt_tiles=npt),
        grid_spec=pltpu.PrefetchScalarGridSpec(
            num_scalar_prefetch=1, grid=(npt + nst,),
            in_specs=[pl.BlockSpec((tm, d), lambda i, p: (i, 0)),
                      pl.BlockSpec((tm, LANES), lambda i, p: (i, 0)),
                      pl.BlockSpec((1, d), lambda i, p: (0, 0)),
                      pl.BlockSpec(memory_space=pl.ANY)],
            out_specs=[pl.BlockSpec((tm, d), lambda i, p: (jnp.minimum(i, npt - 1), 0)),
                       pl.BlockSpec((tm, d), lambda i, p: (jnp.maximum(i - npt, 0), 0))],
            scratch_shapes=[pltpu.VMEM((top_k, tm, d), F32), pltpu.SemaphoreType.DMA(())]),
        out_shape=[jax.ShapeDtypeStruct((tp, d), F32), jax.ShapeDtypeStruct((ts, d), F32)],
        compiler_params=pltpu.CompilerParams(dimension_semantics=("arbitrary",),
                                             vmem_limit_bytes=VMEM_LIMIT),
    )(pos, x, route, nw.reshape(1, d), ys)


def _moe_plan(route_t, counts, n_experts, tmb, n_blocks, top_k, t_real):
    cnt = counts[:n_experts, 0].astype(jnp.int32)
    nblk = (cnt + tmb - 1) // tmb
    blk_end = jnp.cumsum(nblk)
    blk_start = blk_end - nblk
    slot0 = blk_start * tmb
    t = route_t.shape[1]
    tok = jnp.arange(t)
    experts = jnp.arange(n_experts)[:, None]
    pos = []
    for k in range(top_k):
        e_k = route_t[ROUTE_E1 + k].astype(jnp.int32)
        rank = route_t[ROUTE_R1 + k].astype(jnp.int32)
        sorted_slot = jnp.sum(jnp.where(e_k[None, :] == experts, slot0[:, None], 0), axis=0) + rank
        scratch = n_blocks * tmb + (tok - t_real) * top_k + k
        pos.append(jnp.where(tok >= t_real, scratch, sorted_slot))
    j = jnp.arange(n_blocks)
    used = j < blk_end[-1]
    blk_e = jnp.minimum(jnp.sum(j[:, None] >= blk_end[None, :], axis=1), n_experts - 1)
    last_e = jnp.max(jnp.where(nblk > 0, jnp.arange(n_experts), 0))
    blk_e = jnp.where(used, blk_e, last_e).astype(jnp.int32)
    blk_n = jnp.where(used, jnp.clip(cnt[blk_e] - (j - blk_start[blk_e]) * tmb, 0, tmb), 0).astype(jnp.int32)
    return jnp.concatenate(pos).astype(jnp.int32), blk_e, blk_n


class _Group:
    def __init__(self, row0, nb, nc, c, out_rows=None):
        self.row0, self.nb, self.nc, self.c = row0, nb, nc, c
        self.out_rows = c if out_rows is None else out_rows
        assert self.out_rows == c or (nb == 1 and nc == 1 and row0 % self.out_rows == 0)

    def rows(self, width, col_block):
        base, nc, c = self.row0 // self.c, self.nc, self.c
        return pl.BlockSpec((c, width), lambda b, i: (base + b * nc + i, col_block))

    def out_block(self, width):
        base, nc, r = self.row0 // self.out_rows, self.nc, self.out_rows
        return pl.BlockSpec((r, width), lambda b, i: (base + b * nc + i, 0))

    def chunk_rows(self, width):
        return pl.BlockSpec((self.c, width), lambda b, i: (i, 0))


def _full(shape):
    nd = len(shape)
    return pl.BlockSpec(tuple(shape), lambda b, i: (0,) * nd)


def _per_stream(shape):
    nd = len(shape)
    return pl.BlockSpec((None,) + tuple(shape[1:]), lambda b, i: (b,) + (0,) * (nd - 1))


def _scan_call(kernel, group, row_inputs, const_inputs, stream_inputs, y_prev, t_pad, width, extra_out_shapes,
               extra_out_specs, scratch_shapes):
    arrays, specs = [], []
    for arr, w, cb in row_inputs:
        arrays.append(arr)
        specs.append(group.rows(w, cb))
    for item in const_inputs:
        if isinstance(item, tuple):
            arr, spec = item
        else:
            arr, spec = item, _full(item.shape)
        arrays.append(arr)
        specs.append(spec)
    for arr in stream_inputs:
        arrays.append(arr)
        specs.append(_per_stream(arr.shape))
    aliases = {}
    if y_prev is not None:
        aliases = {len(arrays): 0}
        arrays.append(y_prev)
        specs.append(pl.BlockSpec(memory_space=pl.ANY))
    out_shape = [jax.ShapeDtypeStruct((t_pad, width), BF16)] + list(extra_out_shapes)
    out_specs = [group.out_block(width)] + list(extra_out_specs)
    return pl.pallas_call(
        functools.partial(kernel, c=group.c, nc=group.nc, has_prev=y_prev is not None),
        grid=(group.nb, group.nc),
        in_specs=specs,
        out_specs=out_specs,
        out_shape=out_shape,
        scratch_shapes=scratch_shapes,
        input_output_aliases=aliases,
        compiler_params=pltpu.CompilerParams(dimension_semantics=("parallel", "arbitrary"),
                                             vmem_limit_bytes=VMEM_LIMIT),
    )(*arrays)


def _store_rows(y_ref, y):
    c = y.shape[0]
    y_ref[0:c, :] = y
    if y_ref.shape[0] > c:
        y_ref[c:, :] = jnp.zeros((y_ref.shape[0] - c, y_ref.shape[1]), y_ref.dtype)


def _ssd_kernel(*refs, c, nc, has_prev):
    (z_ref, x_ref, bc_ref, dt_ref, convw_ref, convb_ref, dtb_ref, alog_ref, dskip_ref, nw_ref, expand_ref,
     hist0_ref, s0_ref) = refs[:13]
    y_ref, sout_ref, hout_ref, hist_scr, s_ref = refs[13 + int(has_prev):]
    i = pl.program_id(1)
    d_inner = x_ref.shape[1]
    n_state = SSD_STATE
    n_pairs = d_inner // LANES
    pairs_per_group = n_pairs // SSD_GROUPS
    n_hist = SSD_CONV - 1

    @pl.when(i == 0)
    def _():
        for p in range(n_pairs):
            s_ref[p] = jnp.concatenate([s0_ref[2 * p], s0_ref[2 * p + 1]], axis=1)
        hist_scr[HIST_ROWS - n_hist:HIST_ROWS, :] = hist0_ref[...]

    def chunk():
        hist_scr[HIST_ROWS:HIST_ROWS + c, 0:d_inner] = x_ref[...].astype(F32)
        hist_scr[HIST_ROWS:HIST_ROWS + c, d_inner:] = bc_ref[...].astype(F32)
        conv = convb_ref[...]
        for j in range(SSD_CONV):
            conv = conv + hist_scr[HIST_ROWS - n_hist + j:HIST_ROWS - n_hist + j + c, :] * convw_ref[j:j + 1, :]
        tail = hist_scr[HIST_ROWS + c - n_hist:HIST_ROWS + c, :]
        hist_scr[HIST_ROWS - n_hist:HIST_ROWS, :] = tail
        hout_ref[...] = tail
        xbc = _silu(conv)
        xs = xbc[:, 0:d_inner]
        bm = xbc[:, d_inner:d_inner + SSD_GROUPS * n_state]
        cm = xbc[:, d_inner + SSD_GROUPS * n_state:]

        x_dt = dt_ref[...] + dtb_ref[...]
        dt = jnp.maximum(x_dt, 0.0) + jnp.log1p(jnp.exp(-jnp.abs(x_dt)))
        log_a = dt * (-jnp.exp(alog_ref[...]))
        cum = _cumsum_time(log_a)
        expand = expand_ref[...]
        dt_e = _exact_ldot(dt, expand)
        cum_e = _exact_ldot(cum, expand)
        last_e = cum_e[c - 1:c, :]
        xdt = xs * dt_e
        wx = jnp.exp(last_e - cum_e) * xdt
        ecum = jnp.exp(cum_e)
        sdecay = jnp.exp(last_e)

        row = lax.broadcasted_iota(jnp.int32, (c, 2 * c), 0)
        col = lax.broadcasted_iota(jnp.int32, (c, 2 * c), 1)
        second = col >= c
        tcol = jnp.where(second, col - c, col)
        causal2 = tcol <= row
        diag2 = tcol == row
        r2 = lax.broadcasted_iota(jnp.int32, (2 * c, LANES), 0)
        l2 = lax.broadcasted_iota(jnp.int32, (2 * c, LANES), 1)
        half2 = (r2 >= c) == (l2 >= SSD_HEADDIM)

        ys = []
        for g in range(SSD_GROUPS):
            cg = cm[:, g * n_state:(g + 1) * n_state].astype(BF16)
            bg = bm[:, g * n_state:(g + 1) * n_state].astype(BF16)
            scores2 = _dot_nt(cg, jnp.concatenate([bg, bg], axis=0))
            for p in range(g * pairs_per_group, (g + 1) * pairs_per_group):
                sl = slice(p * LANES, (p + 1) * LANES)
                cum_p = cum_e[:, sl]
                cum_col = jnp.where(second, cum_p[:, SSD_HEADDIM:SSD_HEADDIM + 1], cum_p[:, 0:1])
                cum_row = jnp.sum(jnp.where(diag2, cum_col, 0.0), axis=0, keepdims=True)
                seg = cum_col - cum_row
                m2 = (scores2 * jnp.exp(jnp.where(causal2, seg, NEG_BIG))).astype(BF16)
                xp = xdt[:, sl]
                x2 = jnp.where(half2, jnp.concatenate([xp, xp], axis=0), 0.0).astype(BF16)
                y = _dot(m2, x2)
                y = y + _dot(cg, s_ref[p].astype(BF16)) * ecum[:, sl]
                s_ref[p] = s_ref[p] * sdecay[:, sl] + _dot_tn(bg, wx[:, sl].astype(BF16))
                ys.append(y)
        y = jnp.concatenate(ys, axis=1)
        y = y + xs * dskip_ref[...]
        y = y * _silu(z_ref[...].astype(F32))
        gw = d_inner // SSD_GROUPS
        y = jnp.concatenate([_rms(y[:, g * gw:(g + 1) * gw]) for g in range(SSD_GROUPS)], axis=1)
        _store_rows(y_ref, (y * nw_ref[...]).astype(BF16))

        @pl.when(i == nc - 1)
        def _():
            for p in range(n_pairs):
                sout_ref[2 * p] = s_ref[p][:, 0:SSD_HEADDIM]
                sout_ref[2 * p + 1] = s_ref[p][:, SSD_HEADDIM:]

    chunk()


def _ret_kernel(*refs, c, nc, has_prev):
    (q_ref, k_ref, v_ref, g_ref, sin_ref, cos_ref, dmat_ref, ecum_ref, wend_ref, sdec_ref, s0_ref) = refs[:11]
    y_ref, s_ref = refs[11 + int(has_prev):]
    i = pl.program_id(1)
    n_heads = q_ref.shape[1] // LANES

    @pl.when(i == 0)
    def _():
        s_ref[...] = s0_ref[...]

    def chunk():
        sin = sin_ref[...]
        cos = cos_ref[...]
        even = (lax.broadcasted_iota(jnp.int32, (c, LANES), 1) % 2) == 0

        def rotate(x):
            nxt = pltpu.roll(x, LANES - 1, 1)
            prv = pltpu.roll(x, 1, 1)
            return x * cos + jnp.where(even, -nxt, prv) * sin

        scale = LANES ** -0.5
        ys = []
        for h in range(n_heads):
            sl = slice(h * LANES, (h + 1) * LANES)
            qh = rotate(q_ref[:, sl].astype(F32))
            kh = rotate(k_ref[:, sl].astype(F32)) * scale
            vh = v_ref[:, sl]
            scores = _dot_nt(qh.astype(BF16), kh.astype(BF16)) * dmat_ref[h]
            y = _dot(scores.astype(BF16), vh) + _dot((qh * ecum_ref[:, sl]).astype(BF16), s_ref[h].astype(BF16))
            s_ref[h] = s_ref[h] * sdec_ref[:, sl] + _dot_tn((kh * wend_ref[:, sl]).astype(BF16), vh)
            ys.append(_rms(y) * _silu(g_ref[:, sl].astype(F32)))
        _store_rows(y_ref, jnp.concatenate(ys, axis=1).astype(BF16))

    chunk()


def _hgrn_kernel(*refs, c, nc, has_prev):
    (q_ref, f_ref, v_ref, g_ref, lb_ref, nw_ref, s0_ref) = refs[:7]
    y_ref, sout_ref, s_ref = refs[7 + int(has_prev):]
    i = pl.program_id(1)
    n_heads = q_ref.shape[1] // LANES

    @pl.when(i == 0)
    def _():
        for h in range(n_heads):
            s_ref[h] = s0_ref[h].T

    def chunk():
        lb = lb_ref[...]
        forget = lb + (1.0 - lb) * _sigmoid(f_ref[...])
        kk = 1.0 - forget
        gc = _cumsum_time(jnp.log(forget))
        last = gc[c - 1:c, :]
        qg = (_silu(q_ref[...].astype(F32)) * jnp.exp(gc)).astype(BF16)
        kg = (kk * jnp.exp(-gc)).astype(BF16)
        kw = (kk * jnp.exp(last - gc)).astype(BF16)
        sdec = jnp.exp(last)
        causal = _causal(c)
        ys = []
        for h in range(n_heads):
            sl = slice(h * LANES, (h + 1) * LANES)
            vh = v_ref[:, sl]
            scores = jnp.where(causal, _dot_nt(qg[:, sl], kg[:, sl]), 0.0)
            y = _dot(scores.astype(BF16), vh) + _dot_nt(qg[:, sl], s_ref[h].astype(BF16))
            s_ref[h] = s_ref[h] * sdec[:, sl] + _dot_tn(vh, kw[:, sl])
            ys.append(_rms(y) * nw_ref[...] * _silu(g_ref[:, sl].astype(F32)))
        _store_rows(y_ref, jnp.concatenate(ys, axis=1).astype(BF16))

        @pl.when(i == nc - 1)
        def _():
            for h in range(n_heads):
                sout_ref[h] = s_ref[h].T

    chunk()


def _rotation_tables(pos, dk):
    inv = 1.0 / (ROPE_BASE ** jnp.linspace(0.0, 1.0, dk // 2, dtype=F32))
    ang = pos.astype(F32)[:, None] * jnp.repeat(inv, 2)[None, :]
    return jnp.sin(ang), jnp.cos(ang)


def _retention_decay(c, n_heads, dv):
    log_gamma = jnp.log1p(-(2.0 ** (-5.0 - jnp.arange(n_heads, dtype=F32))))
    cum = jnp.cumsum(jnp.broadcast_to(log_gamma, (c, n_heads)), axis=0)
    cum_h = cum.T
    causal = jnp.tril(jnp.ones((c, c), dtype=bool))
    dmat = jnp.exp(jnp.where(causal, cum_h[:, :, None] - cum_h[:, None, :], -jnp.inf))
    ecum = jnp.repeat(jnp.exp(cum), dv, axis=1)
    wend = jnp.repeat(jnp.exp(cum[-1][None, :] - cum), dv, axis=1)
    sdec = jnp.repeat(jnp.exp(cum[-1])[None, :], dv, axis=1)
    return dmat, ecum, wend, sdec


EXPERT_FF_TILE_MAX = 1792
FFN_FF_TILE_MAX = 1408


def _largest_tile(n, cap):
    best = None
    for k in range(1, n // LANES + 1):
        if n % k == 0 and (n // k) % LANES == 0 and n // k <= cap:
            best = n // k
            break
    return n if best is None else best


def kernel(x_prompt, x_sample, state_ssd, state_ssd_conv, state_ret, state_hgrn, meta_tokens, norm_mix, norm_ffn,
           norm_final, w_in_ab, conv_w, conv_b, dt_bias, a_log, d_skip, ssd_norm, w_out_ab, w_ffn_gate, w_ffn_up,
           w_ffn_down, w_in_c, hgrn_lb, hgrn_norm, w_out_c, w_router, w_exp_gate, w_exp_up, w_exp_down):
    bp, seq, d = x_prompt.shape
    bs, dec_seq, _ = x_sample.shape
    n_meta = meta_tokens.shape[0]
    depth = norm_mix.shape[0]
    assert depth == 2 and seq % CHUNK == 0 and n_meta == N_META and d % LANES == 0
    n_ssd_heads = d_skip.shape[1]
    d_inner = n_ssd_heads * SSD_HEADDIM
    bc_w = 2 * SSD_GROUPS * SSD_STATE
    conv_dim = d_inner + bc_w
    ret_w = RET_HEADS * LANES
    assert d_inner == d and ret_w == d and conv_dim == conv_w.shape[2]

    tp, ts = bp * seq, bs * dec_seq
    t_real = tp + ts + n_meta
    tm = 512 if t_real >= 4096 else 64
    t_pad = -(-t_real // tm) * tm
    tm_big = 2 * tm if t_pad % (2 * tm) == 0 else tm
    g_prompt = _Group(0, bp, seq // CHUNK, CHUNK)
    g_sample = _Group(tp, bs, 1, dec_seq)
    g_meta = _Group(tp + ts, 1, 1, n_meta, out_rows=t_pad - tp - ts)
    assert tp % dec_seq == 0 and (tp + ts) % n_meta == 0

    w_in = w_in_ab[0]
    o_z, o_xbc, o_dt, o_q = 0, d_inner, d_inner + conv_dim, d_inner + conv_dim + n_ssd_heads
    n_cols = 6 * d + bc_w
    tn = n_cols // 2 if n_cols % (2 * LANES) == 0 else n_cols
    w_perm, w_dt = _regroup_columns(
        w_in, ((o_z, o_z + d_inner), (o_q, o_q + 4 * ret_w), (o_xbc, o_xbc + conv_dim)),
        (o_dt, o_dt + n_ssd_heads), LANES)
    proj, proj_dt, x0 = _normmm((x_prompt.reshape(tp, d), x_sample.reshape(ts, d), meta_tokens), norm_mix[0],
                                w_perm, w_dt, tm_big, tn, t_pad)
    cb_x, cb_bc = 5, (6 * d) // bc_w
    assert (6 * d) % bc_w == 0

    expand = jnp.asarray(np.arange(LANES)[:, None] == (np.arange(d_inner) // SSD_HEADDIM)[None, :], BF16)
    pad_h = LANES - n_ssd_heads
    ssd_consts = [conv_w[0], conv_b[0].reshape(1, conv_dim), jnp.pad(dt_bias[0], (0, pad_h)).reshape(1, LANES),
                  jnp.pad(a_log[0], (0, pad_h)).reshape(1, LANES),
                  jnp.repeat(d_skip[0], SSD_HEADDIM).reshape(1, d_inner), ssd_norm[0].reshape(1, d_inner), expand]
    def ssd(group, hist0, s0, y_prev):
        nb = group.nb
        return _scan_call(
            _ssd_kernel, group,
            [(proj, d_inner, 0), (proj, d_inner, cb_x), (proj, bc_w, cb_bc), (proj_dt, LANES, 0)],
            ssd_consts, [hist0, s0], y_prev, t_pad, d_inner,
            [jax.ShapeDtypeStruct((nb, n_ssd_heads, SSD_STATE, SSD_HEADDIM), F32),
             jax.ShapeDtypeStruct((nb, SSD_CONV - 1, conv_dim), F32)],
            [_per_stream((nb, n_ssd_heads, SSD_STATE, SSD_HEADDIM)), _per_stream((nb, SSD_CONV - 1, conv_dim))],
            [pltpu.VMEM((HIST_ROWS + group.c, conv_dim), F32),
             pltpu.VMEM((n_ssd_heads // 2, SSD_STATE, LANES), F32)])

    def bcast(s, n):
        return jnp.broadcast_to(s, (n,) + s.shape[1:])

    zeros = jnp.zeros
    y_ssd, s_m, h_m = ssd(g_meta, zeros((1, SSD_CONV - 1, conv_dim), F32),
                          zeros((1, n_ssd_heads, SSD_STATE, SSD_HEADDIM), F32), None)
    y_ssd, ssd_p, conv_p = ssd(g_prompt, bcast(h_m, bp), bcast(s_m, bp), y_ssd)
    y_ssd, ssd_s, conv_s = ssd(g_sample, state_ssd_conv[0], state_ssd[0], y_ssd)

    def ret(group, pos, s0, y_prev):
        nb = group.nb
        sin, cos = _rotation_tables(pos, LANES)
        dmat, ecum, wend, sdec = _retention_decay(group.c, RET_HEADS, LANES)
        return _scan_call(
            _ret_kernel, group,
            [(proj, ret_w, 1), (proj, ret_w, 2), (proj, ret_w, 3), (proj, ret_w, 4)],
            [(sin, group.chunk_rows(LANES)), (cos, group.chunk_rows(LANES)), dmat, ecum, wend, sdec],
            [s0], y_prev, t_pad, ret_w,
            [jax.ShapeDtypeStruct((nb, RET_HEADS, LANES, LANES), F32)],
            [_per_stream((nb, RET_HEADS, LANES, LANES))], [])

    y_ret, r_m = ret(g_meta, jnp.arange(n_meta), zeros((1, RET_HEADS, LANES, LANES), F32), None)
    y_ret, ret_p = ret(g_prompt, n_meta + jnp.arange(seq), bcast(r_m, bp), y_ret)
    y_ret, ret_s = ret(g_sample, n_meta + PAST_LEN + jnp.arange(dec_seq), state_ret[0], y_ret)

    w_out = w_out_ab[0].astype(BF16)
    ff = w_ffn_gate.shape[2]
    tf = _largest_tile(ff, FFN_FF_TILE_MAX)
    x2 = _ffn(x0, [y_ssd, y_ret], [w_out[:d_inner], w_out[d_inner:]], norm_ffn[0], w_ffn_gate[0].astype(BF16),
              w_ffn_up[0].astype(BF16), w_ffn_down[0].astype(BF16), tm, tf)

    w_c = w_in_c[0].astype(BF16)
    w_qig = jnp.concatenate([w_c[:, :d], w_c[:, 2 * d:]], axis=1)
    proj_c, proj_f = _normmm(x2, norm_mix[1], w_qig, w_c[:, d:2 * d], tm_big, 3 * d)
    lb_soft = jax.nn.softmax(hgrn_lb.astype(F32), axis=0)
    lb = (jnp.cumsum(lb_soft, axis=0) - lb_soft[0])[1].reshape(1, d)
    n_hg = d // HG_DK
    hg_consts = [lb, hgrn_norm[0].reshape(1, LANES)]

    def hgrn(group, s0, y_prev):
        nb = group.nb
        return _scan_call(
            _hgrn_kernel, group,
            [(proj_c, d, 0), (proj_f, d, 0), (proj_c, d, 1), (proj_c, d, 2)],
            hg_consts, [s0], y_prev, t_pad, d,
            [jax.ShapeDtypeStruct((nb, n_hg, HG_DK, LANES), F32)],
            [_per_stream((nb, n_hg, HG_DK, LANES))], [pltpu.VMEM((n_hg, LANES, HG_DK), F32)])

    y_hg, g_m = hgrn(g_meta, zeros((1, n_hg, HG_DK, LANES), F32), None)
    y_hg, hg_p = hgrn(g_prompt, bcast(g_m, bp), y_hg)
    y_hg, hg_s = hgrn(g_sample, state_hgrn[0], y_hg)

    n_exp = w_router.shape[2]
    x3, route, route_t, counts = _router(x2, y_hg, w_out_c[0].astype(BF16), norm_ffn[1], w_router[0], tm_big, tm, t_real)
    tmb = tm
    n_blocks = (TOP_K * t_real + n_exp * (tmb - 1)) // tmb
    pos, blk_e, blk_n = _moe_plan(route_t, counts, n_exp, tmb, n_blocks, TOP_K, t_real)
    xs = _dispatch(pos, x3, n_blocks * tmb + TOP_K * (t_pad - t_real), tm, TOP_K)
    ffe = w_exp_gate.shape[3]
    tfe = _largest_tile(ffe, EXPERT_FF_TILE_MAX)
    ys = _experts(blk_e, blk_n, xs, norm_ffn[1], w_exp_gate[0].astype(BF16), w_exp_up[0].astype(BF16),
                  w_exp_down[0].astype(BF16), tmb, tfe)
    y_prompt, y_sample = _combine(pos, x3, route, norm_final, ys, tm, TOP_K, tp, ts)
    y_prompt = y_prompt.reshape(bp, seq, d)
    y_sample = y_sample.reshape(bs, dec_seq, d)
    return (y_prompt, y_sample, ssd_p[None], conv_p[None], ret_p[None], hg_p[None],
            ssd_s[None], conv_s[None], ret_s[None], hg_s[None])
```

```python
import functools

import jax
import jax.numpy as jnp
import numpy as np
from jax import lax
from jax.experimental import pallas as pl
from jax.experimental.pallas import tpu as pltpu

F32 = jnp.float32
BF16 = jnp.bfloat16

CHUNK = 64
N_META = 16
PAST_LEN = 2048
EPS = 1e-6
SSD_HEADDIM = 64
SSD_GROUPS = 2
SSD_STATE = 128
SSD_CONV = 4
RET_HEADS = 8
ROPE_BASE = 10000.0
HG_DK = 128
TOP_K = 2

LANES = 128
SUBLANES = 8
HIST_ROWS = 8
VMEM_LIMIT = 56 * 1024 * 1024
NEG_BIG = -1e30


def _dot(a, b):
    return jnp.dot(a, b, preferred_element_type=F32)


def _dot_nt(a, b):
    return lax.dot_general(a, b, (((1,), (1,)), ((), ())), preferred_element_type=F32)


def _dot_tn(a, b):
    return lax.dot_general(a, b, (((0,), (0,)), ((), ())), preferred_element_type=F32)


def _split3(x):
    hi = x.astype(BF16)
    r = x - hi.astype(F32)
    mid = r.astype(BF16)
    lo = (r - mid.astype(F32)).astype(BF16)
    return hi, mid, lo


def _exact_ldot(x, m01):
    hi, mid, lo = _split3(x)
    return _dot(hi, m01) + _dot(mid, m01) + _dot(lo, m01)


def _exact_rdot(m01, x):
    hi, mid, lo = _split3(x)
    return _dot(m01, hi) + _dot(m01, mid) + _dot(m01, lo)


def _causal(c):
    row = lax.broadcasted_iota(jnp.int32, (c, c), 0)
    col = lax.broadcasted_iota(jnp.int32, (c, c), 1)
    return row >= col


def _cumsum_time(x):
    c = x.shape[0]
    return _exact_rdot(jnp.where(_causal(c), 1.0, 0.0).astype(BF16), x)


def _sigmoid(x):
    return 1.0 / (1.0 + jnp.exp(-x))


def _silu(x):
    return x * _sigmoid(x)


def _rms(x):
    return x * lax.rsqrt(jnp.mean(x * x, axis=-1, keepdims=True) + EPS)


def _normmm_kernel(*refs, n_src, n_lead_tiles):
    xs, (nw_ref, w_ref, ws_ref, o_ref, os_ref) = refs[:n_src], refs[n_src:n_src + 5]
    xn_scr = refs[-1]

    @pl.when(pl.program_id(1) == 0)
    def _():
        if n_src == 1:
            x_ref = xs[0]
        else:
            lead_ref, tail_ref, meta_ref = xs
            x_ref = refs[n_src + 5]
            i = pl.program_id(0)
            tm = x_ref.shape[0]
            n_tail, n_meta = tail_ref.shape[0], meta_ref.shape[0]

            @pl.when(i < n_lead_tiles)
            def _():
                x_ref[...] = lead_ref[...]

            for t in range(pl.cdiv(n_tail + n_meta, tm)):
                @pl.when(i == n_lead_tiles + t)
                def _(lo=t * tm):
                    a, b = lo, min(lo + tm, n_tail)
                    if a < b:
                        x_ref[a - lo:b - lo, :] = tail_ref[a:b, :]
                    a, b = max(lo, n_tail), min(lo + tm, n_tail + n_meta)
                    if a < b:
                        x_ref[a - lo:b - lo, :] = meta_ref[a - n_tail:b - n_tail, :]
                    if b < lo + tm:
                        x_ref[b - lo:, :] = jnp.zeros((lo + tm - b, x_ref.shape[1]), x_ref.dtype)

        xn = (_rms(x_ref[...]) * nw_ref[...]).astype(BF16)
        xn_scr[...] = xn
        os_ref[...] = _dot(xn, ws_ref[...])

    o_ref[...] = _dot(xn_scr[...], w_ref[...]).astype(BF16)


def _normmm(x, nw, w, w_side, tm, tn, t_pad=None):
    sources = x if isinstance(x, tuple) else (x,)
    k = sources[0].shape[1]
    n, ns = w.shape[1], w_side.shape[1]
    out_specs = [pl.BlockSpec((tm, tn), lambda i, j: (i, j)), pl.BlockSpec((tm, ns), lambda i, j: (i, 0))]
    if len(sources) == 1:
        t, n_lead = x.shape[0], 0
        x_specs = [pl.BlockSpec((tm, k), lambda i, j: (i, 0))]
        out_shape = [jax.ShapeDtypeStruct((t, n), BF16), jax.ShapeDtypeStruct((t, ns), F32)]
    else:
        lead, tail, meta = sources
        t, n_lead = t_pad, lead.shape[0] // tm
        assert lead.shape[0] % tm == 0 and (n_lead + pl.cdiv(tail.shape[0] + meta.shape[0], tm)) * tm == t_pad
        x_specs = [pl.BlockSpec((tm, k), lambda i, j: (jnp.minimum(i, n_lead - 1), 0)),
                   pl.BlockSpec(tail.shape, lambda i, j: (0, 0)),
                   pl.BlockSpec(meta.shape, lambda i, j: (0, 0))]
        out_specs.append(pl.BlockSpec((tm, k), lambda i, j: (i, 0)))
        out_shape = [jax.ShapeDtypeStruct((t, n), BF16), jax.ShapeDtypeStruct((t, ns), F32),
                     jax.ShapeDtypeStruct((t, k), F32)]
    return pl.pallas_call(
        functools.partial(_normmm_kernel, n_src=len(sources), n_lead_tiles=n_lead),
        grid=(t // tm, n // tn),
        in_specs=x_specs + [pl.BlockSpec((1, k), lambda i, j: (0, 0)),
                            pl.BlockSpec((k, tn), lambda i, j: (0, j)),
                            pl.BlockSpec((k, ns), lambda i, j: (0, 0))],
        out_specs=out_specs,
        out_shape=out_shape,
        scratch_shapes=[pltpu.VMEM((tm, k), BF16)],
        compiler_params=pltpu.CompilerParams(dimension_semantics=("parallel", "arbitrary"),
                                             vmem_limit_bytes=VMEM_LIMIT),
    )(*sources, nw.reshape(1, k), w, w_side)


def _regroup_kernel(w_ref, o_ref, os_ref, *, pieces, side_piece):
    col = 0
    for a, b in pieces:
        o_ref[:, col:col + b - a] = w_ref[:, a:b].astype(BF16)
        col += b - a
    a, b = side_piece
    os_ref[...] = jnp.zeros_like(os_ref)
    os_ref[:, 0:b - a] = w_ref[:, a:b].astype(BF16)


def _regroup_columns(w, pieces, side_piece, rows):
    k, n = w.shape
    n_out = sum(b - a for a, b in pieces)
    return pl.pallas_call(
        functools.partial(_regroup_kernel, pieces=pieces, side_piece=side_piece),
        grid=(k // rows,),
        in_specs=[pl.BlockSpec((rows, n), lambda i: (i, 0))],
        out_specs=[pl.BlockSpec((rows, n_out), lambda i: (i, 0)), pl.BlockSpec((rows, LANES), lambda i: (i, 0))],
        out_shape=[jax.ShapeDtypeStruct((k, n_out), BF16), jax.ShapeDtypeStruct((k, LANES), BF16)],
        compiler_params=pltpu.CompilerParams(dimension_semantics=("parallel",), vmem_limit_bytes=VMEM_LIMIT),
    )(w)


def _ffn_kernel(*refs, n_in):
    x_ref = refs[0]
    ys = refs[1:1 + n_in]
    ws = refs[1 + n_in:1 + 2 * n_in]
    nw_ref, wg_ref, wu_ref, wd_ref, o_ref, xn_scr, acc_scr = refs[1 + 2 * n_in:]
    f = pl.program_id(1)

    @pl.when(f == 0)
    def _():
        x = x_ref[...]
        for y, w in zip(ys, ws):
            x = x + _dot(y[...], w[...])
        xn_scr[...] = (_rms(x) * nw_ref[...]).astype(BF16)
        acc_scr[...] = x

    xn = xn_scr[...]
    h = _silu(_dot(xn, wg_ref[...])) * _dot(xn, wu_ref[...])
    acc_scr[...] += _dot(h.astype(BF16), wd_ref[...])

    @pl.when(f == pl.num_programs(1) - 1)
    def _():
        o_ref[...] = acc_scr[...]


def _ffn(x, ys, ws, nw, wg, wu, wd, tm, tf):
    t, d = x.shape
    ff = wg.shape[1]
    n_in = len(ys)
    in_specs = [pl.BlockSpec((tm, d), lambda i, f: (i, 0))]
    in_specs += [pl.BlockSpec((tm, y.shape[1]), lambda i, f: (i, 0)) for y in ys]
    in_specs += [pl.BlockSpec(w.shape, lambda i, f: (0, 0)) for w in ws]
    in_specs += [pl.BlockSpec((1, d), lambda i, f: (0, 0)),
                 pl.BlockSpec((d, tf), lambda i, f: (0, f)),
                 pl.BlockSpec((d, tf), lambda i, f: (0, f)),
                 pl.BlockSpec((tf, d), lambda i, f: (f, 0))]
    return pl.pallas_call(
        functools.partial(_ffn_kernel, n_in=n_in),
        grid=(t // tm, ff // tf),
        in_specs=in_specs,
        out_specs=pl.BlockSpec((tm, d), lambda i, f: (i, 0)),
        out_shape=jax.ShapeDtypeStruct((t, d), F32),
        scratch_shapes=[pltpu.VMEM((tm, d), BF16), pltpu.VMEM((tm, d), F32)],
        compiler_params=pltpu.CompilerParams(dimension_semantics=("parallel", "arbitrary"),
                                             vmem_limit_bytes=VMEM_LIMIT),
    )(x, *ys, *ws, nw.reshape(1, d), wg, wu, wd)


ROUTE_E1, ROUTE_E2, ROUTE_R1, ROUTE_R2, ROUTE_G1, ROUTE_G2 = range(6)


def _router_kernel(x_ref, y_ref, wo_ref, nw_ref, whi_ref, wlo_ref, earlier_ref, xo_ref, route_ref, routet_ref,
                   count_ref, *, n_experts, t_real):
    i = pl.program_id(0)
    tm = x_ref.shape[0]
    sub = earlier_ref.shape[0]
    e_rows = count_ref.shape[0]

    @pl.when(i == 0)
    def _():
        count_ref[...] = jnp.zeros_like(count_ref)

    count = count_ref[:, 0:1]
    row = lax.broadcasted_iota(jnp.int32, (e_rows, sub), 0)
    row_f = row.astype(F32)
    field = lax.broadcasted_iota(jnp.int32, (LANES, sub), 0)
    for r0 in range(0, tm, sub):
        rows = slice(r0, r0 + sub)
        xo = x_ref[rows, :] + _dot(y_ref[rows, :], wo_ref[...])
        xo_ref[rows, :] = xo
        xn = _rms(xo) * nw_ref[...]
        hi = xn.astype(BF16)
        lo = (xn - hi.astype(F32)).astype(BF16)
        logits = _dot(hi, whi_ref[...]) + _dot(lo, whi_ref[...]) + _dot(hi, wlo_ref[...])
        lt = jnp.where(row < n_experts, logits.T[0:e_rows, :], NEG_BIG)
        m1 = jnp.max(lt, axis=0, keepdims=True)
        i1 = jnp.min(jnp.where(lt == m1, row_f, float(e_rows)), axis=0, keepdims=True)
        rest = jnp.where(row_f == i1, NEG_BIG, lt)
        m2 = jnp.max(rest, axis=0, keepdims=True)
        i2 = jnp.min(jnp.where(rest == m2, row_f, float(e_rows)), axis=0, keepdims=True)
        e2 = jnp.exp(m2 - m1)
        g1 = 1.0 / (1.0 + e2)
        g2 = e2 / (1.0 + e2)

        tok = lax.broadcasted_iota(jnp.int32, (1, sub), 1) + (i * tm + r0)
        valid = tok < t_real
        sel = jnp.where(valid & ((row_f == i1) | (row_f == i2)), 1.0, 0.0)
        rank = count + _dot(sel.astype(BF16), earlier_ref[...])
        r1 = jnp.sum(jnp.where(row_f == i1, rank, 0.0), axis=0, keepdims=True)
        r2 = jnp.sum(jnp.where(row_f == i2, rank, 0.0), axis=0, keepdims=True)
        count = count + jnp.sum(sel, axis=1, keepdims=True)

        rec = jnp.zeros((LANES, sub), F32)
        for k, v in ((ROUTE_E1, i1), (ROUTE_E2, i2), (ROUTE_R1, r1), (ROUTE_R2, r2), (ROUTE_G1, g1),
                     (ROUTE_G2, g2)):
            rec = jnp.where(field == k, v, rec)
        rec = jnp.where(valid, rec, 0.0)
        route_ref[rows, :] = rec.T
        routet_ref[:, rows] = rec[0:SUBLANES, :]
    count_ref[...] = jnp.broadcast_to(count, count_ref.shape)


def _router(x, y, wo, nw, w_router, tm, sub, t_real):
    t, d = x.shape
    e = w_router.shape[1]
    wpad = jnp.zeros((d, LANES), F32).at[:, :e].set(w_router)
    whi = wpad.astype(BF16)
    wlo = (wpad - whi.astype(F32)).astype(BF16)
    e_rows = -(-e // SUBLANES) * SUBLANES
    earlier = jnp.asarray(np.arange(sub)[:, None] < np.arange(sub)[None, :], BF16)
    return pl.pallas_call(
        functools.partial(_router_kernel, n_experts=e, t_real=t_real),
        grid=(t // tm,),
        in_specs=[pl.BlockSpec((tm, d), lambda i: (i, 0)),
                  pl.BlockSpec((tm, y.shape[1]), lambda i: (i, 0)),
                  pl.BlockSpec(wo.shape, lambda i: (0, 0)),
                  pl.BlockSpec((1, d), lambda i: (0, 0)),
                  pl.BlockSpec((d, LANES), lambda i: (0, 0)),
                  pl.BlockSpec((d, LANES), lambda i: (0, 0)),
                  pl.BlockSpec((sub, sub), lambda i: (0, 0))],
        out_specs=[pl.BlockSpec((tm, d), lambda i: (i, 0)), pl.BlockSpec((tm, LANES), lambda i: (i, 0)),
                   pl.BlockSpec((SUBLANES, tm), lambda i: (0, i)), pl.BlockSpec((e_rows, LANES), lambda i: (0, 0))],
        out_shape=[jax.ShapeDtypeStruct((t, d), F32), jax.ShapeDtypeStruct((t, LANES), F32),
                   jax.ShapeDtypeStruct((SUBLANES, t), F32), jax.ShapeDtypeStruct((e_rows, LANES), F32)],
        compiler_params=pltpu.CompilerParams(dimension_semantics=("arbitrary",),
                                             vmem_limit_bytes=VMEM_LIMIT),
    )(x, y, wo, nw.reshape(1, d), whi, wlo, earlier)


def _row_copy(src, src_row, dst, dst_row, sem):
    return pltpu.make_async_copy(src.at[pl.ds(src_row, 1)], dst.at[pl.ds(dst_row, 1)], sem)


def _dispatch_kernel(pos_ref, x_ref, xs_hbm, sem, *, tm, top_k, n_tok):
    i = pl.program_id(0)

    def start(g, carry):
        r0 = pl.multiple_of(g * SUBLANES, SUBLANES)
        for b in range(SUBLANES):
            for k in range(top_k):
                _row_copy(x_ref, r0 + b, xs_hbm, pos_ref[k * n_tok + i * tm + r0 + b], sem).start()
        return carry

    def wait(g, carry):
        for _ in range(SUBLANES * top_k):
            _row_copy(x_ref, 0, xs_hbm, 0, sem).wait()
        return carry

    lax.fori_loop(0, tm // SUBLANES, start, 0)
    lax.fori_loop(0, tm // SUBLANES, wait, 0)


def _dispatch(pos, x, n_rows_out, tm, top_k):
    t, d = x.shape
    return pl.pallas_call(
        functools.partial(_dispatch_kernel, tm=tm, top_k=top_k, n_tok=t),
        grid_spec=pltpu.PrefetchScalarGridSpec(
            num_scalar_prefetch=1, grid=(t // tm,),
            in_specs=[pl.BlockSpec((tm, d), lambda i, p: (i, 0))],
            out_specs=pl.BlockSpec(memory_space=pl.ANY),
            scratch_shapes=[pltpu.SemaphoreType.DMA(())]),
        out_shape=jax.ShapeDtypeStruct((n_rows_out, d), F32),
        compiler_params=pltpu.CompilerParams(dimension_semantics=("arbitrary",),
                                             vmem_limit_bytes=VMEM_LIMIT),
    )(pos, x)


def _expert_kernel(blk_e_ref, blk_n_ref, xs_ref, nw_ref, wg_ref, wu_ref, wd_ref, o_ref, xn_scr, acc_scr):
    del blk_e_ref
    j = pl.program_id(0)
    f = pl.program_id(1)
    n_valid = blk_n_ref[j]
    last = f == pl.num_programs(1) - 1

    @pl.when((n_valid > 0) & (f == 0))
    def _():
        row = lax.broadcasted_iota(jnp.int32, (xs_ref.shape[0], 1), 0)
        x = jnp.where(row < n_valid, xs_ref[...], 0.0)
        xn_scr[...] = (_rms(x) * nw_ref[...]).astype(BF16)
        acc_scr[...] = jnp.zeros_like(acc_scr)

    @pl.when(n_valid > 0)
    def _():
        xn = xn_scr[...]
        h = _silu(_dot(xn, wg_ref[...])) * _dot(xn, wu_ref[...])
        acc_scr[...] += _dot(h.astype(BF16), wd_ref[...])

    @pl.when((n_valid > 0) & last)
    def _():
        o_ref[...] = acc_scr[...]

    @pl.when((n_valid == 0) & last)
    def _():
        o_ref[...] = jnp.zeros_like(o_ref)


def _experts(blk_e, blk_n, xs, nw, wg, wu, wd, tmb, tf):
    d = xs.shape[1]
    s = blk_e.shape[0] * tmb
    ff = wg.shape[2]
    nf = ff // tf

    def w_up(j, f, be, bn):
        return (be[j], 0, jnp.where(bn[j] > 0, f, nf - 1))

    def w_down(j, f, be, bn):
        return (be[j], jnp.where(bn[j] > 0, f, nf - 1), 0)

    return pl.pallas_call(
        _expert_kernel,
        grid_spec=pltpu.PrefetchScalarGridSpec(
            num_scalar_prefetch=2, grid=(s // tmb, nf),
            in_specs=[pl.BlockSpec((tmb, d), lambda j, f, be, bn: (j, 0)),
                      pl.BlockSpec((1, d), lambda j, f, be, bn: (0, 0)),
                      pl.BlockSpec((None, d, tf), w_up),
                      pl.BlockSpec((None, d, tf), w_up),
                      pl.BlockSpec((None, tf, d), w_down)],
            out_specs=pl.BlockSpec((tmb, d), lambda j, f, be, bn: (j, 0)),
            scratch_shapes=[pltpu.VMEM((tmb, d), BF16), pltpu.VMEM((tmb, d), F32)]),
        out_shape=jax.ShapeDtypeStruct((s, d), F32),
        compiler_params=pltpu.CompilerParams(dimension_semantics=("parallel", "arbitrary"),
                                             vmem_limit_bytes=VMEM_LIMIT),
    )(blk_e, blk_n, xs, nw.reshape(1, d), wg, wu, wd)


def _combine_kernel(pos_ref, x_ref, route_ref, nw_ref, ys_hbm, op_ref, os_ref, ybuf, sem, *, tm, top_k, n_tok,
                    n_prompt_tiles):
    i = pl.program_id(0)

    def start(g, carry):
        r0 = pl.multiple_of(g * SUBLANES, SUBLANES)
        for b in range(SUBLANES):
            for k in range(top_k):
                _row_copy(ys_hbm, pos_ref[k * n_tok + i * tm + r0 + b], ybuf.at[k], r0 + b, sem).start()
        return carry

    def wait(g, carry):
        for _ in range(SUBLANES * top_k):
            _row_copy(ys_hbm, 0, ybuf.at[0], 0, sem).wait()
        return carry

    lax.fori_loop(0, tm // SUBLANES, start, 0)
    lax.fori_loop(0, tm // SUBLANES, wait, 0)
    route = route_ref[...]
    out = x_ref[...] + route[:, ROUTE_G1:ROUTE_G1 + 1] * ybuf[0] + route[:, ROUTE_G2:ROUTE_G2 + 1] * ybuf[1]
    y = _rms(out) * nw_ref[...]

    @pl.when(i < n_prompt_tiles)
    def _():
        op_ref[...] = y

    @pl.when(i >= n_prompt_tiles)
    def _():
        os_ref[...] = y


def _combine(pos, x, route, nw, ys, tm, top_k, tp, ts):
    d = x.shape[1]
    assert top_k == 2 and tp % tm == 0 and ts % tm == 0
    npt, nst = tp // tm, ts // tm
    return pl.pallas_call(
        functools.partial(_combine_kernel, tm=tm, top_k=top_k, n_tok=x.shape[0], n_prompt_tiles=npt),
        grid_spec=pltpu.PrefetchScalarGridSpec(
            num_scalar_prefetch=1, grid=(npt + nst,),
            in_specs=[pl.BlockSpec((tm, d), lambda i, p: (i, 0)),
                      pl.BlockSpec((tm, LANES), lambda i, p: (i, 0)),
                      pl.BlockSpec((1, d), lambda i, p: (0, 0)),
                      pl.BlockSpec(memory_space=pl.ANY)],
            out_specs=[pl.BlockSpec((tm, d), lambda i, p: (jnp.minimum(i, npt - 1), 0)),
                       pl.BlockSpec((tm, d), lambda i, p: (jnp.maximum(i - npt, 0), 0))],
            scratch_shapes=[pltpu.VMEM((top_k, tm, d), F32), pltpu.SemaphoreType.DMA(())]),
        out_shape=[jax.ShapeDtypeStruct((tp, d), F32), jax.ShapeDtypeStruct((ts, d), F32)],
        compiler_params=pltpu.CompilerParams(dimension_semantics=("arbitrary",),
                                             vmem_limit_bytes=VMEM_LIMIT),
    )(pos, x, route, nw.reshape(1, d), ys)


def _moe_plan(route_t, counts, n_experts, tmb, n_blocks, top_k, t_real):
    cnt = counts[:n_experts, 0].astype(jnp.int32)
    nblk = (cnt + tmb - 1) // tmb
    blk_end = jnp.cumsum(nblk)
    blk_start = blk_end - nblk
    slot0 = blk_start * tmb
    t = route_t.shape[1]
    tok = jnp.arange(t)
    experts = jnp.arange(n_experts)[:, None]
    pos = []
    for k in range(top_k):
        e_k = route_t[ROUTE_E1 + k].astype(jnp.int32)
        rank = route_t[ROUTE_R1 + k].astype(jnp.int32)
        sorted_slot = jnp.sum(jnp.where(e_k[None, :] == experts, slot0[:, None], 0), axis=0) + rank
        scratch = n_blocks * tmb + (tok - t_real) * top_k + k
        pos.append(jnp.where(tok >= t_real, scratch, sorted_slot))
    j = jnp.arange(n_blocks)
    used = j < blk_end[-1]
    blk_e = jnp.minimum(jnp.sum(j[:, None] >= blk_end[None, :], axis=1), n_experts - 1)
    last_e = jnp.max(jnp.where(nblk > 0, jnp.arange(n_experts), 0))
    blk_e = jnp.where(used, blk_e, last_e).astype(jnp.int32)
    blk_n = jnp.where(used, jnp.clip(cnt[blk_e] - (j - blk_start[blk_e]) * tmb, 0, tmb), 0).astype(jnp.int32)
    return jnp.concatenate(pos).astype(jnp.int32), blk_e, blk_n


class _Group:
    def __init__(self, row0, nb, nc, c, out_rows=None):
        self.row0, self.nb, self.nc, self.c = row0, nb, nc, c
        self.out_rows = c if out_rows is None else out_rows
        assert self.out_rows == c or (nb == 1 and nc == 1 and row0 % self.out_rows == 0)

    def rows(self, width, col_block):
        base, nc, c = self.row0 // self.c, self.nc, self.c
        return pl.BlockSpec((c, width), lambda b, i: (base + b * nc + i, col_block))

    def out_block(self, width):
        base, nc, r = self.row0 // self.out_rows, self.nc, self.out_rows
        return pl.BlockSpec((r, width), lambda b, i: (base + b * nc + i, 0))

    def chunk_rows(self, width):
        return pl.BlockSpec((self.c, width), lambda b, i: (i, 0))


def _full(shape):
    nd = len(shape)
    return pl.BlockSpec(tuple(shape), lambda b, i: (0,) * nd)


def _per_stream(shape):
    nd = len(shape)
    return pl.BlockSpec((None,) + tuple(shape[1:]), lambda b, i: (b,) + (0,) * (nd - 1))


def _scan_call(kernel, group, row_inputs, const_inputs, stream_inputs, y_prev, t_pad, width, extra_out_shapes,
               extra_out_specs, scratch_shapes):
    arrays, specs = [], []
    for arr, w, cb in row_inputs:
        arrays.append(arr)
        specs.append(group.rows(w, cb))
    for item in const_inputs:
        if isinstance(item, tuple):
            arr, spec = item
        else:
            arr, spec = item, _full(item.shape)
        arrays.append(arr)
        specs.append(spec)
    for arr in stream_inputs:
        arrays.append(arr)
        specs.append(_per_stream(arr.shape))
    aliases = {}
    if y_prev is not None:
        aliases = {len(arrays): 0}
        arrays.append(y_prev)
        specs.append(pl.BlockSpec(memory_space=pl.ANY))
    out_shape = [jax.ShapeDtypeStruct((t_pad, width), BF16)] + list(extra_out_shapes)
    out_specs = [group.out_block(width)] + list(extra_out_specs)
    return pl.pallas_call(
        functools.partial(kernel, c=group.c, nc=group.nc, has_prev=y_prev is not None),
        grid=(group.nb, group.nc),
        in_specs=specs,
        out_specs=out_specs,
        out_shape=out_shape,
        scratch_shapes=scratch_shapes,
        input_output_aliases=aliases,
        compiler_params=pltpu.CompilerParams(dimension_semantics=("parallel", "arbitrary"),
                                             vmem_limit_bytes=VMEM_LIMIT),
    )(*arrays)


def _store_rows(y_ref, y):
    c = y.shape[0]
    y_ref[0:c, :] = y
    if y_ref.shape[0] > c:
        y_ref[c:, :] = jnp.zeros((y_ref.shape[0] - c, y_ref.shape[1]), y_ref.dtype)


def _ssd_kernel(*refs, c, nc, has_prev):
    (z_ref, x_ref, bc_ref, dt_ref, convw_ref, convb_ref, dtb_ref, alog_ref, dskip_ref, nw_ref, expand_ref,
     hist0_ref, s0_ref) = refs[:13]
    y_ref, sout_ref, hout_ref, hist_scr, s_ref = refs[13 + int(has_prev):]
    i = pl.program_id(1)
    d_inner = x_ref.shape[1]
    n_state = SSD_STATE
    n_pairs = d_inner // LANES
    pairs_per_group = n_pairs // SSD_GROUPS
    n_hist = SSD_CONV - 1

    @pl.when(i == 0)
    def _():
        for p in range(n_pairs):
            s_ref[p] = jnp.concatenate([s0_ref[2 * p], s0_ref[2 * p + 1]], axis=1)
        hist_scr[HIST_ROWS - n_hist:HIST_ROWS, :] = hist0_ref[...]

    def chunk():
        hist_scr[HIST_ROWS:HIST_ROWS + c, 0:d_inner] = x_ref[...].astype(F32)
        hist_scr[HIST_ROWS:HIST_ROWS + c, d_inner:] = bc_ref[...].astype(F32)
        conv = convb_ref[...]
        for j in range(SSD_CONV):
            conv = conv + hist_scr[HIST_ROWS - n_hist + j:HIST_ROWS - n_hist + j + c, :] * convw_ref[j:j + 1, :]
        tail = hist_scr[HIST_ROWS + c - n_hist:HIST_ROWS + c, :]
        hist_scr[HIST_ROWS - n_hist:HIST_ROWS, :] = tail
        hout_ref[...] = tail
        xbc = _silu(conv)
        xs = xbc[:, 0:d_inner]
        bm = xbc[:, d_inner:d_inner + SSD_GROUPS * n_state]
        cm = xbc[:, d_inner + SSD_GROUPS * n_state:]

        x_dt = dt_ref[...] + dtb_ref[...]
        dt = jnp.maximum(x_dt, 0.0) + jnp.log1p(jnp.exp(-jnp.abs(x_dt)))
        log_a = dt * (-jnp.exp(alog_ref[...]))
        cum = _cumsum_time(log_a)
        expand = expand_ref[...]
        dt_e = _exact_ldot(dt, expand)
        cum_e = _exact_ldot(cum, expand)
        last_e = cum_e[c - 1:c, :]
        xdt = xs * dt_e
        wx = jnp.exp(last_e - cum_e) * xdt
        ecum = jnp.exp(cum_e)
        sdecay = jnp.exp(last_e)

        row = lax.broadcasted_iota(jnp.int32, (c, 2 * c), 0)
        col = lax.broadcasted_iota(jnp.int32, (c, 2 * c), 1)
        second = col >= c
        tcol = jnp.where(second, col - c, col)
        causal2 = tcol <= row
        diag2 = tcol == row
        r2 = lax.broadcasted_iota(jnp.int32, (2 * c, LANES), 0)
        l2 = lax.broadcasted_iota(jnp.int32, (2 * c, LANES), 1)
        half2 = (r2 >= c) == (l2 >= SSD_HEADDIM)

        ys = []
        for g in range(SSD_GROUPS):
            cg = cm[:, g * n_state:(g + 1) * n_state].astype(BF16)
            bg = bm[:, g * n_state:(g + 1) * n_state].astype(BF16)
            scores2 = _dot_nt(cg, jnp.concatenate([bg, bg], axis=0))
            for p in range(g * pairs_per_group, (g + 1) * pairs_per_group):
                sl = slice(p * LANES, (p + 1) * LANES)
                cum_p = cum_e[:, sl]
                cum_col = jnp.where(second, cum_p[:, SSD_HEADDIM:SSD_HEADDIM + 1], cum_p[:, 0:1])
                cum_row = jnp.sum(jnp.where(diag2, cum_col, 0.0), axis=0, keepdims=True)
                seg = cum_col - cum_row
                m2 = (scores2 * jnp.exp(jnp.where(causal2, seg, NEG_BIG))).astype(BF16)
                xp = xdt[:, sl]
                x2 = jnp.where(half2, jnp.concatenate([xp, xp], axis=0), 0.0).astype(BF16)
                y = _dot(m2, x2)
                y = y + _dot(cg, s_ref[p].astype(BF16)) * ecum[:, sl]
                s_ref[p] = s_ref[p] * sdecay[:, sl] + _dot_tn(bg, wx[:, sl].astype(BF16))
                ys.append(y)
        y = jnp.concatenate(ys, axis=1)
        y = y + xs * dskip_ref[...]
        y = y * _silu(z_ref[...].astype(F32))
        gw = d_inner // SSD_GROUPS
        y = jnp.concatenate([_rms(y[:, g * gw:(g + 1) * gw]) for g in range(SSD_GROUPS)], axis=1)
        _store_rows(y_ref, (y * nw_ref[...]).astype(BF16))

        @pl.when(i == nc - 1)
        def _():
            for p in range(n_pairs):
                sout_ref[2 * p] = s_ref[p][:, 0:SSD_HEADDIM]
                sout_ref[2 * p + 1] = s_ref[p][:, SSD_HEADDIM:]

    chunk()


def _ret_kernel(*refs, c, nc, has_prev):
    (q_ref, k_ref, v_ref, g_ref, sin_ref, cos_ref, dmat_ref, ecum_ref, wend_ref, sdec_ref, s0_ref) = refs[:11]
    y_ref, s_ref = refs[11 + int(has_prev):]
    i = pl.program_id(1)
    n_heads = q_ref.shape[1] // LANES

    @pl.when(i == 0)
    def _():
        s_ref[...] = s0_ref[...]

    def chunk():
        sin = sin_ref[...]
        cos = cos_ref[...]
        even = (lax.broadcasted_iota(jnp.int32, (c, LANES), 1) % 2) == 0

        def rotate(x):
            nxt = pltpu.roll(x, LANES - 1, 1)
            prv = pltpu.roll(x, 1, 1)
            return x * cos + jnp.where(even, -nxt, prv) * sin

        scale = LANES ** -0.5
        ys = []
        for h in range(n_heads):
            sl = slice(h * LANES, (h + 1) * LANES)
            qh = rotate(q_ref[:, sl].astype(F32))
            kh = rotate(k_ref[:, sl].astype(F32)) * scale
            vh = v_ref[:, sl]
            scores = _dot_nt(qh.astype(BF16), kh.astype(BF16)) * dmat_ref[h]
            y = _dot(scores.astype(BF16), vh) + _dot((qh * ecum_ref[:, sl]).astype(BF16), s_ref[h].astype(BF16))
            s_ref[h] = s_ref[h] * sdec_ref[:, sl] + _dot_tn((kh * wend_ref[:, sl]).astype(BF16), vh)
            ys.append(_rms(y) * _silu(g_ref[:, sl].astype(F32)))
        _store_rows(y_ref, jnp.concatenate(ys, axis=1).astype(BF16))

    chunk()


def _hgrn_kernel(*refs, c, nc, has_prev):
    (q_ref, f_ref, v_ref, g_ref, lb_ref, nw_ref, s0_ref) = refs[:7]
    y_ref, sout_ref, s_ref = refs[7 + int(has_prev):]
    i = pl.program_id(1)
    n_heads = q_ref.shape[1] // LANES

    @pl.when(i == 0)
    def _():
        for h in range(n_heads):
            s_ref[h] = s0_ref[h].T

    def chunk():
        lb = lb_ref[...]
        forget = lb + (1.0 - lb) * _sigmoid(f_ref[...])
        kk = 1.0 - forget
        gc = _cumsum_time(jnp.log(forget))
        last = gc[c - 1:c, :]
        qg = (_silu(q_ref[...].astype(F32)) * jnp.exp(gc)).astype(BF16)
        kg = (kk * jnp.exp(-gc)).astype(BF16)
        kw = (kk * jnp.exp(last - gc)).astype(BF16)
        sdec = jnp.exp(last)
        causal = _causal(c)
        ys = []
        for h in range(n_heads):
            sl = slice(h * LANES, (h + 1) * LANES)
            vh = v_ref[:, sl]
            scores = jnp.where(causal, _dot_nt(qg[:, sl], kg[:, sl]), 0.0)
            y = _dot(scores.astype(BF16), vh) + _dot_nt(qg[:, sl], s_ref[h].astype(BF16))
            s_ref[h] = s_ref[h] * sdec[:, sl] + _dot_tn(vh, kw[:, sl])
            ys.append(_rms(y) * nw_ref[...] * _silu(g_ref[:, sl].astype(F32)))
        _store_rows(y_ref, jnp.concatenate(ys, axis=1).astype(BF16))

        @pl.when(i == nc - 1)
        def _():
            for h in range(n_heads):
                sout_ref[h] = s_ref[h].T

    chunk()


def _rotation_tables(pos, dk):
    inv = 1.0 / (ROPE_BASE ** jnp.linspace(0.0, 1.0, dk // 2, dtype=F32))
    ang = pos.astype(F32)[:, None] * jnp.repeat(inv, 2)[None, :]
    return jnp.sin(ang), jnp.cos(ang)


def _retention_decay(c, n_heads, dv):
    log_gamma = jnp.log1p(-(2.0 ** (-5.0 - jnp.arange(n_heads, dtype=F32))))
    cum = jnp.cumsum(jnp.broadcast_to(log_gamma, (c, n_heads)), axis=0)
    cum_h = cum.T
    causal = jnp.tril(jnp.ones((c, c), dtype=bool))
    dmat = jnp.exp(jnp.where(causal, cum_h[:, :, None] - cum_h[:, None, :], -jnp.inf))
    ecum = jnp.repeat(jnp.exp(cum), dv, axis=1)
    wend = jnp.repeat(jnp.exp(cum[-1][None, :] - cum), dv, axis=1)
    sdec = jnp.repeat(jnp.exp(cum[-1])[None, :], dv, axis=1)
    return dmat, ecum, wend, sdec


EXPERT_FF_TILE_MAX = 1792
FFN_FF_TILE_MAX = 1408


def _largest_tile(n, cap):
    best = None
    for k in range(1, n // LANES + 1):
        if n % k == 0 and (n // k) % LANES == 0 and n // k <= cap:
            best = n // k
            break
    return n if best is None else best


def kernel(x_prompt, x_sample, state_ssd, state_ssd_conv, state_ret, state_hgrn, meta_tokens, norm_mix, norm_ffn,
           norm_final, w_in_ab, conv_w, conv_b, dt_bias, a_log, d_skip, ssd_norm, w_out_ab, w_ffn_gate, w_ffn_up,
           w_ffn_down, w_in_c, hgrn_lb, hgrn_norm, w_out_c, w_router, w_exp_gate, w_exp_up, w_exp_down):
    bp, seq, d = x_prompt.shape
    bs, dec_seq, _ = x_sample.shape
    n_meta = meta_tokens.shape[0]
    depth = norm_mix.shape[0]
    assert depth == 2 and seq % CHUNK == 0 and n_meta == N_META and d % LANES == 0
    n_ssd_heads = d_skip.shape[1]
    d_inner = n_ssd_heads * SSD_HEADDIM
    bc_w = 2 * SSD_GROUPS * SSD_STATE
    conv_dim = d_inner + bc_w
    ret_w = RET_HEADS * LANES
    assert d_inner == d and ret_w == d and conv_dim == conv_w.shape[2]

    tp, ts = bp * seq, bs * dec_seq
    t_real = tp + ts + n_meta
    tm = 512 if t_real >= 4096 else 64
    t_pad = -(-t_real // tm) * tm
    tm_big = 2 * tm if t_pad % (2 * tm) == 0 else tm
    g_prompt = _Group(0, bp, seq // CHUNK, CHUNK)
    g_sample = _Group(tp, bs, 1, dec_seq)
    g_meta = _Group(tp + ts, 1, 1, n_meta, out_rows=t_pad - tp - ts)
    assert tp % dec_seq == 0 and (tp + ts) % n_meta == 0

    w_in = w_in_ab[0]
    o_z, o_xbc, o_dt, o_q = 0, d_inner, d_inner + conv_dim, d_inner + conv_dim + n_ssd_heads
    n_cols = 6 * d + bc_w
    tn = n_cols // 2 if n_cols % (2 * LANES) == 0 else n_cols
    w_perm, w_dt = _regroup_columns(
        w_in, ((o_z, o_z + d_inner), (o_q, o_q + 4 * ret_w), (o_xbc, o_xbc + conv_dim)),
        (o_dt, o_dt + n_ssd_heads), LANES)
    proj, proj_dt, x0 = _normmm((x_prompt.reshape(tp, d), x_sample.reshape(ts, d), meta_tokens), norm_mix[0],
                                w_perm, w_dt, tm_big, tn, t_pad)
    cb_x, cb_bc = 5, (6 * d) // bc_w
    assert (6 * d) % bc_w == 0

    expand = jnp.asarray(np.arange(LANES)[:, None] == (np.arange(d_inner) // SSD_HEADDIM)[None, :], BF16)
    pad_h = LANES - n_ssd_heads
    ssd_consts = [conv_w[0], conv_b[0].reshape(1, conv_dim), jnp.pad(dt_bias[0], (0, pad_h)).reshape(1, LANES),
                  jnp.pad(a_log[0], (0, pad_h)).reshape(1, LANES),
                  jnp.repeat(d_skip[0], SSD_HEADDIM).reshape(1, d_inner), ssd_norm[0].reshape(1, d_inner), expand]
    def ssd(group, hist0, s0, y_prev):
        nb = group.nb
        return _scan_call(
            _ssd_kernel, group,
            [(proj, d_inner, 0), (proj, d_inner, cb_x), (proj, bc_w, cb_bc), (proj_dt, LANES, 0)],
            ssd_consts, [hist0, s0], y_prev, t_pad, d_inner,
            [jax.ShapeDtypeStruct((nb, n_ssd_heads, SSD_STATE, SSD_HEADDIM), F32),
             jax.ShapeDtypeStruct((nb, SSD_CONV - 1, conv_dim), F32)],
            [_per_stream((nb, n_ssd_heads, SSD_STATE, SSD_HEADDIM)), _per_stream((nb, SSD_CONV - 1, conv_dim))],
            [pltpu.VMEM((HIST_ROWS + group.c, conv_dim), F32),
             pltpu.VMEM((n_ssd_heads // 2, SSD_STATE, LANES), F32)])

    def bcast(s, n):
        return jnp.broadcast_to(s, (n,) + s.shape[1:])

    zeros = jnp.zeros
    y_ssd, s_m, h_m = ssd(g_meta, zeros((1, SSD_CONV - 1, conv_dim), F32),
                          zeros((1, n_ssd_heads, SSD_STATE, SSD_HEADDIM), F32), None)
    y_ssd, ssd_p, conv_p = ssd(g_prompt, bcast(h_m, bp), bcast(s_m, bp), y_ssd)
    y_ssd, ssd_s, conv_s = ssd(g_sample, state_ssd_conv[0], state_ssd[0], y_ssd)

    def ret(group, pos, s0, y_prev):
        nb = group.nb
        sin, cos = _rotation_tables(pos, LANES)
        dmat, ecum, wend, sdec = _retention_decay(group.c, RET_HEADS, LANES)
        return _scan_call(
            _ret_kernel, group,
            [(proj, ret_w, 1), (proj, ret_w, 2), (proj, ret_w, 3), (proj, ret_w, 4)],
            [(sin, group.chunk_rows(LANES)), (cos, group.chunk_rows(LANES)), dmat, ecum, wend, sdec],
            [s0], y_prev, t_pad, ret_w,
            [jax.ShapeDtypeStruct((nb, RET_HEADS, LANES, LANES), F32)],
            [_per_stream((nb, RET_HEADS, LANES, LANES))], [])

    y_ret, r_m = ret(g_meta, jnp.arange(n_meta), zeros((1, RET_HEADS, LANES, LANES), F32), None)
    y_ret, ret_p = ret(g_prompt, n_meta + jnp.arange(seq), bcast(r_m, bp), y_ret)
    y_ret, ret_s = ret(g_sample, n_meta + PAST_LEN + jnp.arange(dec_seq), state_ret[0], y_ret)

    w_out = w_out_ab[0].astype(BF16)
    ff = w_ffn_gate.shape[2]
    tf = _largest_tile(ff, FFN_FF_TILE_MAX)
    x2 = _ffn(x0, [y_ssd, y_ret], [w_out[:d_inner], w_out[d_inner:]], norm_ffn[0], w_ffn_gate[0].astype(BF16),
              w_ffn_up[0].astype(BF16), w_ffn_down[0].astype(BF16), tm, tf)

    w_c = w_in_c[0].astype(BF16)
    w_qig = jnp.concatenate([w_c[:, :d], w_c[:, 2 * d:]], axis=1)
    proj_c, proj_f = _normmm(x2, norm_mix[1], w_qig, w_c[:, d:2 * d], tm_big, 3 * d)
    lb_soft = jax.nn.softmax(hgrn_lb.astype(F32), axis=0)
    lb = (jnp.cumsum(lb_soft, axis=0) - lb_soft[0])[1].reshape(1, d)
    n_hg = d // HG_DK
    hg_consts = [lb, hgrn_norm[0].reshape(1, LANES)]

    def hgrn(group, s0, y_prev):
        nb = group.nb
        return _scan_call(
            _hgrn_kernel, group,
            [(proj_c, d, 0), (proj_f, d, 0), (proj_c, d, 1), (proj_c, d, 2)],
            hg_consts, [s0], y_prev, t_pad, d,
            [jax.ShapeDtypeStruct((nb, n_hg, HG_DK, LANES), F32)],
            [_per_stream((nb, n_hg, HG_DK, LANES))], [pltpu.VMEM((n_hg, LANES, HG_DK), F32)])

    y_hg, g_m = hgrn(g_meta, zeros((1, n_hg, HG_DK, LANES), F32), None)
    y_hg, hg_p = hgrn(g_prompt, bcast(g_m, bp), y_hg)
    y_hg, hg_s = hgrn(g_sample, state_hgrn[0], y_hg)

    n_exp = w_router.shape[2]
    x3, route, route_t, counts = _router(x2, y_hg, w_out_c[0].astype(BF16), norm_ffn[1], w_router[0], tm_big, tm // 2, t_real)
    tmb = tm
    n_blocks = (TOP_K * t_real + n_exp * (tmb - 1)) // tmb
    pos, blk_e, blk_n = _moe_plan(route_t, counts, n_exp, tmb, n_blocks, TOP_K, t_real)
    xs = _dispatch(pos, x3, n_blocks * tmb + TOP_K * (t_pad - t_real), tm, TOP_K)
    ffe = w_exp_gate.shape[3]
    tfe = _largest_tile(ffe, EXPERT_FF_TILE_MAX)
    ys = _experts(blk_e, blk_n, xs, norm_ffn[1], w_exp_gate[0].astype(BF16), w_exp_up[0].astype(BF16),
                  w_exp_down[0].astype(BF16), tmb, tfe)
    y_prompt, y_sample = _combine(pos, x3, route, norm_final, ys, tm, TOP_K, tp, ts)
    y_prompt = y_prompt.reshape(bp, seq, d)
    y_sample = y_sample.reshape(bs, dec_seq, d)
    return (y_prompt, y_sample, ssd_p[None], conv_p[None], ret_p[None], hg_p[None],
            ssd_s[None], conv_s[None], ret_s[None], hg_s[None])
```
